```python
import jax, jax.numpy as jnp
from jax import lax
import numpy as np

D_MODEL = 1024
BATCH = 8
SEQ = 4096
DEPTH = 4

GRID_W = 64
CTX_LEN = 256
EPS = 1e-6

FNET_GROUPS = 4
FNET_GROUP_DIM = 128
FNET_WIDTH = FNET_GROUPS * FNET_GROUP_DIM
HG_HEADS = 4
HG_DK = 128
HG_DV = 128
HG_KW = HG_HEADS * HG_DK
HG_VW = HG_HEADS * HG_DV
HG_CHUNK = 64
AB_SPLITS = [FNET_WIDTH, FNET_WIDTH + HG_KW, FNET_WIDTH + 2 * HG_KW, FNET_WIDTH + 3 * HG_KW,
             FNET_WIDTH + 3 * HG_KW + HG_VW]
AB_IN = FNET_WIDTH + 3 * HG_KW + 2 * HG_VW
AB_OUT = FNET_WIDTH + HG_VW
ATT_HEADS = 8
ATT_KV_HEADS = 2
ATT_GROUP = ATT_HEADS // ATT_KV_HEADS
HEAD_DIM = 128
ATT_WIDTH = ATT_HEADS * HEAD_DIM
ATT_KV_WIDTH = ATT_KV_HEADS * HEAD_DIM
ATT_IN = ATT_WIDTH + 2 * ATT_KV_WIDTH
Q_BLOCK = 128
ROPE_THETA = 10000.0
D_FF = 2816
CONV_K = 3

N_EVEN = (DEPTH + 1) // 2
N_ODD = DEPTH // 2

kernel_name = "hybrid_fourier_hgrn2_gqa_prefix_dit"


def rms_norm(x, gain=None):
    x32 = x.astype(jnp.float32)
    y = x32 * lax.rsqrt(jnp.mean(x32 * x32, axis=-1, keepdims=True) + EPS)
    if gain is not None:
        y = y * gain.astype(jnp.float32)
    return y.astype(x.dtype)


def modulate(x, shift, scale):
    return rms_norm(x) * (1 + scale) + shift


def modulation(cond, w_mod, b_mod):
    m = jax.nn.silu(cond) @ w_mod + b_mod
    return jnp.split(m, 6, axis=-1)


def axial_rope(x):
    L = x.shape[1]
    t = jnp.arange(L)
    row = (t // GRID_W).astype(jnp.float32)
    col = (t % GRID_W).astype(jnp.float32)
    n_freq = HEAD_DIM // 4
    freqs = ROPE_THETA ** (-jnp.arange(n_freq, dtype=jnp.float32) / n_freq)
    ang = jnp.concatenate([row[:, None] * freqs, col[:, None] * freqs], axis=-1)
    cos = jnp.cos(ang)[None, :, None, :]
    sin = jnp.sin(ang)[None, :, None, :]
    xp = x.astype(jnp.float32).reshape(*x.shape[:-1], HEAD_DIM // 2, 2)
    x0, x1 = xp[..., 0], xp[..., 1]
    out = jnp.stack([x0 * cos - x1 * sin, x0 * sin + x1 * cos], axis=-1).reshape(x.shape)
    return out.astype(x.dtype)


def block_attention(q, k, v):
    B, Lq = q.shape[:2]
    nb = Lq // Q_BLOCK
    qb = q.reshape(B, nb, Q_BLOCK, ATT_KV_HEADS, ATT_GROUP, HEAD_DIM).transpose(1, 0, 2, 3, 4, 5)
    scale = HEAD_DIM ** -0.5

    def one_block(qi):
        s = jnp.einsum('bqhgd,bkhd->bhgqk', qi, k, preferred_element_type=jnp.float32) * scale
        p = jax.nn.softmax(s, axis=-1)
        return jnp.einsum('bhgqk,bkhd->bqhgd', p.astype(v.dtype), v)

    o = lax.map(one_block, qb)
    return o.transpose(1, 0, 2, 3, 4, 5).reshape(B, Lq, ATT_WIDTH)


def gla_chunk_scan(q, k, v, log_f, s0):
    B, L, H, _ = q.shape
    DV = v.shape[-1]
    n = L // HG_CHUNK

    def to_chunks(t):
        return t.reshape(B, n, HG_CHUNK, H, t.shape[-1]).transpose(1, 0, 3, 2, 4)

    lower = jnp.tril(jnp.ones((HG_CHUNK, HG_CHUNK), dtype=bool))[None, None, :, :, None]

    def step(S, inp):
        qc, kc, vc, lfc = inp
        b = jnp.cumsum(lfc, axis=2)
        inter = jnp.einsum('bhtk,bhkv->bhtv', qc * jnp.exp(b), S)
        rel = jnp.where(lower, b[:, :, :, None, :] - b[:, :, None, :, :], -jnp.inf)
        att = jnp.einsum('bhtk,bhtsk,bhsk->bhts', qc, jnp.exp(rel), kc)
        o = inter + jnp.einsum('bhts,bhsv->bhtv', att, vc)
        b_last = b[:, :, -1:, :]
        S = jnp.exp(b_last[:, :, 0, :, None]) * S + jnp.einsum('bhsk,bhsv->bhkv', kc * jnp.exp(b_last - b), vc)
        return S, o

    S, o = lax.scan(step, s0, (to_chunks(q), to_chunks(k), to_chunks(v), to_chunks(log_f)))
    return o.transpose(1, 0, 3, 2, 4).reshape(B, L, H, DV), S


def fourier_mix(a):
    B, L, _ = a.shape
    ag = a.astype(jnp.float32).reshape(B, L, FNET_GROUPS, FNET_GROUP_DIM)
    y = jnp.fft.fft2(ag, axes=(1, 3), norm='ortho').real
    return y.reshape(B, L, FNET_WIDTH).astype(a.dtype)


def fourier_hgrn_mixer(h_lat, h_ctx, w_in, w_out, lb, gn_gain, need_ctx):
    B = h_lat.shape[0]

    def parts(h):
        p = h @ w_in
        a, q, zf0, zf1, i, g = jnp.split(p, AB_SPLITS, axis=-1)
        heads = lambda t, d: t.reshape(*t.shape[:2], HG_HEADS, d).astype(jnp.float32)
        return a, heads(jax.nn.silu(q), HG_DK), (heads(zf0, HG_DK), heads(zf1, HG_DK)), heads(i, HG_DV), g

    a_l, q_l, zf_l, v_l, g_l = parts(h_lat)
    a_c, q_c, zf_c, v_c, g_c = parts(h_ctx)

    def forget(z, lb_d):
        f = lb_d + (1.0 - lb_d) * jax.nn.sigmoid(z)
        return jnp.log(f), 1.0 - f

    outs_l, outs_c = [], []
    for d in range(2):
        lb_d = lb[d].reshape(HG_HEADS, HG_DK).astype(jnp.float32)
        lf_l, k_l = forget(zf_l[d], lb_d)
        lf_c, k_c = forget(zf_c[d], lb_d)
        seq_c = [q_c, k_c, v_c, lf_c]
        seq_l = [q_l, k_l, v_l, lf_l]
        if d == 1:
            seq_c = [jnp.flip(t, axis=1) for t in seq_c]
            seq_l = [jnp.flip(t, axis=1) for t in seq_l]
        s0 = jnp.zeros((B, HG_HEADS, HG_DK, HG_DV), jnp.float32)
        oc, s_ctx = gla_chunk_scan(*seq_c, s0)
        ol, _ = gla_chunk_scan(*seq_l, s_ctx)
        if d == 1:
            oc, ol = jnp.flip(oc, axis=1), jnp.flip(ol, axis=1)
        outs_c.append(oc)
        outs_l.append(ol)

    def gated_out(o, g):
        o = rms_norm(o, gn_gain.reshape(HG_HEADS, HG_DV))
        return (o.reshape(*o.shape[:2], HG_VW) * jax.nn.silu(g.astype(jnp.float32))).astype(g.dtype)

    y_lat = jnp.concatenate([fourier_mix(a_l), gated_out(outs_l[0] + outs_l[1], g_l)], axis=-1) @ w_out
    y_ctx = None
    if need_ctx:
        y_ctx = jnp.concatenate([fourier_mix(a_c), gated_out(outs_c[0] + outs_c[1], g_c)], axis=-1) @ w_out
    return y_lat, y_ctx


def attention_mixer(h_lat, h_ctx, w_qkv, qn_g, kn_g, w_out, need_ctx):
    def qkv(h, rope):
        B, L, _ = h.shape
        p = h @ w_qkv
        q, k, v = jnp.split(p, [ATT_WIDTH, ATT_WIDTH + ATT_KV_WIDTH], axis=-1)
        q = rms_norm(q.reshape(B, L, ATT_HEADS, HEAD_DIM), qn_g)
        k = rms_norm(k.reshape(B, L, ATT_KV_HEADS, HEAD_DIM), kn_g)
        v = v.reshape(B, L, ATT_KV_HEADS, HEAD_DIM)
        if rope:
            q, k = axial_rope(q), axial_rope(k)
        return q, k, v

    q_l, k_l, v_l = qkv(h_lat, True)
    q_c, k_c, v_c = qkv(h_ctx, False)
    o_l = block_attention(q_l, jnp.concatenate([k_l, k_c], axis=1), jnp.concatenate([v_l, v_c], axis=1))
    y_lat = o_l @ w_out
    y_ctx = None
    if need_ctx:
        y_ctx = block_attention(q_c, k_c, v_c) @ w_out
    return y_lat, y_ctx


def conv_ffn(h, w_up, conv_w, conv_b, w_down, on_grid):
    B, L, _ = h.shape
    gate, val = jnp.split(h @ w_up, [D_FF], axis=-1)
    w = conv_w.astype(h.dtype)
    if on_grid:
        rows = L // GRID_W
        g2 = gate.reshape(B, rows, GRID_W, D_FF)
        conv = lax.conv_general_dilated(g2, w[:, :, None, :], (1, 1), 'SAME',
                                        dimension_numbers=('NHWC', 'HWIO', 'NHWC'),
                                        feature_group_count=D_FF).reshape(B, L, D_FF)
    else:
        conv = lax.conv_general_dilated(gate, w[CONV_K // 2][:, None, :], (1,), 'SAME',
                                        dimension_numbers=('NWC', 'WIO', 'NWC'),
                                        feature_group_count=D_FF)
    act = jax.nn.silu(conv + conv_b.astype(h.dtype)) * val
    return act @ w_down


def setup_inputs(seed: int = 0) -> dict:
    key = jax.random.key(seed)
    ks = jax.random.split(key, 20)
    nrm = lambda k, shape, scale: jax.random.normal(k, shape, jnp.float32) * scale
    return {
        "x": nrm(ks[0], (BATCH, SEQ, D_MODEL), 1.0),
        "c": nrm(ks[1], (BATCH, D_MODEL), 1.0),
        "ctx": nrm(ks[2], (BATCH, CTX_LEN, D_MODEL), 1.0),
        "c_ctx": nrm(ks[3], (D_MODEL,), 1.0),
        "w_mod": nrm(ks[4], (DEPTH, D_MODEL, 6 * D_MODEL), D_MODEL ** -0.5),
        "b_mod": nrm(ks[5], (DEPTH, 6 * D_MODEL), 0.01),
        "w_in_ab": nrm(ks[6], (N_EVEN, D_MODEL, AB_IN), D_MODEL ** -0.5),
        "w_out_ab": nrm(ks[7], (N_EVEN, AB_OUT, D_MODEL), AB_OUT ** -0.5),
        "hg_lb_logits": nrm(ks[8], (N_EVEN, 2, HG_KW), 0.5),
        "hg_norm_g": 1.0 + nrm(ks[9], (N_EVEN, HG_VW), 0.02),
        "w_qkv": nrm(ks[10], (N_ODD, D_MODEL, ATT_IN), D_MODEL ** -0.5),
        "q_norm_g": 1.0 + nrm(ks[11], (N_ODD, HEAD_DIM), 0.02),
        "k_norm_g": 1.0 + nrm(ks[12], (N_ODD, HEAD_DIM), 0.02),
        "w_out_att": nrm(ks[13], (N_ODD, ATT_WIDTH, D_MODEL), ATT_WIDTH ** -0.5),
        "w_up": nrm(ks[14], (DEPTH, D_MODEL, 2 * D_FF), D_MODEL ** -0.5),
        "conv_w": nrm(ks[15], (DEPTH, CONV_K, CONV_K, D_FF), 1.0 / CONV_K),
        "conv_b": nrm(ks[16], (DEPTH, D_FF), 0.01),
        "w_down": nrm(ks[17], (DEPTH, D_FF, D_MODEL), D_FF ** -0.5),
        "final_norm_g": 1.0 + nrm(ks[18], (D_MODEL,), 0.02),
    }


def reference(x, c, ctx, c_ctx, w_mod, b_mod, w_in_ab, w_out_ab, hg_lb_logits, hg_norm_g,
              w_qkv, q_norm_g, k_norm_g, w_out_att, w_up, conv_w, conv_b, w_down, final_norm_g):
    lbp = jax.nn.softmax(hg_lb_logits.astype(jnp.float32), axis=0)
    lower_bounds = jnp.cumsum(lbp, axis=0) - lbp[0]

    for layer in range(DEPTH):
        last = layer == DEPTH - 1
        sh1, sc1, g1, sh2, sc2, g2 = [t[:, None, :] for t in modulation(c, w_mod[layer], b_mod[layer])]
        ch1, cs1, cg1, ch2, cs2, cg2 = modulation(c_ctx, w_mod[layer], b_mod[layer])
        h_lat = modulate(x, sh1, sc1)
        h_ctx = modulate(ctx, ch1, cs1)
        if layer % 2 == 0:
            e = layer // 2
            y_lat, y_ctx = fourier_hgrn_mixer(h_lat, h_ctx, w_in_ab[e], w_out_ab[e], lower_bounds[e],
                                              hg_norm_g[e], not last)
        else:
            o = layer // 2
            y_lat, y_ctx = attention_mixer(h_lat, h_ctx, w_qkv[o], q_norm_g[o], k_norm_g[o],
                                           w_out_att[o], not last)
        x = x + g1 * y_lat
        x = x + g2 * conv_ffn(modulate(x, sh2, sc2), w_up[layer], conv_w[layer], conv_b[layer],
                              w_down[layer], True)
        if not last:
            ctx = ctx + cg1 * y_ctx
            ctx = ctx + cg2 * conv_ffn(modulate(ctx, ch2, cs2), w_up[layer], conv_w[layer], conv_b[layer],
                                       w_down[layer], False)
    return rms_norm(x, final_norm_g)
```

```python
import functools
import math

import numpy as np
import jax
import jax.numpy as jnp
from jax import lax
from jax.experimental import pallas as pl
from jax.experimental.pallas import tpu as pltpu

F32 = jnp.float32
BF = jnp.bfloat16

EPS = 1e-6
GRID_W = 64
LANE = 128
HEAD_DIM = 128
FNET_GROUPS = 4
HG_HEADS = 4
HG_W = HG_HEADS * HEAD_DIM
ATT_HEADS = 8
ATT_KV_HEADS = 2
ATT_GROUP = ATT_HEADS // ATT_KV_HEADS
ROPE_THETA = 10000.0
SCAN_CHUNK = 64
VMEM_LIMIT = 56 * 1024 * 1024


def _cparams(*sem):
    return pltpu.CompilerParams(dimension_semantics=sem, vmem_limit_bytes=VMEM_LIMIT)


def _dot(a, b):
    return jnp.dot(a, b, preferred_element_type=F32)


def _dot_nt(a, b):
    return lax.dot_general(a, b, (((1,), (1,)), ((), ())), preferred_element_type=F32)


def _dot_tn(a, b):
    return lax.dot_general(a, b, (((0,), (0,)), ((), ())), preferred_element_type=F32)


def _rms(x):
    return x * lax.rsqrt(jnp.mean(x * x, axis=-1, keepdims=True) + EPS)


def _silu(x):
    return x * jax.nn.sigmoid(x)


def _modulate(x, shift, scale):
    return (_rms(x) * (1.0 + scale) + shift).astype(BF)


def _row_tile(n, pref):
    return pref if n % pref == 0 else n


def _mod_body(c_ref, w_ref, b_ref, o_ref):
    s = _silu(c_ref[...]).astype(BF)
    o_ref[...] = _dot(s, w_ref[...].astype(BF)) + b_ref[...]


def _modulation(cc, w_mod, b_mod):
    depth, d, six_d = w_mod.shape
    r = cc.shape[0]
    return pl.pallas_call(
        _mod_body,
        grid=(depth, six_d // d),
        in_specs=[pl.BlockSpec((r, d), lambda l, j: (0, 0)),
                  pl.BlockSpec((None, d, d), lambda l, j: (l, 0, j)),
                  pl.BlockSpec((None, 1, d), lambda l, j: (l, 0, j))],
        out_specs=pl.BlockSpec((None, r, d), lambda l, j: (l, 0, j)),
        out_shape=jax.ShapeDtypeStruct((depth, r, six_d), F32),
        compiler_params=_cparams("parallel", "parallel"),
        name="modulation",
    )(cc, w_mod, b_mod.reshape(depth, 1, six_d))


def _proj_even_body(e, n_even, x_ref, mod_ref, w_ref, lbl_ref, a_ref, q_ref, lf_ref, k_ref, v_ref, g_ref):
    mod = mod_ref[...]
    h = _modulate(x_ref[...], mod[0:1], mod[1:2])
    rows = [lbl_ref[i:i + 1, :] for i in range(n_even)]
    mx = functools.reduce(jnp.maximum, rows)
    ex = [jnp.exp(r - mx) for r in rows]
    den = functools.reduce(lambda a, b: a + b, ex)
    lb = jnp.zeros_like(mx)
    for i in range(1, e + 1):
        lb = lb + ex[i] / den
    w = HG_W
    a_ref[...] = _dot(h, w_ref[:, 0:w]).astype(BF)
    q_ref[...] = _silu(_dot(h, w_ref[:, w:2 * w])).astype(BF)
    for d in range(2):
        z = _dot(h, w_ref[:, (2 + d) * w:(3 + d) * w])
        lbd = lb[:, d * w:(d + 1) * w]
        f = lbd + (1.0 - lbd) * jax.nn.sigmoid(z)
        lf_ref[:, d * w:(d + 1) * w] = jnp.log(f)
        k_ref[:, d * w:(d + 1) * w] = (1.0 - f).astype(BF)
    v_ref[...] = _dot(h, w_ref[:, 4 * w:5 * w]).astype(BF)
    g_ref[...] = _dot(h, w_ref[:, 5 * w:6 * w]).astype(BF)


def _proj_even(x, mod, w_in, lb_logits, e):
    b, l, d = x.shape
    n_even = lb_logits.shape[0]
    tm = _row_tile(l, 512)
    w = HG_W
    row = lambda width: pl.BlockSpec((None, tm, width), lambda i, t: (i, t, 0))
    shp = lambda width, dt: jax.ShapeDtypeStruct((b, l, width), dt)
    return pl.pallas_call(
        functools.partial(_proj_even_body, e, n_even),
        grid=(b, l // tm),
        in_specs=[row(d),
                  pl.BlockSpec((None, 6, d), lambda i, t: (i, 0, 0)),
                  pl.BlockSpec(w_in.shape, lambda i, t: (0, 0)),
                  pl.BlockSpec(lb_logits.shape, lambda i, t: (0, 0))],
        out_specs=[row(w), row(w), row(2 * w), row(2 * w), row(w), row(w)],
        out_shape=[shp(w, BF), shp(w, BF), shp(2 * w, F32), shp(2 * w, BF), shp(w, BF), shp(w, BF)],
        compiler_params=_cparams("parallel", "parallel"),
        name="proj_even",
    )(x, mod, w_in, lb_logits)


@functools.lru_cache(maxsize=None)
def _dft_tables(n):
    idx = np.arange(n, dtype=np.int64)
    ang = 2.0 * np.pi * ((idx[:, None] * idx[None, :]) % n).astype(np.float64) / n
    s = 1.0 / math.sqrt(n)
    return (np.cos(ang) * s).astype(np.float32), (np.sin(ang) * s).astype(np.float32)


def _dft_body(c_ref, s_ref, a_ref, cs_ref, o_ref):
    a = a_ref[...]
    p = _dot(c_ref[...], a).astype(BF)
    q = _dot(s_ref[...], a).astype(BF)
    cs = cs_ref[...]
    for g in range(FNET_GROUPS):
        sl = slice(g * LANE, (g + 1) * LANE)
        pq = jnp.concatenate([p[:, sl], q[:, sl]], axis=1)
        o_ref[:, sl] = _dot(pq, cs).astype(BF)


def _dft_mix(a):
    b, l, w = a.shape
    cl, sl_ = _dft_tables(l)
    cc, sc = _dft_tables(LANE)
    cos_l = jnp.asarray(cl).astype(BF)
    sin_l = jnp.asarray(sl_).astype(BF)
    cs = jnp.concatenate([jnp.asarray(cc), -jnp.asarray(sc)], axis=0).astype(BF)
    tm = _row_tile(l, 512)
    return pl.pallas_call(
        _dft_body,
        grid=(l // tm, b),
        in_specs=[pl.BlockSpec((tm, l), lambda m, i: (m, 0)),
                  pl.BlockSpec((tm, l), lambda m, i: (m, 0)),
                  pl.BlockSpec((None, l, w), lambda m, i: (i, 0, 0)),
                  pl.BlockSpec(cs.shape, lambda m, i: (0, 0))],
        out_specs=pl.BlockSpec((None, tm, w), lambda m, i: (i, m, 0)),
        out_shape=jax.ShapeDtypeStruct((b, l, w), BF),
        compiler_params=_cparams("parallel", "parallel"),
        name="dft_mix",
    )(cos_l, sin_l, a, cs)


def _scan_chunk(rev, q_ref, k_ref, lf_ref, v_ref, o_ref, st_ref):
    c, w = lf_ref.shape
    nlev = c.bit_length() - 1
    lf = lf_ref[...]
    r = lax.broadcasted_iota(jnp.int32, (c, w), 0)
    pos = (c - 1 - r) if rev else r

    def from_prev(x, sh):
        return pltpu.roll(x, (c - sh) if rev else sh, axis=0)

    def from_next(x, sh):
        return pltpu.roll(x, sh if rev else (c - sh), axis=0)

    b = lf
    for j in range(nlev):
        sh = 1 << j
        b = b + jnp.where(pos >= sh, from_prev(b, sh), 0.0)

    q = q_ref[...].astype(F32)
    k = k_ref[...].astype(F32)
    ti = lax.broadcasted_iota(jnp.int32, (c, c), 0)
    si = lax.broadcasted_iota(jnp.int32, (c, c), 1)
    if rev:
        ti, si = c - 1 - ti, c - 1 - si
    att = [jnp.zeros((c, c), F32) for _ in range(HG_HEADS)]
    y = b
    for j in range(nlev):
        sh = 1 << j
        upper = ((pos >> j) & 1) == 1
        bound = jnp.where(upper, from_prev(y, sh), y)
        dlt = b - bound
        e = jnp.exp(jnp.where(upper, dlt, -dlt))
        qm = jnp.where(upper, q * e, 0.0).astype(BF)
        km = jnp.where(upper, 0.0, k * e).astype(BF)
        same = (ti >> (j + 1)) == (si >> (j + 1))
        for h in range(HG_HEADS):
            sl = slice(h * LANE, (h + 1) * LANE)
            att[h] = att[h] + jnp.where(same, _dot_nt(qm[:, sl], km[:, sl]), 0.0)
        y = jnp.where(upper, y, from_next(y, sh))
    b_last = y
    qe = (q * jnp.exp(b)).astype(BF)
    kd = (k * jnp.exp(b_last - b)).astype(BF)
    carry = jnp.exp(b_last[0:1, :])
    v = v_ref[...]
    vf = v.astype(F32)
    for h in range(HG_HEADS):
        sl = slice(h * LANE, (h + 1) * LANE)
        st = st_ref[h]
        inter = _dot_nt(qe[:, sl], st.astype(BF))
        diag = jnp.sum(q[:, sl] * k[:, sl], axis=-1, keepdims=True)
        o_ref[:, sl] = inter + _dot(att[h].astype(BF), v[:, sl]) + diag * vf[:, sl]
        st_ref[h] = st * carry[:, sl] + _dot_tn(v[:, sl], kd[:, sl])


def _scan_body(rev, n_ctx, qc, kc, lfc, vc, ql, kl, lfl, vl, oc, ol, st_ref):
    s = pl.program_id(1)

    @pl.when(s == 0)
    def _():
        st_ref[...] = jnp.zeros_like(st_ref)

    @pl.when(s < n_ctx)
    def _():
        _scan_chunk(rev, qc, kc, lfc, vc, oc, st_ref)

    @pl.when(s >= n_ctx)
    def _():
        _scan_chunk(rev, ql, kl, lfl, vl, ol, st_ref)


def _scan(rev, pc, pl_):
    qc, lfc, kc, vc = pc
    ql, lfl, kl, vl = pl_
    b, lc, w = qc.shape
    ll = ql.shape[1]
    c = SCAN_CHUNK
    n_ctx, n_lat = lc // c, ll // c
    d = 1 if rev else 0

    if rev:
        ci = lambda s: jnp.maximum(n_ctx - 1 - s, 0)
        li = lambda s: n_lat - 1 - jnp.maximum(s - n_ctx, 0)
    else:
        ci = lambda s: jnp.minimum(s, n_ctx - 1)
        li = lambda s: jnp.maximum(s - n_ctx, 0)

    def spec(idx, col):
        return pl.BlockSpec((None, c, w), lambda i, s: (i, idx(s), col))

    return pl.pallas_call(
        functools.partial(_scan_body, rev, n_ctx),
        grid=(b, n_ctx + n_lat),
        in_specs=[spec(ci, 0), spec(ci, d), spec(ci, d), spec(ci, 0),
                  spec(li, 0), spec(li, d), spec(li, d), spec(li, 0)],
        out_specs=[spec(ci, 0), spec(li, 0)],
        out_shape=[jax.ShapeDtypeStruct((b, lc, w), F32), jax.ShapeDtypeStruct((b, ll, w), F32)],
        scratch_shapes=[pltpu.VMEM((HG_HEADS, HEAD_DIM, HEAD_DIM), F32)],
        compiler_params=_cparams("parallel", "arbitrary"),
        name="hgrn_scan_bwd" if rev else "hgrn_scan_fwd",
    )(qc, kc, lfc, vc, ql, kl, lfl, vl)


def _out_even_body(yf_ref, of_ref, ob_ref, g_ref, x_ref, mod_ref, gn_ref, w_ref, o_ref):
    o = of_ref[...] + ob_ref[...]
    g = g_ref[...].astype(F32)
    gn = gn_ref[...]
    parts = []
    for h in range(HG_HEADS):
        sl = slice(h * LANE, (h + 1) * LANE)
        parts.append((_rms(o[:, sl]) * gn[:, sl] * _silu(g[:, sl])).astype(BF))
    gated = jnp.concatenate(parts, axis=1)
    w = HG_W
    y = _dot(yf_ref[...], w_ref[0:w, :]) + _dot(gated, w_ref[w:2 * w, :])
    o_ref[...] = x_ref[...] + mod_ref[2:3, :] * y


def _out_even(yf, o_f, o_b, g, x, mod, gn, w_out):
    b, l, d = x.shape
    tm = _row_tile(l, 512)
    row = lambda width: pl.BlockSpec((None, tm, width), lambda i, t: (i, t, 0))
    return pl.pallas_call(
        _out_even_body,
        grid=(b, l // tm),
        in_specs=[row(HG_W), row(HG_W), row(HG_W), row(HG_W), row(d),
                  pl.BlockSpec((None, 6, d), lambda i, t: (i, 0, 0)),
                  pl.BlockSpec(gn.shape, lambda i, t: (0, 0)),
                  pl.BlockSpec(w_out.shape, lambda i, t: (0, 0))],
        out_specs=row(d),
        out_shape=jax.ShapeDtypeStruct((b, l, d), F32),
        compiler_params=_cparams("parallel", "parallel"),
        name="out_even",
    )(yf, o_f, o_b, g, x, mod, gn, w_out)


def _proj_att_body(x_ref, mod_ref, w_ref, qg_ref, kg_ref, cos_ref, sin_ref, q_ref, k_ref, v_ref):
    mod = mod_ref[...]
    h = _modulate(x_ref[...], mod[0:1], mod[1:2])
    cos = cos_ref[...]
    sin = sin_ref[...]
    scale = HEAD_DIM ** -0.5
    for i in range(ATT_HEADS + ATT_KV_HEADS):
        sl = slice(i * LANE, (i + 1) * LANE)
        p = _dot(h, w_ref[:, sl])
        gain = qg_ref[...] if i < ATT_HEADS else kg_ref[...]
        p = _rms(p) * gain
        p = p * cos + pltpu.roll(p, HEAD_DIM // 2, axis=1) * sin
        if i < ATT_HEADS:
            q_ref[:, sl] = (p * scale).astype(BF)
        else:
            k_ref[:, (i - ATT_HEADS) * LANE:(i - ATT_HEADS + 1) * LANE] = p.astype(BF)
    nqk = (ATT_HEADS + ATT_KV_HEADS) * LANE
    v_ref[...] = _dot(h, w_ref[:, nqk:nqk + ATT_KV_HEADS * LANE]).astype(BF)


def _proj_att(x, mod, w_qkv, qg, kg, cos, sin):
    b, l, d = x.shape
    tm = _row_tile(l, 512)
    qw, kw = ATT_HEADS * LANE, ATT_KV_HEADS * LANE
    row = lambda width: pl.BlockSpec((None, tm, width), lambda i, t: (i, t, 0))
    const = lambda a: pl.BlockSpec(a.shape, lambda i, t: (0, 0))
    tab = pl.BlockSpec((tm, LANE), lambda i, t: (t, 0))
    return pl.pallas_call(
        _proj_att_body,
        grid=(b, l // tm),
        in_specs=[row(d), pl.BlockSpec((None, 6, d), lambda i, t: (i, 0, 0)),
                  const(w_qkv), const(qg), const(kg), tab, tab],
        out_specs=[row(qw), row(kw), row(kw)],
        out_shape=[jax.ShapeDtypeStruct((b, l, qw), BF), jax.ShapeDtypeStruct((b, l, kw), BF),
                   jax.ShapeDtypeStruct((b, l, kw), BF)],
        compiler_params=_cparams("parallel", "parallel"),
        name="proj_att",
    )(x, mod, w_qkv, qg, kg, cos, sin)


@functools.lru_cache(maxsize=None)
def _rope_tables(l):
    t = np.arange(l)
    row = (t // GRID_W).astype(np.float64)
    col = (t % GRID_W).astype(np.float64)
    n_freq = HEAD_DIM // 4
    freqs = ROPE_THETA ** (-np.arange(n_freq, dtype=np.float64) / n_freq)
    ang = np.concatenate([row[:, None] * freqs, col[:, None] * freqs], axis=-1)
    cos, sin = np.cos(ang), np.sin(ang)
    return (np.concatenate([cos, cos], axis=-1).astype(np.float32),
            np.concatenate([-sin, sin], axis=-1).astype(np.float32))


def _attn_body(n_src, q_ref, *refs):
    o_ref = refs[2 * n_src]
    tq = q_ref.shape[0]
    q = q_ref[...]
    qs = jnp.concatenate([q[:, g * LANE:(g + 1) * LANE] for g in range(ATT_GROUP)], axis=0)
    ss = [_dot_nt(qs, refs[2 * i][...]) for i in range(n_src)]
    m = functools.reduce(jnp.maximum, [jnp.max(s, axis=-1, keepdims=True) for s in ss])
    ps = [jnp.exp(s - m) for s in ss]
    den = functools.reduce(lambda a, b: a + b, [jnp.sum(p, axis=-1, keepdims=True) for p in ps])
    acc = functools.reduce(lambda a, b: a + b,
                           [_dot(p.astype(BF), refs[2 * i + 1][...]) for i, p in enumerate(ps)])
    o = acc * (1.0 / den)
    for g in range(ATT_GROUP):
        o_ref[:, g * LANE:(g + 1) * LANE] = o[g * tq:(g + 1) * tq].astype(BF)


def _attention(q, kvs, tq):
    b, lq, qw = q.shape
    gw = ATT_GROUP * LANE
    in_specs = [pl.BlockSpec((None, tq, gw), lambda i, h, t: (i, t, h))]
    args = [q]
    for k, v in kvs:
        lk = k.shape[1]
        for a in (k, v):
            in_specs.append(pl.BlockSpec((None, lk, LANE), lambda i, h, t: (i, 0, h)))
            args.append(a)
    return pl.pallas_call(
        functools.partial(_attn_body, len(kvs)),
        grid=(b, ATT_KV_HEADS, lq // tq),
        in_specs=in_specs,
        out_specs=pl.BlockSpec((None, tq, gw), lambda i, h, t: (i, t, h)),
        out_shape=jax.ShapeDtypeStruct((b, lq, qw), BF),
        compiler_params=_cparams("parallel", "parallel", "parallel"),
        name="attention",
    )(*args)


def _out_att_body(o_ref, x_ref, mod_ref, w_ref, y_ref):
    y_ref[...] = x_ref[...] + mod_ref[2:3, :] * _dot(o_ref[...], w_ref[...])


def _out_att(o, x, mod, w_out):
    b, l, d = x.shape
    tm = _row_tile(l, 512)
    row = lambda width: pl.BlockSpec((None, tm, width), lambda i, t: (i, t, 0))
    return pl.pallas_call(
        _out_att_body,
        grid=(b, l // tm),
        in_specs=[row(o.shape[-1]), row(d), pl.BlockSpec((None, 6, d), lambda i, t: (i, 0, 0)),
                  pl.BlockSpec(w_out.shape, lambda i, t: (0, 0))],
        out_specs=row(d),
        out_shape=jax.ShapeDtypeStruct((b, l, d), F32),
        compiler_params=_cparams("parallel", "parallel"),
        name="out_att",
    )(o, x, mod, w_out)


def _ffn_body(on_grid, n_chunk, *refs):
    if on_grid:
        xp_ref, xm_ref, xn_ref = refs[:3]
        refs = refs[3:]
    else:
        xm_ref = refs[0]
        refs = refs[1:]
    mod_ref, wg_ref, wv_ref, cw_ref, cb_ref, wd_ref, o_ref, h_buf, acc_ref = refs
    tm = xm_ref.shape[0]
    ck = wg_ref.shape[-1]
    halo = GRID_W if on_grid else 0
    rows = tm + 2 * halo
    mod = mod_ref[...]
    md = lambda x: _modulate(x, mod[3:4], mod[4:5])
    if on_grid:
        h_buf[0:halo, :] = md(xp_ref[...])
        h_buf[halo:halo + tm, :] = md(xm_ref[...])
        h_buf[halo + tm:rows, :] = md(xn_ref[...])
    else:
        h_buf[...] = md(xm_ref[...])
    acc_ref[...] = jnp.zeros_like(acc_ref)
    r = lax.broadcasted_iota(jnp.int32, (rows, ck), 0)
    if on_grid:
        t = pl.program_id(1)
        nt = pl.num_programs(1)
        col = r & (GRID_W - 1)
        has_left = col != 0
        has_right = col != GRID_W - 1
        inside = jnp.logical_and(jnp.logical_or(r >= halo, t > 0),
                                 jnp.logical_or(r < halo + tm, t < nt - 1))
    else:
        has_left = r != 0
        has_right = r != rows - 1

    def chunk(c, carry):
        gate = _dot(h_buf[...], wg_ref[c])
        if on_grid:
            gate = jnp.where(inside, gate, 0.0)
        gl = jnp.where(has_left, pltpu.roll(gate, 1, axis=0), 0.0)
        gr = jnp.where(has_right, pltpu.roll(gate, rows - 1, axis=0), 0.0)
        cw = cw_ref[c]
        if on_grid:
            conv = None
            for dy in range(3):
                sl = slice(dy * halo, dy * halo + tm)
                term = (cw[3 * dy:3 * dy + 1] * gl[sl] + cw[3 * dy + 1:3 * dy + 2] * gate[sl]
                        + cw[3 * dy + 2:3 * dy + 3] * gr[sl])
                conv = term if conv is None else conv + term
            hm = h_buf[halo:halo + tm, :]
        else:
            conv = cw[3:4] * gl + cw[4:5] * gate + cw[5:6] * gr
            hm = h_buf[...]
        val = _dot(hm, wv_ref[c])
        act = (_silu(conv + cb_ref[c]) * val).astype(BF)
        acc_ref[...] += _dot(act, wd_ref[c])
        return carry

    lax.fori_loop(0, n_chunk, chunk, 0)
    o_ref[...] = xm_ref[...] + mod[5:6] * acc_ref[...]


def _ffn(x, mod, wg, wv, cw, cb, wd, on_grid):
    b, l, d = x.shape
    n_chunk, _, ck = wg.shape
    const = lambda a: pl.BlockSpec(a.shape, lambda i, t: (0,) * a.ndim, pipeline_mode=pl.Buffered(1))
    modspec = pl.BlockSpec((None, 6, d), lambda i, t: (i, 0, 0))
    if on_grid:
        tm = 512
        per = tm // GRID_W
        n_rows = l // GRID_W
        x_specs = [pl.BlockSpec((None, GRID_W, d), lambda i, t: (i, jnp.maximum(t * per - 1, 0), 0)),
                   pl.BlockSpec((None, tm, d), lambda i, t: (i, t, 0)),
                   pl.BlockSpec((None, GRID_W, d), lambda i, t: (i, jnp.minimum((t + 1) * per, n_rows - 1), 0))]
        xs = [x, x, x]
        rows = tm + 2 * GRID_W
    else:
        tm = l
        x_specs = [pl.BlockSpec((None, tm, d), lambda i, t: (i, t, 0))]
        xs = [x]
        rows = tm
    return pl.pallas_call(
        functools.partial(_ffn_body, on_grid, n_chunk),
        grid=(b, l // tm),
        in_specs=x_specs + [modspec, const(wg), const(wv), const(cw), const(cb), const(wd)],
        out_specs=pl.BlockSpec((None, tm, d), lambda i, t: (i, t, 0)),
        out_shape=jax.ShapeDtypeStruct((b, l, d), F32),
        scratch_shapes=[pltpu.VMEM((rows, d), BF), pltpu.VMEM((tm, d), F32)],
        compiler_params=_cparams("parallel", "parallel"),
        name="ffn_grid" if on_grid else "ffn_seq",
    )(*xs, mod, wg, wv, cw, cb, wd)


def _final_body(x_ref, g_ref, o_ref):
    o_ref[...] = _rms(x_ref[...]) * g_ref[...]


def _final_norm(x, g):
    b, l, d = x.shape
    tm = _row_tile(l, 1024)
    row = pl.BlockSpec((None, tm, d), lambda i, t: (i, t, 0))
    return pl.pallas_call(
        _final_body,
        grid=(b, l // tm),
        in_specs=[row, pl.BlockSpec((1, d), lambda i, t: (0, 0))],
        out_specs=row,
        out_shape=jax.ShapeDtypeStruct((b, l, d), F32),
        compiler_params=_cparams("parallel", "parallel"),
        name="final_norm",
    )(x, g.reshape(1, d))


def _ffn_chunk(d_ff):
    for ck in (256, 128):
        if d_ff % ck == 0:
            return ck
    return d_ff


def _even_mixer(x, ctx, ml, mc, w_in, w_out, lb_logits, gn, e, need_ctx):
    w_in = w_in.astype(BF)
    w_out = w_out.astype(BF)
    gn = gn.reshape(1, -1)
    lbl = lb_logits.reshape(lb_logits.shape[0], -1).astype(F32)
    a_l, q_l, lf_l, k_l, v_l, g_l = _proj_even(x, ml, w_in, lbl, e)
    a_c, q_c, lf_c, k_c, v_c, g_c = _proj_even(ctx, mc, w_in, lbl, e)
    pc, pl_ = (q_c, lf_c, k_c, v_c), (q_l, lf_l, k_l, v_l)
    of_c, of_l = _scan(False, pc, pl_)
    ob_c, ob_l = _scan(True, pc, pl_)
    x = _out_even(_dft_mix(a_l), of_l, ob_l, g_l, x, ml, gn, w_out)
    if need_ctx:
        ctx = _out_even(_dft_mix(a_c), of_c, ob_c, g_c, ctx, mc, gn, w_out)
    return x, ctx


def _att_mixer(x, ctx, ml, mc, w_qkv, qn_g, kn_g, w_out, need_ctx):
    l, lc = x.shape[1], ctx.shape[1]
    perm = np.concatenate([np.arange(0, HEAD_DIM, 2), np.arange(1, HEAD_DIM, 2)])
    n_qk = ATT_HEADS + ATT_KV_HEADS
    cols = np.concatenate([h * HEAD_DIM + perm for h in range(n_qk)]
                          + [np.arange(n_qk * HEAD_DIM, w_qkv.shape[-1])])
    w_in = w_qkv[:, cols].astype(BF)
    w_out = w_out.astype(BF)
    qg = qn_g[perm].reshape(1, -1)
    kg = kn_g[perm].reshape(1, -1)
    cos_l, sin_l = (jnp.asarray(t) for t in _rope_tables(l))
    cos_c, sin_c = jnp.ones((lc, HEAD_DIM), F32), jnp.zeros((lc, HEAD_DIM), F32)
    q_l, k_l, v_l = _proj_att(x, ml, w_in, qg, kg, cos_l, sin_l)
    q_c, k_c, v_c = _proj_att(ctx, mc, w_in, qg, kg, cos_c, sin_c)
    x = _out_att(_attention(q_l, [(k_l, v_l), (k_c, v_c)], min(l, 128)), x, ml, w_out)
    if need_ctx:
        ctx = _out_att(_attention(q_c, [(k_c, v_c)], lc), ctx, mc, w_out)
    return x, ctx


def _ffn_layer(x, ctx, ml, mc, w_up, conv_w, conv_b, w_down, need_ctx):
    d = x.shape[-1]
    d_ff = w_down.shape[0]
    ck = _ffn_chunk(d_ff)
    n_chunk = d_ff // ck
    wu = w_up.astype(BF)
    wg = wu[:, :d_ff].reshape(d, n_chunk, ck).transpose(1, 0, 2)
    wv = wu[:, d_ff:].reshape(d, n_chunk, ck).transpose(1, 0, 2)
    cw = conv_w.reshape(9, n_chunk, ck).transpose(1, 0, 2)
    cb = conv_b.reshape(n_chunk, 1, ck)
    wd = w_down.astype(BF).reshape(n_chunk, ck, d)
    x = _ffn(x, ml, wg, wv, cw, cb, wd, True)
    if need_ctx:
        ctx = _ffn(ctx, mc, wg, wv, cw, cb, wd, False)
    return x, ctx


def _mod_vectors(c, c_ctx, w_mod, b_mod):
    b, d = c.shape
    depth = w_mod.shape[0]
    pad = (-(b + 1)) % 8
    cc = jnp.concatenate([c, c_ctx[None, :], jnp.zeros((pad, d), F32)], axis=0)
    mods = _modulation(cc, w_mod, b_mod)
    mod_lat = mods[:, :b].reshape(depth, b, 6, d)
    mod_ctx = jnp.broadcast_to(mods[:, b].reshape(depth, 1, 6, d), (depth, b, 6, d))
    return mod_lat, mod_ctx


def kernel(x, c, ctx, c_ctx, w_mod, b_mod, w_in_ab, w_out_ab, hg_lb_logits, hg_norm_g, w_qkv, q_norm_g,
           k_norm_g, w_out_att, w_up, conv_w, conv_b, w_down, final_norm_g):
    depth = w_mod.shape[0]
    mod_lat, mod_ctx = _mod_vectors(c, c_ctx, w_mod, b_mod)
    for layer in range(depth):
        need_ctx = layer != depth - 1
        ml, mc = mod_lat[layer], mod_ctx[layer]
        if layer % 2 == 0:
            e = layer // 2
            x, ctx = _even_mixer(x, ctx, ml, mc, w_in_ab[e], w_out_ab[e], hg_lb_logits, hg_norm_g[e], e, need_ctx)
        else:
            o = layer // 2
            x, ctx = _att_mixer(x, ctx, ml, mc, w_qkv[o], q_norm_g[o], k_norm_g[o], w_out_att[o], need_ctx)
        x, ctx = _ffn_layer(x, ctx, ml, mc, w_up[layer], conv_w[layer], conv_b[layer], w_down[layer], need_ctx)
    return _final_norm(x, final_norm_g)
```

```python
import functools
import math

import numpy as np
import jax
import jax.numpy as jnp
from jax import lax
from jax.experimental import pallas as pl
from jax.experimental.pallas import tpu as pltpu

F32 = jnp.float32
BF = jnp.bfloat16

EPS = 1e-6
GRID_W = 64
LANE = 128
HEAD_DIM = 128
FNET_GROUPS = 4
HG_HEADS = 4
HG_W = HG_HEADS * HEAD_DIM
ATT_HEADS = 8
ATT_KV_HEADS = 2
ATT_GROUP = ATT_HEADS // ATT_KV_HEADS
ROPE_THETA = 10000.0
SCAN_CHUNK = 64
VMEM_LIMIT = 56 * 1024 * 1024


def _cparams(*sem):
    return pltpu.CompilerParams(dimension_semantics=sem, vmem_limit_bytes=VMEM_LIMIT)


def _dot(a, b):
    return jnp.dot(a, b, preferred_element_type=F32)


def _dot_nt(a, b):
    return lax.dot_general(a, b, (((1,), (1,)), ((), ())), preferred_element_type=F32)


def _dot_tn(a, b):
    return lax.dot_general(a, b, (((0,), (0,)), ((), ())), preferred_element_type=F32)


def _rms(x):
    return x * lax.rsqrt(jnp.mean(x * x, axis=-1, keepdims=True) + EPS)


def _silu(x):
    return x * jax.nn.sigmoid(x)


def _modulate(x, shift, scale):
    return (_rms(x) * (1.0 + scale) + shift).astype(BF)


def _row_tile(n, pref):
    return pref if n % pref == 0 else n


def _mod_body(c_ref, w_ref, b_ref, o_ref):
    s = _silu(c_ref[...]).astype(BF)
    o_ref[...] = _dot(s, w_ref[...].astype(BF)) + b_ref[...]


def _modulation(cc, w_mod, b_mod):
    depth, d, six_d = w_mod.shape
    r = cc.shape[0]
    return pl.pallas_call(
        _mod_body,
        grid=(depth, six_d // d),
        in_specs=[pl.BlockSpec((r, d), lambda l, j: (0, 0)),
                  pl.BlockSpec((None, d, d), lambda l, j: (l, 0, j)),
                  pl.BlockSpec((None, 1, d), lambda l, j: (l, 0, j))],
        out_specs=pl.BlockSpec((None, r, d), lambda l, j: (l, 0, j)),
        out_shape=jax.ShapeDtypeStruct((depth, r, six_d), F32),
        compiler_params=_cparams("parallel", "parallel"),
        name="modulation",
    )(cc, w_mod, b_mod.reshape(depth, 1, six_d))


def _proj_even_body(e, n_even, x_ref, mod_ref, w_ref, lbl_ref, a_ref, q_ref, lf_ref, k_ref, v_ref, g_ref):
    mod = mod_ref[...]
    h = _modulate(x_ref[...], mod[0:1], mod[1:2])
    rows = [lbl_ref[i:i + 1, :] for i in range(n_even)]
    mx = functools.reduce(jnp.maximum, rows)
    ex = [jnp.exp(r - mx) for r in rows]
    den = functools.reduce(lambda a, b: a + b, ex)
    lb = jnp.zeros_like(mx)
    for i in range(1, e + 1):
        lb = lb + ex[i] / den
    w = HG_W
    a_ref[...] = _dot(h, w_ref[:, 0:w]).astype(BF)
    q_ref[...] = _silu(_dot(h, w_ref[:, w:2 * w])).astype(BF)
    for d in range(2):
        z = _dot(h, w_ref[:, (2 + d) * w:(3 + d) * w])
        lbd = lb[:, d * w:(d + 1) * w]
        f = lbd + (1.0 - lbd) * jax.nn.sigmoid(z)
        lf_ref[:, d * w:(d + 1) * w] = jnp.log(f)
        k_ref[:, d * w:(d + 1) * w] = (1.0 - f).astype(BF)
    v_ref[...] = _dot(h, w_ref[:, 4 * w:5 * w]).astype(BF)
    g_ref[...] = _dot(h, w_ref[:, 5 * w:6 * w]).astype(BF)


def _proj_even(x, mod, w_in, lb_logits, e):
    b, l, d = x.shape
    n_even = lb_logits.shape[0]
    tm = _row_tile(l, 512)
    w = HG_W
    row = lambda width: pl.BlockSpec((None, tm, width), lambda i, t: (i, t, 0))
    shp = lambda width, dt: jax.ShapeDtypeStruct((b, l, width), dt)
    return pl.pallas_call(
        functools.partial(_proj_even_body, e, n_even),
        grid=(b, l // tm),
        in_specs=[row(d),
                  pl.BlockSpec((None, 6, d), lambda i, t: (i, 0, 0)),
                  pl.BlockSpec(w_in.shape, lambda i, t: (0, 0)),
                  pl.BlockSpec(lb_logits.shape, lambda i, t: (0, 0))],
        out_specs=[row(w), row(w), row(2 * w), row(2 * w), row(w), row(w)],
        out_shape=[shp(w, BF), shp(w, BF), shp(2 * w, F32), shp(2 * w, BF), shp(w, BF), shp(w, BF)],
        compiler_params=_cparams("parallel", "parallel"),
        name="proj_even",
    )(x, mod, w_in, lb_logits)


@functools.lru_cache(maxsize=None)
def _dft_tables(n):
    idx = np.arange(n, dtype=np.int64)
    ang = 2.0 * np.pi * ((idx[:, None] * idx[None, :]) % n).astype(np.float64) / n
    s = 1.0 / math.sqrt(n)
    return (np.cos(ang) * s).astype(np.float32), (np.sin(ang) * s).astype(np.float32)


def _dft_body(c_ref, s_ref, a_ref, cs_ref, o_ref):
    a = a_ref[...]
    p = _dot(c_ref[...], a).astype(BF)
    q = _dot(s_ref[...], a).astype(BF)
    cs = cs_ref[...]
    for g in range(FNET_GROUPS):
        sl = slice(g * LANE, (g + 1) * LANE)
        pq = jnp.concatenate([p[:, sl], q[:, sl]], axis=1)
        o_ref[:, sl] = _dot(pq, cs).astype(BF)


def _dft_mix(a):
    b, l, w = a.shape
    cl, sl_ = _dft_tables(l)
    cc, sc = _dft_tables(LANE)
    cos_l = jnp.asarray(cl).astype(BF)
    sin_l = jnp.asarray(sl_).astype(BF)
    cs = jnp.concatenate([jnp.asarray(cc), -jnp.asarray(sc)], axis=0).astype(BF)
    tm = _row_tile(l, 512)
    return pl.pallas_call(
        _dft_body,
        grid=(l // tm, b),
        in_specs=[pl.BlockSpec((tm, l), lambda m, i: (m, 0)),
                  pl.BlockSpec((tm, l), lambda m, i: (m, 0)),
                  pl.BlockSpec((None, l, w), lambda m, i: (i, 0, 0)),
                  pl.BlockSpec(cs.shape, lambda m, i: (0, 0))],
        out_specs=pl.BlockSpec((None, tm, w), lambda m, i: (i, m, 0)),
        out_shape=jax.ShapeDtypeStruct((b, l, w), BF),
        compiler_params=_cparams("parallel", "parallel"),
        name="dft_mix",
    )(cos_l, sin_l, a, cs)


def _scan_chunk(rev, q_ref, k_ref, lf_ref, v_ref, o_ref, st_ref):
    c, w = lf_ref.shape
    nlev = c.bit_length() - 1
    lf = lf_ref[...]
    r = lax.broadcasted_iota(jnp.int32, (c, w), 0)
    pos = (c - 1 - r) if rev else r

    def from_prev(x, sh):
        return pltpu.roll(x, (c - sh) if rev else sh, axis=0)

    def from_next(x, sh):
        return pltpu.roll(x, sh if rev else (c - sh), axis=0)

    b = lf
    for j in range(nlev):
        sh = 1 << j
        b = b + jnp.where(pos >= sh, from_prev(b, sh), 0.0)

    q = q_ref[...].astype(F32)
    k = k_ref[...].astype(F32)
    ti = lax.broadcasted_iota(jnp.int32, (c, c), 0)
    si = lax.broadcasted_iota(jnp.int32, (c, c), 1)
    if rev:
        ti, si = c - 1 - ti, c - 1 - si
    att = [jnp.zeros((c, c), F32) for _ in range(HG_HEADS)]
    y = b
    for j in range(nlev):
        sh = 1 << j
        upper = ((pos >> j) & 1) == 1
        bound = jnp.where(upper, from_prev(y, sh), y)
        dlt = b - bound
        e = jnp.exp(jnp.where(upper, dlt, -dlt))
        qm = jnp.where(upper, q * e, 0.0).astype(BF)
        km = jnp.where(upper, 0.0, k * e).astype(BF)
        same = (ti >> (j + 1)) == (si >> (j + 1))
        for h in range(HG_HEADS):
            sl = slice(h * LANE, (h + 1) * LANE)
            att[h] = att[h] + jnp.where(same, _dot_nt(qm[:, sl], km[:, sl]), 0.0)
        y = jnp.where(upper, y, from_next(y, sh))
    b_last = y
    qe = (q * jnp.exp(b)).astype(BF)
    kd = (k * jnp.exp(b_last - b)).astype(BF)
    carry = jnp.exp(b_last[0:1, :])
    v = v_ref[...]
    vf = v.astype(F32)
    for h in range(HG_HEADS):
        sl = slice(h * LANE, (h + 1) * LANE)
        st = st_ref[h]
        inter = _dot_nt(qe[:, sl], st.astype(BF))
        diag = jnp.sum(q[:, sl] * k[:, sl], axis=-1, keepdims=True)
        o_ref[:, sl] = inter + _dot(att[h].astype(BF), v[:, sl]) + diag * vf[:, sl]
        st_ref[h] = st * carry[:, sl] + _dot_tn(v[:, sl], kd[:, sl])


def _scan_body(rev, n_ctx, qc, kc, lfc, vc, ql, kl, lfl, vl, oc, ol, st_ref):
    s = pl.program_id(1)

    @pl.when(s == 0)
    def _():
        st_ref[...] = jnp.zeros_like(st_ref)

    @pl.when(s < n_ctx)
    def _():
        _scan_chunk(rev, qc, kc, lfc, vc, oc, st_ref)

    @pl.when(s >= n_ctx)
    def _():
        _scan_chunk(rev, ql, kl, lfl, vl, ol, st_ref)


def _scan(rev, pc, pl_):
    qc, lfc, kc, vc = pc
    ql, lfl, kl, vl = pl_
    b, lc, w = qc.shape
    ll = ql.shape[1]
    c = SCAN_CHUNK
    n_ctx, n_lat = lc // c, ll // c
    d = 1 if rev else 0

    if rev:
        ci = lambda s: jnp.maximum(n_ctx - 1 - s, 0)
        li = lambda s: n_lat - 1 - jnp.maximum(s - n_ctx, 0)
    else:
        ci = lambda s: jnp.minimum(s, n_ctx - 1)
        li = lambda s: jnp.maximum(s - n_ctx, 0)

    def spec(idx, col):
        return pl.BlockSpec((None, c, w), lambda i, s: (i, idx(s), col))

    return pl.pallas_call(
        functools.partial(_scan_body, rev, n_ctx),
        grid=(b, n_ctx + n_lat),
        in_specs=[spec(ci, 0), spec(ci, d), spec(ci, d), spec(ci, 0),
                  spec(li, 0), spec(li, d), spec(li, d), spec(li, 0)],
        out_specs=[spec(ci, 0), spec(li, 0)],
        out_shape=[jax.ShapeDtypeStruct((b, lc, w), F32), jax.ShapeDtypeStruct((b, ll, w), F32)],
        scratch_shapes=[pltpu.VMEM((HG_HEADS, HEAD_DIM, HEAD_DIM), F32)],
        compiler_params=_cparams("parallel", "arbitrary"),
        name="hgrn_scan_bwd" if rev else "hgrn_scan_fwd",
    )(qc, kc, lfc, vc, ql, kl, lfl, vl)


def _out_even_body(yf_ref, of_ref, ob_ref, g_ref, x_ref, mod_ref, gn_ref, w_ref, o_ref):
    o = of_ref[...] + ob_ref[...]
    g = g_ref[...].astype(F32)
    gn = gn_ref[...]
    parts = []
    for h in range(HG_HEADS):
        sl = slice(h * LANE, (h + 1) * LANE)
        parts.append((_rms(o[:, sl]) * gn[:, sl] * _silu(g[:, sl])).astype(BF))
    gated = jnp.concatenate(parts, axis=1)
    w = HG_W
    y = _dot(yf_ref[...], w_ref[0:w, :]) + _dot(gated, w_ref[w:2 * w, :])
    o_ref[...] = x_ref[...] + mod_ref[2:3, :] * y


def _out_even(yf, o_f, o_b, g, x, mod, gn, w_out):
    b, l, d = x.shape
    tm = _row_tile(l, 512)
    row = lambda width: pl.BlockSpec((None, tm, width), lambda i, t: (i, t, 0))
    return pl.pallas_call(
        _out_even_body,
        grid=(b, l // tm),
        in_specs=[row(HG_W), row(HG_W), row(HG_W), row(HG_W), row(d),
                  pl.BlockSpec((None, 6, d), lambda i, t: (i, 0, 0)),
                  pl.BlockSpec(gn.shape, lambda i, t: (0, 0)),
                  pl.BlockSpec(w_out.shape, lambda i, t: (0, 0))],
        out_specs=row(d),
        out_shape=jax.ShapeDtypeStruct((b, l, d), F32),
        compiler_params=_cparams("parallel", "parallel"),
        name="out_even",
    )(yf, o_f, o_b, g, x, mod, gn, w_out)


def _proj_att_body(x_ref, mod_ref, w_ref, qg_ref, kg_ref, cos_ref, sin_ref, q_ref, k_ref, v_ref):
    mod = mod_ref[...]
    h = _modulate(x_ref[...], mod[0:1], mod[1:2])
    cos = cos_ref[...]
    sin = sin_ref[...]
    scale = HEAD_DIM ** -0.5 * math.log2(math.e)
    for i in range(ATT_HEADS + ATT_KV_HEADS):
        sl = slice(i * LANE, (i + 1) * LANE)
        p = _dot(h, w_ref[:, sl])
        gain = qg_ref[...] if i < ATT_HEADS else kg_ref[...]
        p = _rms(p) * gain
        p = p * cos + pltpu.roll(p, HEAD_DIM // 2, axis=1) * sin
        if i < ATT_HEADS:
            q_ref[:, sl] = (p * scale).astype(BF)
        else:
            k_ref[:, (i - ATT_HEADS) * LANE:(i - ATT_HEADS + 1) * LANE] = p.astype(BF)
    nqk = (ATT_HEADS + ATT_KV_HEADS) * LANE
    v_ref[...] = _dot(h, w_ref[:, nqk:nqk + ATT_KV_HEADS * LANE]).astype(BF)


def _proj_att(x, mod, w_qkv, qg, kg, cos, sin):
    b, l, d = x.shape
    tm = _row_tile(l, 512)
    qw, kw = ATT_HEADS * LANE, ATT_KV_HEADS * LANE
    row = lambda width: pl.BlockSpec((None, tm, width), lambda i, t: (i, t, 0))
    const = lambda a: pl.BlockSpec(a.shape, lambda i, t: (0, 0))
    tab = pl.BlockSpec((tm, LANE), lambda i, t: (t, 0))
    return pl.pallas_call(
        _proj_att_body,
        grid=(b, l // tm),
        in_specs=[row(d), pl.BlockSpec((None, 6, d), lambda i, t: (i, 0, 0)),
                  const(w_qkv), const(qg), const(kg), tab, tab],
        out_specs=[row(qw), row(kw), row(kw)],
        out_shape=[jax.ShapeDtypeStruct((b, l, qw), BF), jax.ShapeDtypeStruct((b, l, kw), BF),
                   jax.ShapeDtypeStruct((b, l, kw), BF)],
        compiler_params=_cparams("parallel", "parallel"),
        name="proj_att",
    )(x, mod, w_qkv, qg, kg, cos, sin)


@functools.lru_cache(maxsize=None)
def _rope_tables(l):
    t = np.arange(l)
    row = (t // GRID_W).astype(np.float64)
    col = (t % GRID_W).astype(np.float64)
    n_freq = HEAD_DIM // 4
    freqs = ROPE_THETA ** (-np.arange(n_freq, dtype=np.float64) / n_freq)
    ang = np.concatenate([row[:, None] * freqs, col[:, None] * freqs], axis=-1)
    cos, sin = np.cos(ang), np.sin(ang)
    return (np.concatenate([cos, cos], axis=-1).astype(np.float32),
            np.concatenate([-sin, sin], axis=-1).astype(np.float32))


ATT_KV_BLOCK = 256


def _attn_body(n_src, q_ref, *refs):
    o_ref = refs[2 * n_src]
    tq = q_ref.shape[0]
    n = ATT_GROUP * tq
    vt_rows = refs[1].shape[0]
    q = q_ref[...]
    qs = jnp.concatenate([q[:, g * LANE:(g + 1) * LANE] for g in range(ATT_GROUP)], axis=0)
    blocks = []
    for i in range(n_src):
        k_ref, vt_ref = refs[2 * i], refs[2 * i + 1]
        lk = k_ref.shape[0]
        bk = min(ATT_KV_BLOCK, lk)
        blocks += [(k_ref, vt_ref, j * bk, bk) for j in range(lk // bk)]

    def scores(blk):
        k_ref, _, off, bk = blk
        return _dot_nt(k_ref[off:off + bk, :], qs)

    m = jnp.full((1, n), -1e30, F32)
    acc = jnp.zeros((vt_rows, n), F32)
    s_next = scores(blocks[0])
    for j, (_, vt_ref, off, bk) in enumerate(blocks):
        s = s_next
        if j + 1 < len(blocks):
            s_next = scores(blocks[j + 1])
        m_new = jnp.maximum(m, jnp.max(s, axis=0, keepdims=True))
        alpha = jnp.exp2(m - m_new)
        p = jnp.exp2((s - m_new).astype(BF))
        acc = alpha * acc + _dot(vt_ref[:, off:off + bk], p)
        m = m_new

    o = acc[:HEAD_DIM] * (1.0 / acc[HEAD_DIM:HEAD_DIM + 1])
    for g in range(ATT_GROUP):
        o_ref[:, g * LANE:(g + 1) * LANE] = o[:, g * tq:(g + 1) * tq].T.astype(BF)


def _values_t(v):
    b, l, _ = v.shape
    vt = v.reshape(b, l, ATT_KV_HEADS, HEAD_DIM).transpose(0, 2, 3, 1)
    return jnp.concatenate([vt, jnp.ones((b, ATT_KV_HEADS, 16, l), v.dtype)], axis=2)


def _attention(q, kvs, tq):
    b, lq, qw = q.shape
    gw = ATT_GROUP * LANE
    in_specs = [pl.BlockSpec((None, tq, gw), lambda i, h, t: (i, t, h))]
    args = [q]
    for k, vt in kvs:
        lk = k.shape[1]
        in_specs.append(pl.BlockSpec((None, lk, LANE), lambda i, h, t: (i, 0, h)))
        in_specs.append(pl.BlockSpec((None, None, vt.shape[2], lk), lambda i, h, t: (i, h, 0, 0)))
        args += [k, vt]
    return pl.pallas_call(
        functools.partial(_attn_body, len(kvs)),
        grid=(b, ATT_KV_HEADS, lq // tq),
        in_specs=in_specs,
        out_specs=pl.BlockSpec((None, tq, gw), lambda i, h, t: (i, t, h)),
        out_shape=jax.ShapeDtypeStruct((b, lq, qw), BF),
        compiler_params=_cparams("parallel", "parallel", "parallel"),
        name="attention",
    )(*args)


def _out_att_body(o_ref, x_ref, mod_ref, w_ref, y_ref):
    y_ref[...] = x_ref[...] + mod_ref[2:3, :] * _dot(o_ref[...], w_ref[...])


def _out_att(o, x, mod, w_out):
    b, l, d = x.shape
    tm = _row_tile(l, 512)
    row = lambda width: pl.BlockSpec((None, tm, width), lambda i, t: (i, t, 0))
    return pl.pallas_call(
        _out_att_body,
        grid=(b, l // tm),
        in_specs=[row(o.shape[-1]), row(d), pl.BlockSpec((None, 6, d), lambda i, t: (i, 0, 0)),
                  pl.BlockSpec(w_out.shape, lambda i, t: (0, 0))],
        out_specs=row(d),
        out_shape=jax.ShapeDtypeStruct((b, l, d), F32),
        compiler_params=_cparams("parallel", "parallel"),
        name="out_att",
    )(o, x, mod, w_out)


def _ffn_body(on_grid, n_chunk, *refs):
    if on_grid:
        xp_ref, xm_ref, xn_ref = refs[:3]
        refs = refs[3:]
    else:
        xm_ref = refs[0]
        refs = refs[1:]
    mod_ref, wg_ref, wv_ref, cw_ref, cb_ref, wd_ref, o_ref, h_buf, acc_ref = refs
    tm = xm_ref.shape[0]
    ck = wg_ref.shape[-1]
    halo = GRID_W if on_grid else 0
    rows = tm + 2 * halo
    mod = mod_ref[...]
    md = lambda x: _modulate(x, mod[3:4], mod[4:5])
    if on_grid:
        h_buf[0:halo, :] = md(xp_ref[...])
        h_buf[halo:halo + tm, :] = md(xm_ref[...])
        h_buf[halo + tm:rows, :] = md(xn_ref[...])
    else:
        h_buf[...] = md(xm_ref[...])
    acc_ref[...] = jnp.zeros_like(acc_ref)
    r = lax.broadcasted_iota(jnp.int32, (rows, ck), 0)
    if on_grid:
        t = pl.program_id(1)
        nt = pl.num_programs(1)
        col = r & (GRID_W - 1)
        has_left = col != 0
        has_right = col != GRID_W - 1
        inside = jnp.logical_and(jnp.logical_or(r >= halo, t > 0),
                                 jnp.logical_or(r < halo + tm, t < nt - 1))
    else:
        has_left = r != 0
        has_right = r != rows - 1

    def chunk(c, carry):
        gate = _dot(h_buf[...], wg_ref[c])
        if on_grid:
            gate = jnp.where(inside, gate, 0.0)
        gl = jnp.where(has_left, pltpu.roll(gate, 1, axis=0), 0.0)
        gr = jnp.where(has_right, pltpu.roll(gate, rows - 1, axis=0), 0.0)
        cw = cw_ref[c]
        if on_grid:
            conv = None
            for dy in range(3):
                sl = slice(dy * halo, dy * halo + tm)
                term = (cw[3 * dy:3 * dy + 1] * gl[sl] + cw[3 * dy + 1:3 * dy + 2] * gate[sl]
                        + cw[3 * dy + 2:3 * dy + 3] * gr[sl])
                conv = term if conv is None else conv + term
            hm = h_buf[halo:halo + tm, :]
        else:
            conv = cw[3:4] * gl + cw[4:5] * gate + cw[5:6] * gr
            hm = h_buf[...]
        val = _dot(hm, wv_ref[c])
        act = (_silu(conv + cb_ref[c]) * val).astype(BF)
        acc_ref[...] += _dot(act, wd_ref[c])
        return carry

    lax.fori_loop(0, n_chunk, chunk, 0)
    o_ref[...] = xm_ref[...] + mod[5:6] * acc_ref[...]


def _ffn(x, mod, wg, wv, cw, cb, wd, on_grid):
    b, l, d = x.shape
    n_chunk, _, ck = wg.shape
    const = lambda a: pl.BlockSpec(a.shape, lambda i, t: (0,) * a.ndim, pipeline_mode=pl.Buffered(1))
    modspec = pl.BlockSpec((None, 6, d), lambda i, t: (i, 0, 0))
    if on_grid:
        tm = 512
        per = tm // GRID_W
        n_rows = l // GRID_W
        x_specs = [pl.BlockSpec((None, GRID_W, d), lambda i, t: (i, jnp.maximum(t * per - 1, 0), 0)),
                   pl.BlockSpec((None, tm, d), lambda i, t: (i, t, 0)),
                   pl.BlockSpec((None, GRID_W, d), lambda i, t: (i, jnp.minimum((t + 1) * per, n_rows - 1), 0))]
        xs = [x, x, x]
        rows = tm + 2 * GRID_W
    else:
        tm = l
        x_specs = [pl.BlockSpec((None, tm, d), lambda i, t: (i, t, 0))]
        xs = [x]
        rows = tm
    return pl.pallas_call(
        functools.partial(_ffn_body, on_grid, n_chunk),
        grid=(b, l // tm),
        in_specs=x_specs + [modspec, const(wg), const(wv), const(cw), const(cb), const(wd)],
        out_specs=pl.BlockSpec((None, tm, d), lambda i, t: (i, t, 0)),
        out_shape=jax.ShapeDtypeStruct((b, l, d), F32),
        scratch_shapes=[pltpu.VMEM((rows, d), BF), pltpu.VMEM((tm, d), F32)],
        compiler_params=_cparams("parallel", "parallel"),
        name="ffn_grid" if on_grid else "ffn_seq",
    )(*xs, mod, wg, wv, cw, cb, wd)


def _final_body(x_ref, g_ref, o_ref):
    o_ref[...] = _rms(x_ref[...]) * g_ref[...]


def _final_norm(x, g):
    b, l, d = x.shape
    tm = _row_tile(l, 1024)
    row = pl.BlockSpec((None, tm, d), lambda i, t: (i, t, 0))
    return pl.pallas_call(
        _final_body,
        grid=(b, l // tm),
        in_specs=[row, pl.BlockSpec((1, d), lambda i, t: (0, 0))],
        out_specs=row,
        out_shape=jax.ShapeDtypeStruct((b, l, d), F32),
        compiler_params=_cparams("parallel", "parallel"),
        name="final_norm",
    )(x, g.reshape(1, d))


def _ffn_chunk(d_ff):
    for ck in (256, 128):
        if d_ff % ck == 0:
            return ck
    return d_ff


def _even_mixer(x, ctx, ml, mc, w_in, w_out, lb_logits, gn, e, need_ctx):
    w_in = w_in.astype(BF)
    w_out = w_out.astype(BF)
    gn = gn.reshape(1, -1)
    lbl = lb_logits.reshape(lb_logits.shape[0], -1).astype(F32)
    a_l, q_l, lf_l, k_l, v_l, g_l = _proj_even(x, ml, w_in, lbl, e)
    a_c, q_c, lf_c, k_c, v_c, g_c = _proj_even(ctx, mc, w_in, lbl, e)
    pc, pl_ = (q_c, lf_c, k_c, v_c), (q_l, lf_l, k_l, v_l)
    of_c, of_l = _scan(False, pc, pl_)
    ob_c, ob_l = _scan(True, pc, pl_)
    x = _out_even(_dft_mix(a_l), of_l, ob_l, g_l, x, ml, gn, w_out)
    if need_ctx:
        ctx = _out_even(_dft_mix(a_c), of_c, ob_c, g_c, ctx, mc, gn, w_out)
    return x, ctx


def _att_mixer(x, ctx, ml, mc, w_qkv, qn_g, kn_g, w_out, need_ctx):
    l, lc = x.shape[1], ctx.shape[1]
    perm = np.concatenate([np.arange(0, HEAD_DIM, 2), np.arange(1, HEAD_DIM, 2)])
    n_qk = ATT_HEADS + ATT_KV_HEADS
    cols = np.concatenate([h * HEAD_DIM + perm for h in range(n_qk)]
                          + [np.arange(n_qk * HEAD_DIM, w_qkv.shape[-1])])
    w_in = w_qkv[:, cols].astype(BF)
    w_out = w_out.astype(BF)
    qg = qn_g[perm].reshape(1, -1)
    kg = kn_g[perm].reshape(1, -1)
    cos_l, sin_l = (jnp.asarray(t) for t in _rope_tables(l))
    cos_c, sin_c = jnp.ones((lc, HEAD_DIM), F32), jnp.zeros((lc, HEAD_DIM), F32)
    q_l, k_l, v_l = _proj_att(x, ml, w_in, qg, kg, cos_l, sin_l)
    q_c, k_c, v_c = _proj_att(ctx, mc, w_in, qg, kg, cos_c, sin_c)
    vt_l, vt_c = _values_t(v_l), _values_t(v_c)
    x = _out_att(_attention(q_l, [(k_l, vt_l), (k_c, vt_c)], min(l, 128)), x, ml, w_out)
    if need_ctx:
        ctx = _out_att(_attention(q_c, [(k_c, vt_c)], lc), ctx, mc, w_out)
    return x, ctx


def _ffn_layer(x, ctx, ml, mc, w_up, conv_w, conv_b, w_down, need_ctx):
    d = x.shape[-1]
    d_ff = w_down.shape[0]
    ck = _ffn_chunk(d_ff)
    n_chunk = d_ff // ck
    wu = w_up.astype(BF)
    wg = wu[:, :d_ff].reshape(d, n_chunk, ck).transpose(1, 0, 2)
    wv = wu[:, d_ff:].reshape(d, n_chunk, ck).transpose(1, 0, 2)
    cw = conv_w.reshape(9, n_chunk, ck).transpose(1, 0, 2)
    cb = conv_b.reshape(n_chunk, 1, ck)
    wd = w_down.astype(BF).reshape(n_chunk, ck, d)
    x = _ffn(x, ml, wg, wv, cw, cb, wd, True)
    if need_ctx:
        ctx = _ffn(ctx, mc, wg, wv, cw, cb, wd, False)
    return x, ctx


def _mod_vectors(c, c_ctx, w_mod, b_mod):
    b, d = c.shape
    depth = w_mod.shape[0]
    pad = (-(b + 1)) % 8
    cc = jnp.concatenate([c, c_ctx[None, :], jnp.zeros((pad, d), F32)], axis=0)
    mods = _modulation(cc, w_mod, b_mod)
    mod_lat = mods[:, :b].reshape(depth, b, 6, d)
    mod_ctx = jnp.broadcast_to(mods[:, b].reshape(depth, 1, 6, d), (depth, b, 6, d))
    return mod_lat, mod_ctx


def kernel(x, c, ctx, c_ctx, w_mod, b_mod, w_in_ab, w_out_ab, hg_lb_logits, hg_norm_g, w_qkv, q_norm_g,
           k_norm_g, w_out_att, w_up, conv_w, conv_b, w_down, final_norm_g):
    depth = w_mod.shape[0]
    mod_lat, mod_ctx = _mod_vectors(c, c_ctx, w_mod, b_mod)
    for layer in range(depth):
        need_ctx = layer != depth - 1
        ml, mc = mod_lat[layer], mod_ctx[layer]
        if layer % 2 == 0:
            e = layer // 2
            x, ctx = _even_mixer(x, ctx, ml, mc, w_in_ab[e], w_out_ab[e], hg_lb_logits, hg_norm_g[e], e, need_ctx)
        else:
            o = layer // 2
            x, ctx = _att_mixer(x, ctx, ml, mc, w_qkv[o], q_norm_g[o], k_norm_g[o], w_out_att[o], need_ctx)
        x, ctx = _ffn_layer(x, ctx, ml, mc, w_up[layer], conv_w[layer], conv_b[layer], w_down[layer], need_ctx)
    return _final_norm(x, final_norm_g)
```

```python
import functools
import math

import numpy as np
import jax
import jax.numpy as jnp
from jax import lax
from jax.experimental import pallas as pl
from jax.experimental.pallas import tpu as pltpu

F32 = jnp.float32
BF = jnp.bfloat16

EPS = 1e-6
GRID_W = 64
LANE = 128
HEAD_DIM = 128
FNET_GROUPS = 4
HG_HEADS = 4
HG_W = HG_HEADS * HEAD_DIM
ATT_HEADS = 8
ATT_KV_HEADS = 2
ATT_GROUP = ATT_HEADS // ATT_KV_HEADS
ROPE_THETA = 10000.0
SCAN_CHUNK = 64
VMEM_LIMIT = 56 * 1024 * 1024


def _cparams(*sem):
    return pltpu.CompilerParams(dimension_semantics=sem, vmem_limit_bytes=VMEM_LIMIT)


def _dot(a, b):
    return jnp.dot(a, b, preferred_element_type=F32)


def _dot_nt(a, b):
    return lax.dot_general(a, b, (((1,), (1,)), ((), ())), preferred_element_type=F32)


def _dot_tn(a, b):
    return lax.dot_general(a, b, (((0,), (0,)), ((), ())), preferred_element_type=F32)


def _rms(x):
    return x * lax.rsqrt(jnp.mean(x * x, axis=-1, keepdims=True) + EPS)


def _silu(x):
    return x * jax.nn.sigmoid(x)


def _modulate(x, shift, scale):
    return (_rms(x) * (1.0 + scale) + shift).astype(BF)


def _row_tile(n, pref):
    return pref if n % pref == 0 else n


def _mod_body(c_ref, w_ref, b_ref, o_ref):
    s = _silu(c_ref[...]).astype(BF)
    o_ref[...] = _dot(s, w_ref[...].astype(BF)) + b_ref[...]


def _modulation(cc, w_mod, b_mod):
    depth, d, six_d = w_mod.shape
    r = cc.shape[0]
    return pl.pallas_call(
        _mod_body,
        grid=(depth, six_d // d),
        in_specs=[pl.BlockSpec((r, d), lambda l, j: (0, 0)),
                  pl.BlockSpec((None, d, d), lambda l, j: (l, 0, j)),
                  pl.BlockSpec((None, 1, d), lambda l, j: (l, 0, j))],
        out_specs=pl.BlockSpec((None, r, d), lambda l, j: (l, 0, j)),
        out_shape=jax.ShapeDtypeStruct((depth, r, six_d), F32),
        compiler_params=_cparams("parallel", "parallel"),
        name="modulation",
    )(cc, w_mod, b_mod.reshape(depth, 1, six_d))


def _proj_even_body(e, n_even, x_ref, mod_ref, w_ref, lbl_ref, a_ref, q_ref, lf_ref, k_ref, v_ref, g_ref):
    mod = mod_ref[...]
    h = _modulate(x_ref[...], mod[0:1], mod[1:2])
    rows = [lbl_ref[i:i + 1, :] for i in range(n_even)]
    mx = functools.reduce(jnp.maximum, rows)
    ex = [jnp.exp(r - mx) for r in rows]
    den = functools.reduce(lambda a, b: a + b, ex)
    lb = jnp.zeros_like(mx)
    for i in range(1, e + 1):
        lb = lb + ex[i] / den
    w = HG_W
    a_ref[...] = _dot(h, w_ref[:, 0:w]).astype(BF)
    q_ref[...] = _silu(_dot(h, w_ref[:, w:2 * w])).astype(BF)
    for d in range(2):
        z = _dot(h, w_ref[:, (2 + d) * w:(3 + d) * w])
        lbd = lb[:, d * w:(d + 1) * w]
        f = lbd + (1.0 - lbd) * jax.nn.sigmoid(z)
        lf_ref[:, d * w:(d + 1) * w] = jnp.log2(f)
        k_ref[:, d * w:(d + 1) * w] = (1.0 - f).astype(BF)
    v_ref[...] = _dot(h, w_ref[:, 4 * w:5 * w]).astype(BF)
    g_ref[...] = _dot(h, w_ref[:, 5 * w:6 * w]).astype(BF)


def _proj_even(x, mod, w_in, lb_logits, e):
    b, l, d = x.shape
    n_even = lb_logits.shape[0]
    tm = _row_tile(l, 512)
    w = HG_W
    row = lambda width: pl.BlockSpec((None, tm, width), lambda i, t: (i, t, 0))
    shp = lambda width, dt: jax.ShapeDtypeStruct((b, l, width), dt)
    return pl.pallas_call(
        functools.partial(_proj_even_body, e, n_even),
        grid=(b, l // tm),
        in_specs=[row(d),
                  pl.BlockSpec((None, 6, d), lambda i, t: (i, 0, 0)),
                  pl.BlockSpec(w_in.shape, lambda i, t: (0, 0)),
                  pl.BlockSpec(lb_logits.shape, lambda i, t: (0, 0))],
        out_specs=[row(w), row(w), row(2 * w), row(2 * w), row(w), row(w)],
        out_shape=[shp(w, BF), shp(w, BF), shp(2 * w, F32), shp(2 * w, BF), shp(w, BF), shp(w, BF)],
        compiler_params=_cparams("parallel", "parallel"),
        name="proj_even",
    )(x, mod, w_in, lb_logits)


@functools.lru_cache(maxsize=None)
def _dft_tables(n):
    idx = np.arange(n, dtype=np.int64)
    ang = 2.0 * np.pi * ((idx[:, None] * idx[None, :]) % n).astype(np.float64) / n
    s = 1.0 / math.sqrt(n)
    return (np.cos(ang) * s).astype(np.float32), (np.sin(ang) * s).astype(np.float32)


def _dft_body(c_ref, s_ref, a_ref, cs_ref, o_ref):
    a = a_ref[...]
    p = _dot(c_ref[...], a).astype(BF)
    q = _dot(s_ref[...], a).astype(BF)
    cs = cs_ref[...]
    for g in range(FNET_GROUPS):
        sl = slice(g * LANE, (g + 1) * LANE)
        pq = jnp.concatenate([p[:, sl], q[:, sl]], axis=1)
        o_ref[:, sl] = _dot(pq, cs).astype(BF)


def _dft_mix(a):
    b, l, w = a.shape
    cl, sl_ = _dft_tables(l)
    cc, sc = _dft_tables(LANE)
    cos_l = jnp.asarray(cl).astype(BF)
    sin_l = jnp.asarray(sl_).astype(BF)
    cs = jnp.concatenate([jnp.asarray(cc), -jnp.asarray(sc)], axis=0).astype(BF)
    tm = _row_tile(l, 512)
    return pl.pallas_call(
        _dft_body,
        grid=(l // tm, b),
        in_specs=[pl.BlockSpec((tm, l), lambda m, i: (m, 0)),
                  pl.BlockSpec((tm, l), lambda m, i: (m, 0)),
                  pl.BlockSpec((None, l, w), lambda m, i: (i, 0, 0)),
                  pl.BlockSpec(cs.shape, lambda m, i: (0, 0))],
        out_specs=pl.BlockSpec((None, tm, w), lambda m, i: (i, m, 0)),
        out_shape=jax.ShapeDtypeStruct((b, l, w), BF),
        compiler_params=_cparams("parallel", "parallel"),
        name="dft_mix",
    )(cos_l, sin_l, a, cs)


def _scan_chunk(rev, q_ref, k_ref, lf_ref, v_ref, o_ref, st_ref):
    c, w = lf_ref.shape
    nlev = c.bit_length() - 1
    lf = lf_ref[...]
    r = lax.broadcasted_iota(jnp.int32, (c, w), 0)
    pos = (c - 1 - r) if rev else r

    def from_prev(x, sh):
        return pltpu.roll(x, (c - sh) if rev else sh, axis=0)

    def from_next(x, sh):
        return pltpu.roll(x, sh if rev else (c - sh), axis=0)

    def neg_abs(x):
        return pltpu.bitcast(pltpu.bitcast(x, jnp.uint32) | jnp.uint32(0x80000000), F32)

    b = lf
    for j in range(nlev):
        sh = 1 << j
        b = b + jnp.where(pos >= sh, from_prev(b, sh), 0.0)

    q = q_ref[...].astype(F32)
    k = k_ref[...].astype(F32)
    ti = lax.broadcasted_iota(jnp.int32, (c, c), 0)
    si = lax.broadcasted_iota(jnp.int32, (c, c), 1)
    later = (ti < si) if rev else (ti > si)
    diff = ti ^ si
    att = [jnp.zeros((c, c), F32) for _ in range(HG_HEADS)]
    y = b
    for j in range(nlev):
        sh = 1 << j
        upper = ((pos >> j) & 1) == 1
        bound = jnp.where(upper, from_prev(y, sh), y)
        e = jnp.exp2(neg_abs(b - bound))
        qm = (q * e).astype(BF)
        km = (k * e).astype(BF)
        level = jnp.logical_and((diff >> j) == 1, later)
        for h in range(HG_HEADS):
            sl = slice(h * LANE, (h + 1) * LANE)
            att[h] = att[h] + jnp.where(level, _dot_nt(qm[:, sl], km[:, sl]), 0.0)
        y = jnp.where(upper, y, from_next(y, sh))
    b_last = y
    qe = (q * jnp.exp2(b)).astype(BF)
    kd = (k * jnp.exp2(b_last - b)).astype(BF)
    carry = jnp.exp2(b_last[0:1, :])
    v = v_ref[...]
    vf = v.astype(F32)
    for h in range(HG_HEADS):
        sl = slice(h * LANE, (h + 1) * LANE)
        st = st_ref[h]
        inter = _dot_nt(qe[:, sl], st.astype(BF))
        diag = jnp.sum(q[:, sl] * k[:, sl], axis=-1, keepdims=True)
        o_ref[:, sl] = inter + _dot(att[h].astype(BF), v[:, sl]) + diag * vf[:, sl]
        st_ref[h] = st * carry[:, sl] + _dot_tn(v[:, sl], kd[:, sl])


def _scan_body(n_ctx, *refs):
    ins, outs, states = refs[:16], refs[16:20], refs[20:]
    s = pl.program_id(1)

    @pl.when(s == 0)
    def _():
        for st_ref in states:
            st_ref[...] = jnp.zeros_like(st_ref)

    def run(stream):
        for d in range(2):
            base = 8 * d + 4 * stream
            _scan_chunk(d == 1, *ins[base:base + 4], outs[2 * d + stream], states[d])

    @pl.when(s < n_ctx)
    def _():
        run(0)

    @pl.when(s >= n_ctx)
    def _():
        run(1)


def _scan(pc, pl_):
    qc, lfc, kc, vc = pc
    ql, lfl, kl, vl = pl_
    b, lc, w = qc.shape
    ll = ql.shape[1]
    c = SCAN_CHUNK
    n_ctx, n_lat = lc // c, ll // c

    def spec(idx, col):
        return pl.BlockSpec((None, c, w), lambda i, s: (i, idx(s), col))

    in_specs, out_specs, args = [], [], []
    for d in range(2):
        if d:
            ci = lambda s: jnp.maximum(n_ctx - 1 - s, 0)
            li = lambda s: n_lat - 1 - jnp.maximum(s - n_ctx, 0)
        else:
            ci = lambda s: jnp.minimum(s, n_ctx - 1)
            li = lambda s: jnp.maximum(s - n_ctx, 0)
        in_specs += [spec(ci, 0), spec(ci, d), spec(ci, d), spec(ci, 0),
                     spec(li, 0), spec(li, d), spec(li, d), spec(li, 0)]
        args += [qc, kc, lfc, vc, ql, kl, lfl, vl]
        out_specs += [spec(ci, 0), spec(li, 0)]
    shp = lambda l: jax.ShapeDtypeStruct((b, l, w), F32)
    of_c, of_l, ob_c, ob_l = pl.pallas_call(
        functools.partial(_scan_body, n_ctx),
        grid=(b, n_ctx + n_lat),
        in_specs=in_specs,
        out_specs=out_specs,
        out_shape=[shp(lc), shp(ll), shp(lc), shp(ll)],
        scratch_shapes=[pltpu.VMEM((HG_HEADS, HEAD_DIM, HEAD_DIM), F32) for _ in range(2)],
        compiler_params=_cparams("parallel", "arbitrary"),
        name="hgrn_scan",
    )(*args)
    return (of_c, of_l), (ob_c, ob_l)


def _out_even_body(yf_ref, of_ref, ob_ref, g_ref, x_ref, mod_ref, gn_ref, w_ref, o_ref):
    o = of_ref[...] + ob_ref[...]
    g = g_ref[...].astype(F32)
    gn = gn_ref[...]
    parts = []
    for h in range(HG_HEADS):
        sl = slice(h * LANE, (h + 1) * LANE)
        parts.append((_rms(o[:, sl]) * gn[:, sl] * _silu(g[:, sl])).astype(BF))
    gated = jnp.concatenate(parts, axis=1)
    w = HG_W
    y = _dot(yf_ref[...], w_ref[0:w, :]) + _dot(gated, w_ref[w:2 * w, :])
    o_ref[...] = x_ref[...] + mod_ref[2:3, :] * y


def _out_even(yf, o_f, o_b, g, x, mod, gn, w_out):
    b, l, d = x.shape
    tm = _row_tile(l, 512)
    row = lambda width: pl.BlockSpec((None, tm, width), lambda i, t: (i, t, 0))
    return pl.pallas_call(
        _out_even_body,
        grid=(b, l // tm),
        in_specs=[row(HG_W), row(HG_W), row(HG_W), row(HG_W), row(d),
                  pl.BlockSpec((None, 6, d), lambda i, t: (i, 0, 0)),
                  pl.BlockSpec(gn.shape, lambda i, t: (0, 0)),
                  pl.BlockSpec(w_out.shape, lambda i, t: (0, 0))],
        out_specs=row(d),
        out_shape=jax.ShapeDtypeStruct((b, l, d), F32),
        compiler_params=_cparams("parallel", "parallel"),
        name="out_even",
    )(yf, o_f, o_b, g, x, mod, gn, w_out)


def _proj_att_body(x_ref, mod_ref, w_ref, qg_ref, kg_ref, cos_ref, sin_ref, q_ref, k_ref, v_ref):
    mod = mod_ref[...]
    h = _modulate(x_ref[...], mod[0:1], mod[1:2])
    cos = cos_ref[...]
    sin = sin_ref[...]
    scale = HEAD_DIM ** -0.5 * math.log2(math.e)
    for i in range(ATT_HEADS + ATT_KV_HEADS):
        sl = slice(i * LANE, (i + 1) * LANE)
        p = _dot(h, w_ref[:, sl])
        gain = qg_ref[...] if i < ATT_HEADS else kg_ref[...]
        p = _rms(p) * gain
        p = p * cos + pltpu.roll(p, HEAD_DIM // 2, axis=1) * sin
        if i < ATT_HEADS:
            q_ref[:, sl] = (p * scale).astype(BF)
        else:
            k_ref[:, (i - ATT_HEADS) * LANE:(i - ATT_HEADS + 1) * LANE] = p.astype(BF)
    nqk = (ATT_HEADS + ATT_KV_HEADS) * LANE
    v_ref[...] = _dot(h, w_ref[:, nqk:nqk + ATT_KV_HEADS * LANE]).astype(BF)


def _proj_att(x, mod, w_qkv, qg, kg, cos, sin):
    b, l, d = x.shape
    tm = _row_tile(l, 512)
    qw, kw = ATT_HEADS * LANE, ATT_KV_HEADS * LANE
    row = lambda width: pl.BlockSpec((None, tm, width), lambda i, t: (i, t, 0))
    const = lambda a: pl.BlockSpec(a.shape, lambda i, t: (0, 0))
    tab = pl.BlockSpec((tm, LANE), lambda i, t: (t, 0))
    return pl.pallas_call(
        _proj_att_body,
        grid=(b, l // tm),
        in_specs=[row(d), pl.BlockSpec((None, 6, d), lambda i, t: (i, 0, 0)),
                  const(w_qkv), const(qg), const(kg), tab, tab],
        out_specs=[row(qw), row(kw), row(kw)],
        out_shape=[jax.ShapeDtypeStruct((b, l, qw), BF), jax.ShapeDtypeStruct((b, l, kw), BF),
                   jax.ShapeDtypeStruct((b, l, kw), BF)],
        compiler_params=_cparams("parallel", "parallel"),
        name="proj_att",
    )(x, mod, w_qkv, qg, kg, cos, sin)


@functools.lru_cache(maxsize=None)
def _rope_tables(l):
    t = np.arange(l)
    row = (t // GRID_W).astype(np.float64)
    col = (t % GRID_W).astype(np.float64)
    n_freq = HEAD_DIM // 4
    freqs = ROPE_THETA ** (-np.arange(n_freq, dtype=np.float64) / n_freq)
    ang = np.concatenate([row[:, None] * freqs, col[:, None] * freqs], axis=-1)
    cos, sin = np.cos(ang), np.sin(ang)
    return (np.concatenate([cos, cos], axis=-1).astype(np.float32),
            np.concatenate([-sin, sin], axis=-1).astype(np.float32))


ATT_KV_BLOCK = 256


def _attn_body(n_src, q_ref, *refs):
    o_ref = refs[2 * n_src]
    tq = q_ref.shape[0]
    n = ATT_GROUP * tq
    vt_rows = refs[1].shape[0]
    q = q_ref[...]
    qs = jnp.concatenate([q[:, g * LANE:(g + 1) * LANE] for g in range(ATT_GROUP)], axis=0)
    blocks = []
    for i in range(n_src):
        k_ref, vt_ref = refs[2 * i], refs[2 * i + 1]
        lk = k_ref.shape[0]
        bk = min(ATT_KV_BLOCK, lk)
        blocks += [(k_ref, vt_ref, j * bk, bk) for j in range(lk // bk)]

    def scores(blk):
        k_ref, _, off, bk = blk
        return _dot_nt(k_ref[off:off + bk, :], qs)

    m = jnp.full((1, n), -1e30, F32)
    acc = jnp.zeros((vt_rows, n), F32)
    s_next = scores(blocks[0])
    for j, (_, vt_ref, off, bk) in enumerate(blocks):
        s = s_next
        if j + 1 < len(blocks):
            s_next = scores(blocks[j + 1])
        m_new = jnp.maximum(m, jnp.max(s, axis=0, keepdims=True))
        alpha = jnp.exp2(m - m_new)
        p = jnp.exp2((s - m_new).astype(BF))
        acc = alpha * acc + _dot(vt_ref[:, off:off + bk], p)
        m = m_new

    o = acc[:HEAD_DIM] * (1.0 / acc[HEAD_DIM:HEAD_DIM + 1])
    for g in range(ATT_GROUP):
        o_ref[:, g * LANE:(g + 1) * LANE] = o[:, g * tq:(g + 1) * tq].T.astype(BF)


def _values_t(v):
    b, l, _ = v.shape
    vt = v.reshape(b, l, ATT_KV_HEADS, HEAD_DIM).transpose(0, 2, 3, 1)
    return jnp.concatenate([vt, jnp.ones((b, ATT_KV_HEADS, 16, l), v.dtype)], axis=2)


def _attention(q, kvs, tq):
    b, lq, qw = q.shape
    gw = ATT_GROUP * LANE
    in_specs = [pl.BlockSpec((None, tq, gw), lambda i, h, t: (i, t, h))]
    args = [q]
    for k, vt in kvs:
        lk = k.shape[1]
        in_specs.append(pl.BlockSpec((None, lk, LANE), lambda i, h, t: (i, 0, h)))
        in_specs.append(pl.BlockSpec((None, None, vt.shape[2], lk), lambda i, h, t: (i, h, 0, 0)))
        args += [k, vt]
    return pl.pallas_call(
        functools.partial(_attn_body, len(kvs)),
        grid=(b, ATT_KV_HEADS, lq // tq),
        in_specs=in_specs,
        out_specs=pl.BlockSpec((None, tq, gw), lambda i, h, t: (i, t, h)),
        out_shape=jax.ShapeDtypeStruct((b, lq, qw), BF),
        compiler_params=_cparams("parallel", "parallel", "parallel"),
        name="attention",
    )(*args)


def _out_att_body(o_ref, x_ref, mod_ref, w_ref, y_ref):
    y_ref[...] = x_ref[...] + mod_ref[2:3, :] * _dot(o_ref[...], w_ref[...])


def _out_att(o, x, mod, w_out):
    b, l, d = x.shape
    tm = _row_tile(l, 512)
    row = lambda width: pl.BlockSpec((None, tm, width), lambda i, t: (i, t, 0))
    return pl.pallas_call(
        _out_att_body,
        grid=(b, l // tm),
        in_specs=[row(o.shape[-1]), row(d), pl.BlockSpec((None, 6, d), lambda i, t: (i, 0, 0)),
                  pl.BlockSpec(w_out.shape, lambda i, t: (0, 0))],
        out_specs=row(d),
        out_shape=jax.ShapeDtypeStruct((b, l, d), F32),
        compiler_params=_cparams("parallel", "parallel"),
        name="out_att",
    )(o, x, mod, w_out)


def _ffn_body(on_grid, n_chunk, *refs):
    if on_grid:
        xp_ref, xm_ref, xn_ref = refs[:3]
        refs = refs[3:]
    else:
        xm_ref = refs[0]
        refs = refs[1:]
    mod_ref, wg_ref, wv_ref, cw_ref, cb_ref, wd_ref, o_ref, h_buf = refs
    tm = xm_ref.shape[0]
    ck = wg_ref.shape[-1]
    halo = GRID_W if on_grid else 0
    rows = tm + 2 * halo
    mod = mod_ref[...]
    md = lambda x: _modulate(x, mod[3:4], mod[4:5])
    if on_grid:
        h_buf[0:halo, :] = md(xp_ref[...])
        h_buf[halo:halo + tm, :] = md(xm_ref[...])
        h_buf[halo + tm:rows, :] = md(xn_ref[...])
    else:
        h_buf[...] = md(xm_ref[...])
    r = lax.broadcasted_iota(jnp.int32, (rows, ck), 0)
    if on_grid:
        t = pl.program_id(1)
        nt = pl.num_programs(1)
        col = r & (GRID_W - 1)
        has_left = col != 0
        has_right = col != GRID_W - 1
        inside = jnp.logical_and(jnp.logical_or(r >= halo, t > 0),
                                 jnp.logical_or(r < halo + tm, t < nt - 1))
    else:
        has_left = r != 0
        has_right = r != rows - 1

    hm_sl = slice(halo, halo + tm)

    def up(c):
        gate = _dot(h_buf[...], wg_ref[c])
        val = _dot(h_buf[hm_sl, :], wv_ref[c])
        return gate, val

    def down(c, gate, val):
        if on_grid:
            gate = jnp.where(inside, gate, 0.0)
        gl = jnp.where(has_left, pltpu.roll(gate, 1, axis=0), 0.0)
        gr = jnp.where(has_right, pltpu.roll(gate, rows - 1, axis=0), 0.0)
        cw = cw_ref[c]
        if on_grid:
            conv = None
            for dy in range(3):
                sl = slice(dy * halo, dy * halo + tm)
                term = (cw[3 * dy:3 * dy + 1] * gl[sl] + cw[3 * dy + 1:3 * dy + 2] * gate[sl]
                        + cw[3 * dy + 2:3 * dy + 3] * gr[sl])
                conv = term if conv is None else conv + term
        else:
            conv = cw[3:4] * gl + cw[4:5] * gate + cw[5:6] * gr
        act = (_silu(conv + cb_ref[c]) * val).astype(BF)
        return _dot(act, wd_ref[c])

    nxt = up(0)
    acc = None
    for c in range(n_chunk):
        cur = nxt
        if c + 1 < n_chunk:
            nxt = up(c + 1)
        part = down(c, *cur)
        acc = part if acc is None else acc + part
    o_ref[...] = xm_ref[...] + mod[5:6] * acc


def _ffn(x, mod, wg, wv, cw, cb, wd, on_grid):
    b, l, d = x.shape
    n_chunk, _, ck = wg.shape
    const = lambda a: pl.BlockSpec(a.shape, lambda i, t: (0,) * a.ndim, pipeline_mode=pl.Buffered(1))
    modspec = pl.BlockSpec((None, 6, d), lambda i, t: (i, 0, 0))
    if on_grid:
        tm = 512
        per = tm // GRID_W
        n_rows = l // GRID_W
        x_specs = [pl.BlockSpec((None, GRID_W, d), lambda i, t: (i, jnp.maximum(t * per - 1, 0), 0)),
                   pl.BlockSpec((None, tm, d), lambda i, t: (i, t, 0)),
                   pl.BlockSpec((None, GRID_W, d), lambda i, t: (i, jnp.minimum((t + 1) * per, n_rows - 1), 0))]
        xs = [x, x, x]
        rows = tm + 2 * GRID_W
    else:
        tm = l
        x_specs = [pl.BlockSpec((None, tm, d), lambda i, t: (i, t, 0))]
        xs = [x]
        rows = tm
    return pl.pallas_call(
        functools.partial(_ffn_body, on_grid, n_chunk),
        grid=(b, l // tm),
        in_specs=x_specs + [modspec, const(wg), const(wv), const(cw), const(cb), const(wd)],
        out_specs=pl.BlockSpec((None, tm, d), lambda i, t: (i, t, 0)),
        out_shape=jax.ShapeDtypeStruct((b, l, d), F32),
        scratch_shapes=[pltpu.VMEM((rows, d), BF)],
        compiler_params=_cparams("parallel", "parallel"),
        name="ffn_grid" if on_grid else "ffn_seq",
    )(*xs, mod, wg, wv, cw, cb, wd)


def _final_body(x_ref, g_ref, o_ref):
    o_ref[...] = _rms(x_ref[...]) * g_ref[...]


def _final_norm(x, g):
    b, l, d = x.shape
    tm = _row_tile(l, 1024)
    row = pl.BlockSpec((None, tm, d), lambda i, t: (i, t, 0))
    return pl.pallas_call(
        _final_body,
        grid=(b, l // tm),
        in_specs=[row, pl.BlockSpec((1, d), lambda i, t: (0, 0))],
        out_specs=row,
        out_shape=jax.ShapeDtypeStruct((b, l, d), F32),
        compiler_params=_cparams("parallel", "parallel"),
        name="final_norm",
    )(x, g.reshape(1, d))


def _ffn_chunk(d_ff):
    for ck in (256, 128):
        if d_ff % ck == 0:
            return ck
    return d_ff


def _even_mixer(x, ctx, ml, mc, w_in, w_out, lb_logits, gn, e, need_ctx):
    w_in = w_in.astype(BF)
    w_out = w_out.astype(BF)
    gn = gn.reshape(1, -1)
    lbl = lb_logits.reshape(lb_logits.shape[0], -1).astype(F32)
    a_l, q_l, lf_l, k_l, v_l, g_l = _proj_even(x, ml, w_in, lbl, e)
    a_c, q_c, lf_c, k_c, v_c, g_c = _proj_even(ctx, mc, w_in, lbl, e)
    pc, pl_ = (q_c, lf_c, k_c, v_c), (q_l, lf_l, k_l, v_l)
    (of_c, of_l), (ob_c, ob_l) = _scan(pc, pl_)
    x = _out_even(_dft_mix(a_l), of_l, ob_l, g_l, x, ml, gn, w_out)
    if need_ctx:
        ctx = _out_even(_dft_mix(a_c), of_c, ob_c, g_c, ctx, mc, gn, w_out)
    return x, ctx


def _att_mixer(x, ctx, ml, mc, w_qkv, qn_g, kn_g, w_out, need_ctx):
    l, lc = x.shape[1], ctx.shape[1]
    perm = np.concatenate([np.arange(0, HEAD_DIM, 2), np.arange(1, HEAD_DIM, 2)])
    n_qk = ATT_HEADS + ATT_KV_HEADS
    cols = np.concatenate([h * HEAD_DIM + perm for h in range(n_qk)]
                          + [np.arange(n_qk * HEAD_DIM, w_qkv.shape[-1])])
    w_in = w_qkv[:, cols].astype(BF)
    w_out = w_out.astype(BF)
    qg = qn_g[perm].reshape(1, -1)
    kg = kn_g[perm].reshape(1, -1)
    cos_l, sin_l = (jnp.asarray(t) for t in _rope_tables(l))
    cos_c, sin_c = jnp.ones((lc, HEAD_DIM), F32), jnp.zeros((lc, HEAD_DIM), F32)
    q_l, k_l, v_l = _proj_att(x, ml, w_in, qg, kg, cos_l, sin_l)
    q_c, k_c, v_c = _proj_att(ctx, mc, w_in, qg, kg, cos_c, sin_c)
    vt_l, vt_c = _values_t(v_l), _values_t(v_c)
    x = _out_att(_attention(q_l, [(k_l, vt_l), (k_c, vt_c)], min(l, 128)), x, ml, w_out)
    if need_ctx:
        ctx = _out_att(_attention(q_c, [(k_c, vt_c)], lc), ctx, mc, w_out)
    return x, ctx


def _ffn_layer(x, ctx, ml, mc, w_up, conv_w, conv_b, w_down, need_ctx):
    d = x.shape[-1]
    d_ff = w_down.shape[0]
    ck = _ffn_chunk(d_ff)
    n_chunk = d_ff // ck
    wu = w_up.astype(BF)
    wg = wu[:, :d_ff].reshape(d, n_chunk, ck).transpose(1, 0, 2)
    wv = wu[:, d_ff:].reshape(d, n_chunk, ck).transpose(1, 0, 2)
    cw = conv_w.reshape(9, n_chunk, ck).transpose(1, 0, 2)
    cb = conv_b.reshape(n_chunk, 1, ck)
    wd = w_down.astype(BF).reshape(n_chunk, ck, d)
    x = _ffn(x, ml, wg, wv, cw, cb, wd, True)
    if need_ctx:
        ctx = _ffn(ctx, mc, wg, wv, cw, cb, wd, False)
    return x, ctx


def _mod_vectors(c, c_ctx, w_mod, b_mod):
    b, d = c.shape
    depth = w_mod.shape[0]
    pad = (-(b + 1)) % 8
    cc = jnp.concatenate([c, c_ctx[None, :], jnp.zeros((pad, d), F32)], axis=0)
    mods = _modulation(cc, w_mod, b_mod)
    mod_lat = mods[:, :b].reshape(depth, b, 6, d)
    mod_ctx = jnp.broadcast_to(mods[:, b].reshape(depth, 1, 6, d), (depth, b, 6, d))
    return mod_lat, mod_ctx


def kernel(x, c, ctx, c_ctx, w_mod, b_mod, w_in_ab, w_out_ab, hg_lb_logits, hg_norm_g, w_qkv, q_norm_g,
           k_norm_g, w_out_att, w_up, conv_w, conv_b, w_down, final_norm_g):
    depth = w_mod.shape[0]
    mod_lat, mod_ctx = _mod_vectors(c, c_ctx, w_mod, b_mod)
    for layer in range(depth):
        need_ctx = layer != depth - 1
        ml, mc = mod_lat[layer], mod_ctx[layer]
        if layer % 2 == 0:
            e = layer // 2
            x, ctx = _even_mixer(x, ctx, ml, mc, w_in_ab[e], w_out_ab[e], hg_lb_logits, hg_norm_g[e], e, need_ctx)
        else:
            o = layer // 2
            x, ctx = _att_mixer(x, ctx, ml, mc, w_qkv[o], q_norm_g[o], k_norm_g[o], w_out_att[o], need_ctx)
        x, ctx = _ffn_layer(x, ctx, ml, mc, w_up[layer], conv_w[layer], conv_b[layer], w_down[layer], need_ctx)
    return _final_norm(x, final_norm_g)
```

```python
import functools
import math

import numpy as np
import jax
import jax.numpy as jnp
from jax import lax
from jax.experimental import pallas as pl
from jax.experimental.pallas import tpu as pltpu

F32 = jnp.float32
BF = jnp.bfloat16

EPS = 1e-6
GRID_W = 64
LANE = 128
HEAD_DIM = 128
FNET_GROUPS = 4
HG_HEADS = 4
HG_W = HG_HEADS * HEAD_DIM
ATT_HEADS = 8
ATT_KV_HEADS = 2
ATT_GROUP = ATT_HEADS // ATT_KV_HEADS
ROPE_THETA = 10000.0
SCAN_CHUNK = 64
VMEM_LIMIT = 56 * 1024 * 1024


def _cparams(*sem):
    return pltpu.CompilerParams(dimension_semantics=sem, vmem_limit_bytes=VMEM_LIMIT)


def _dot(a, b):
    return jnp.dot(a, b, preferred_element_type=F32)


def _dot_nt(a, b):
    return lax.dot_general(a, b, (((1,), (1,)), ((), ())), preferred_element_type=F32)


def _dot_tn(a, b):
    return lax.dot_general(a, b, (((0,), (0,)), ((), ())), preferred_element_type=F32)


def _rms(x):
    return x * lax.rsqrt(jnp.mean(x * x, axis=-1, keepdims=True) + EPS)


def _silu(x):
    return x * jax.nn.sigmoid(x)


def _modulate(x, shift, scale):
    return (_rms(x) * (1.0 + scale) + shift).astype(BF)


def _row_tile(n, pref):
    return pref if n % pref == 0 else n


def _mod_body(c_ref, w_ref, b_ref, o_ref):
    s = _silu(c_ref[...]).astype(BF)
    o_ref[...] = _dot(s, w_ref[...].astype(BF)) + b_ref[...]


def _modulation(cc, w_mod, b_mod):
    depth, d, six_d = w_mod.shape
    r = cc.shape[0]
    return pl.pallas_call(
        _mod_body,
        grid=(depth, six_d // d),
        in_specs=[pl.BlockSpec((r, d), lambda l, j: (0, 0)),
                  pl.BlockSpec((None, d, d), lambda l, j: (l, 0, j)),
                  pl.BlockSpec((None, 1, d), lambda l, j: (l, 0, j))],
        out_specs=pl.BlockSpec((None, r, d), lambda l, j: (l, 0, j)),
        out_shape=jax.ShapeDtypeStruct((depth, r, six_d), F32),
        compiler_params=_cparams("parallel", "parallel"),
        name="modulation",
    )(cc, w_mod, b_mod.reshape(depth, 1, six_d))


def _proj_even_body(e, n_even, x_ref, mod_ref, w_ref, lbl_ref, a_ref, q_ref, lf_ref, k_ref, v_ref, g_ref):
    mod = mod_ref[...]
    h = _modulate(x_ref[...], mod[0:1], mod[1:2])
    rows = [lbl_ref[i:i + 1, :] for i in range(n_even)]
    mx = functools.reduce(jnp.maximum, rows)
    ex = [jnp.exp(r - mx) for r in rows]
    den = functools.reduce(lambda a, b: a + b, ex)
    lb = jnp.zeros_like(mx)
    for i in range(1, e + 1):
        lb = lb + ex[i] / den
    w = HG_W
    a_ref[...] = _dot(h, w_ref[:, 0:w]).astype(BF)
    q_ref[...] = _silu(_dot(h, w_ref[:, w:2 * w])).astype(BF)
    for d in range(2):
        z = _dot(h, w_ref[:, (2 + d) * w:(3 + d) * w])
        lbd = lb[:, d * w:(d + 1) * w]
        f = lbd + (1.0 - lbd) * jax.nn.sigmoid(z)
        lf_ref[:, d * w:(d + 1) * w] = jnp.log2(f)
        k_ref[:, d * w:(d + 1) * w] = (1.0 - f).astype(BF)
    v_ref[...] = _dot(h, w_ref[:, 4 * w:5 * w]).astype(BF)
    g_ref[...] = _dot(h, w_ref[:, 5 * w:6 * w]).astype(BF)


def _proj_even(x, mod, w_in, lb_logits, e):
    b, l, d = x.shape
    n_even = lb_logits.shape[0]
    tm = _row_tile(l, 512)
    w = HG_W
    row = lambda width: pl.BlockSpec((None, tm, width), lambda i, t: (i, t, 0))
    shp = lambda width, dt: jax.ShapeDtypeStruct((b, l, width), dt)
    return pl.pallas_call(
        functools.partial(_proj_even_body, e, n_even),
        grid=(b, l // tm),
        in_specs=[row(d),
                  pl.BlockSpec((None, 6, d), lambda i, t: (i, 0, 0)),
                  pl.BlockSpec(w_in.shape, lambda i, t: (0, 0)),
                  pl.BlockSpec(lb_logits.shape, lambda i, t: (0, 0))],
        out_specs=[row(w), row(w), row(2 * w), row(2 * w), row(w), row(w)],
        out_shape=[shp(w, BF), shp(w, BF), shp(2 * w, F32), shp(2 * w, BF), shp(w, BF), shp(w, BF)],
        compiler_params=_cparams("parallel", "parallel"),
        name="proj_even",
    )(x, mod, w_in, lb_logits)


@functools.lru_cache(maxsize=None)
def _dft_tables(n):
    idx = np.arange(n, dtype=np.int64)
    ang = 2.0 * np.pi * ((idx[:, None] * idx[None, :]) % n).astype(np.float64) / n
    s = 1.0 / math.sqrt(n)
    return (np.cos(ang) * s).astype(np.float32), (np.sin(ang) * s).astype(np.float32)


def _dft_body(c_ref, s_ref, a_ref, cs_ref, o_ref):
    a = a_ref[...]
    p = _dot(c_ref[...], a).astype(BF)
    q = _dot(s_ref[...], a).astype(BF)
    cs = cs_ref[...]
    for g in range(FNET_GROUPS):
        sl = slice(g * LANE, (g + 1) * LANE)
        pq = jnp.concatenate([p[:, sl], q[:, sl]], axis=1)
        o_ref[:, sl] = _dot(pq, cs).astype(BF)


def _dft_mix(a):
    b, l, w = a.shape
    cl, sl_ = _dft_tables(l)
    cc, sc = _dft_tables(LANE)
    cos_l = jnp.asarray(cl).astype(BF)
    sin_l = jnp.asarray(sl_).astype(BF)
    cs = jnp.concatenate([jnp.asarray(cc), -jnp.asarray(sc)], axis=0).astype(BF)
    tm = _row_tile(l, 512)
    return pl.pallas_call(
        _dft_body,
        grid=(l // tm, b),
        in_specs=[pl.BlockSpec((tm, l), lambda m, i: (m, 0)),
                  pl.BlockSpec((tm, l), lambda m, i: (m, 0)),
                  pl.BlockSpec((None, l, w), lambda m, i: (i, 0, 0)),
                  pl.BlockSpec(cs.shape, lambda m, i: (0, 0))],
        out_specs=pl.BlockSpec((None, tm, w), lambda m, i: (i, m, 0)),
        out_shape=jax.ShapeDtypeStruct((b, l, w), BF),
        compiler_params=_cparams("parallel", "parallel"),
        name="dft_mix",
    )(cos_l, sin_l, a, cs)


def _scan_chunk(rev, q_ref, k_ref, lf_ref, v_ref, o_ref, st_ref):
    c, w = lf_ref.shape
    nlev = c.bit_length() - 1
    lf = lf_ref[...]
    r = lax.broadcasted_iota(jnp.int32, (c, w), 0)
    pos = (c - 1 - r) if rev else r

    def from_prev(x, sh):
        return pltpu.roll(x, (c - sh) if rev else sh, axis=0)

    def from_next(x, sh):
        return pltpu.roll(x, sh if rev else (c - sh), axis=0)

    def neg_abs(x):
        return pltpu.bitcast(pltpu.bitcast(x, jnp.uint32) | jnp.uint32(0x80000000), F32)

    def spread_half(x, sh, low):
        parts = []
        for i in range(c // (2 * sh)):
            first, second = x[2 * i * sh:(2 * i + 1) * sh], x[(2 * i + 1) * sh:(2 * i + 2) * sh]
            src = (second if rev else first) if low else (first if rev else second)
            parts += [src, src]
        return jnp.concatenate(parts, axis=0)

    b = lf
    for j in range(nlev):
        sh = 1 << j
        if sh % 8 == 0:
            zero = jnp.zeros((sh, w), F32)
            b = b + (jnp.concatenate([b[sh:], zero], axis=0) if rev else jnp.concatenate([zero, b[:c - sh]], axis=0))
        else:
            b = b + jnp.where(pos >= sh, from_prev(b, sh), 0.0)

    q = q_ref[...].astype(F32)
    k = k_ref[...].astype(F32)
    ti = lax.broadcasted_iota(jnp.int32, (c, c), 0)
    si = lax.broadcasted_iota(jnp.int32, (c, c), 1)
    later = (ti < si) if rev else (ti > si)
    diff = ti ^ si
    att = [jnp.zeros((c, c), F32) for _ in range(HG_HEADS)]
    y = b
    for j in range(nlev):
        sh = 1 << j
        upper = ((pos >> j) & 1) == 1
        bound = spread_half(y, sh, True) if sh % 8 == 0 else jnp.where(upper, from_prev(y, sh), y)
        e = jnp.exp2(neg_abs(b - bound))
        qm = (q * e).astype(BF)
        km = (k * e).astype(BF)
        level = jnp.logical_and((diff >> j) == 1, later)
        for h in range(HG_HEADS):
            sl = slice(h * LANE, (h + 1) * LANE)
            att[h] = att[h] + jnp.where(level, _dot_nt(qm[:, sl], km[:, sl]), 0.0)
        y = spread_half(y, sh, False) if sh % 8 == 0 else jnp.where(upper, y, from_next(y, sh))
    b_last = y
    qe = (q * jnp.exp2(b)).astype(BF)
    kd = (k * jnp.exp2(b_last - b)).astype(BF)
    carry = jnp.exp2(b_last[0:1, :])
    v = v_ref[...]
    vf = v.astype(F32)
    for h in range(HG_HEADS):
        sl = slice(h * LANE, (h + 1) * LANE)
        st = st_ref[h]
        inter = _dot_nt(qe[:, sl], st.astype(BF))
        diag = jnp.sum(q[:, sl] * k[:, sl], axis=-1, keepdims=True)
        o_ref[:, sl] = (inter + _dot(att[h].astype(BF), v[:, sl]) + diag * vf[:, sl]).astype(o_ref.dtype)
        st_ref[h] = st * carry[:, sl] + _dot_tn(v[:, sl], kd[:, sl])


def _scan_body(n_ctx, *refs):
    ins, outs, states = refs[:16], refs[16:20], refs[20:]
    s = pl.program_id(1)

    @pl.when(s == 0)
    def _():
        for st_ref in states:
            st_ref[...] = jnp.zeros_like(st_ref)

    def run(stream):
        for d in range(2):
            base = 8 * d + 4 * stream
            _scan_chunk(d == 1, *ins[base:base + 4], outs[2 * d + stream], states[d])

    @pl.when(s < n_ctx)
    def _():
        run(0)

    @pl.when(s >= n_ctx)
    def _():
        run(1)


def _scan(pc, pl_):
    qc, lfc, kc, vc = pc
    ql, lfl, kl, vl = pl_
    b, lc, w = qc.shape
    ll = ql.shape[1]
    c = SCAN_CHUNK
    n_ctx, n_lat = lc // c, ll // c

    def spec(idx, col):
        return pl.BlockSpec((None, c, w), lambda i, s: (i, idx(s), col))

    in_specs, out_specs, args = [], [], []
    for d in range(2):
        if d:
            ci = lambda s: jnp.maximum(n_ctx - 1 - s, 0)
            li = lambda s: n_lat - 1 - jnp.maximum(s - n_ctx, 0)
        else:
            ci = lambda s: jnp.minimum(s, n_ctx - 1)
            li = lambda s: jnp.maximum(s - n_ctx, 0)
        in_specs += [spec(ci, 0), spec(ci, d), spec(ci, d), spec(ci, 0),
                     spec(li, 0), spec(li, d), spec(li, d), spec(li, 0)]
        args += [qc, kc, lfc, vc, ql, kl, lfl, vl]
        out_specs += [spec(ci, 0), spec(li, 0)]
    shp = lambda l: jax.ShapeDtypeStruct((b, l, w), BF)
    of_c, of_l, ob_c, ob_l = pl.pallas_call(
        functools.partial(_scan_body, n_ctx),
        grid=(b, n_ctx + n_lat),
        in_specs=in_specs,
        out_specs=out_specs,
        out_shape=[shp(lc), shp(ll), shp(lc), shp(ll)],
        scratch_shapes=[pltpu.VMEM((HG_HEADS, HEAD_DIM, HEAD_DIM), F32) for _ in range(2)],
        compiler_params=_cparams("parallel", "arbitrary"),
        name="hgrn_scan",
    )(*args)
    return (of_c, of_l), (ob_c, ob_l)


def _out_even_body(yf_ref, of_ref, ob_ref, g_ref, x_ref, mod_ref, gn_ref, w_ref, o_ref):
    o = of_ref[...].astype(F32) + ob_ref[...].astype(F32)
    g = g_ref[...].astype(F32)
    gn = gn_ref[...]
    parts = []
    for h in range(HG_HEADS):
        sl = slice(h * LANE, (h + 1) * LANE)
        parts.append((_rms(o[:, sl]) * gn[:, sl] * _silu(g[:, sl])).astype(BF))
    gated = jnp.concatenate(parts, axis=1)
    w = HG_W
    y = _dot(yf_ref[...], w_ref[0:w, :]) + _dot(gated, w_ref[w:2 * w, :])
    o_ref[...] = x_ref[...] + mod_ref[2:3, :] * y


def _out_even(yf, o_f, o_b, g, x, mod, gn, w_out):
    b, l, d = x.shape
    tm = _row_tile(l, 512)
    row = lambda width: pl.BlockSpec((None, tm, width), lambda i, t: (i, t, 0))
    return pl.pallas_call(
        _out_even_body,
        grid=(b, l // tm),
        in_specs=[row(HG_W), row(HG_W), row(HG_W), row(HG_W), row(d),
                  pl.BlockSpec((None, 6, d), lambda i, t: (i, 0, 0)),
                  pl.BlockSpec(gn.shape, lambda i, t: (0, 0)),
                  pl.BlockSpec(w_out.shape, lambda i, t: (0, 0))],
        out_specs=row(d),
        out_shape=jax.ShapeDtypeStruct((b, l, d), F32),
        compiler_params=_cparams("parallel", "parallel"),
        name="out_even",
    )(yf, o_f, o_b, g, x, mod, gn, w_out)


def _proj_att_body(x_ref, mod_ref, w_ref, qg_ref, kg_ref, cos_ref, sin_ref, q_ref, k_ref, v_ref):
    mod = mod_ref[...]
    h = _modulate(x_ref[...], mod[0:1], mod[1:2])
    cos = cos_ref[...]
    sin = sin_ref[...]
    scale = HEAD_DIM ** -0.5 * math.log2(math.e)
    for i in range(ATT_HEADS + ATT_KV_HEADS):
        sl = slice(i * LANE, (i + 1) * LANE)
        p = _dot(h, w_ref[:, sl])
        gain = qg_ref[...] if i < ATT_HEADS else kg_ref[...]
        p = _rms(p) * gain
        p = p * cos + pltpu.roll(p, HEAD_DIM // 2, axis=1) * sin
        if i < ATT_HEADS:
            q_ref[:, sl] = (p * scale).astype(BF)
        else:
            k_ref[:, (i - ATT_HEADS) * LANE:(i - ATT_HEADS + 1) * LANE] = p.astype(BF)
    nqk = (ATT_HEADS + ATT_KV_HEADS) * LANE
    v_ref[...] = _dot(h, w_ref[:, nqk:nqk + ATT_KV_HEADS * LANE]).astype(BF)


def _proj_att(x, mod, w_qkv, qg, kg, cos, sin):
    b, l, d = x.shape
    tm = _row_tile(l, 512)
    qw, kw = ATT_HEADS * LANE, ATT_KV_HEADS * LANE
    row = lambda width: pl.BlockSpec((None, tm, width), lambda i, t: (i, t, 0))
    const = lambda a: pl.BlockSpec(a.shape, lambda i, t: (0, 0))
    tab = pl.BlockSpec((tm, LANE), lambda i, t: (t, 0))
    return pl.pallas_call(
        _proj_att_body,
        grid=(b, l // tm),
        in_specs=[row(d), pl.BlockSpec((None, 6, d), lambda i, t: (i, 0, 0)),
                  const(w_qkv), const(qg), const(kg), tab, tab],
        out_specs=[row(qw), row(kw), row(kw)],
        out_shape=[jax.ShapeDtypeStruct((b, l, qw), BF), jax.ShapeDtypeStruct((b, l, kw), BF),
                   jax.ShapeDtypeStruct((b, l, kw), BF)],
        compiler_params=_cparams("parallel", "parallel"),
        name="proj_att",
    )(x, mod, w_qkv, qg, kg, cos, sin)


@functools.lru_cache(maxsize=None)
def _rope_tables(l):
    t = np.arange(l)
    row = (t // GRID_W).astype(np.float64)
    col = (t % GRID_W).astype(np.float64)
    n_freq = HEAD_DIM // 4
    freqs = ROPE_THETA ** (-np.arange(n_freq, dtype=np.float64) / n_freq)
    ang = np.concatenate([row[:, None] * freqs, col[:, None] * freqs], axis=-1)
    cos, sin = np.cos(ang), np.sin(ang)
    return (np.concatenate([cos, cos], axis=-1).astype(np.float32),
            np.concatenate([-sin, sin], axis=-1).astype(np.float32))


ATT_KV_BLOCK = 256


def _attn_body(n_src, q_ref, *refs):
    o_ref = refs[2 * n_src]
    tq = q_ref.shape[0]
    n = ATT_GROUP * tq
    vt_rows = refs[1].shape[0]
    q = q_ref[...]
    qs = jnp.concatenate([q[:, g * LANE:(g + 1) * LANE] for g in range(ATT_GROUP)], axis=0)
    blocks = []
    for i in range(n_src):
        k_ref, vt_ref = refs[2 * i], refs[2 * i + 1]
        lk = k_ref.shape[0]
        bk = min(ATT_KV_BLOCK, lk)
        blocks += [(k_ref, vt_ref, j * bk, bk) for j in range(lk // bk)]

    def scores(blk):
        k_ref, _, off, bk = blk
        return _dot_nt(k_ref[off:off + bk, :], qs)

    m = jnp.full((1, n), -1e30, F32)
    acc = jnp.zeros((vt_rows, n), F32)
    def accumulate(acc, blk, alpha, p):
        _, vt_ref, off, bk = blk
        return alpha * acc + _dot(vt_ref[:, off:off + bk], p)

    nb = len(blocks)
    ahead = [scores(blocks[j]) for j in range(min(2, nb))]
    pending = None
    for j in range(nb):
        s = ahead.pop(0)
        if j + 2 < nb:
            ahead.append(scores(blocks[j + 2]))
        m_new = jnp.maximum(m, jnp.max(s, axis=0, keepdims=True))
        alpha = jnp.exp2(m - m_new)
        p = jnp.exp2((s - m_new).astype(BF))
        m = m_new
        if pending is not None:
            acc = accumulate(acc, *pending)
        pending = (blocks[j], alpha, p)
    acc = accumulate(acc, *pending)

    o = acc[:HEAD_DIM] * (1.0 / acc[HEAD_DIM:HEAD_DIM + 1])
    for g in range(ATT_GROUP):
        o_ref[:, g * LANE:(g + 1) * LANE] = o[:, g * tq:(g + 1) * tq].T.astype(BF)


def _values_t(v):
    b, l, _ = v.shape
    vt = v.reshape(b, l, ATT_KV_HEADS, HEAD_DIM).transpose(0, 2, 3, 1)
    return jnp.concatenate([vt, jnp.ones((b, ATT_KV_HEADS, 16, l), v.dtype)], axis=2)


def _attention(q, kvs, tq):
    b, lq, qw = q.shape
    gw = ATT_GROUP * LANE
    in_specs = [pl.BlockSpec((None, tq, gw), lambda i, h, t: (i, t, h))]
    args = [q]
    for k, vt in kvs:
        lk = k.shape[1]
        in_specs.append(pl.BlockSpec((None, lk, LANE), lambda i, h, t: (i, 0, h)))
        in_specs.append(pl.BlockSpec((None, None, vt.shape[2], lk), lambda i, h, t: (i, h, 0, 0)))
        args += [k, vt]
    return pl.pallas_call(
        functools.partial(_attn_body, len(kvs)),
        grid=(b, ATT_KV_HEADS, lq // tq),
        in_specs=in_specs,
        out_specs=pl.BlockSpec((None, tq, gw), lambda i, h, t: (i, t, h)),
        out_shape=jax.ShapeDtypeStruct((b, lq, qw), BF),
        compiler_params=_cparams("parallel", "parallel", "parallel"),
        name="attention",
    )(*args)


def _out_att_body(o_ref, x_ref, mod_ref, w_ref, y_ref):
    y_ref[...] = x_ref[...] + mod_ref[2:3, :] * _dot(o_ref[...], w_ref[...])


def _out_att(o, x, mod, w_out):
    b, l, d = x.shape
    tm = _row_tile(l, 512)
    row = lambda width: pl.BlockSpec((None, tm, width), lambda i, t: (i, t, 0))
    return pl.pallas_call(
        _out_att_body,
        grid=(b, l // tm),
        in_specs=[row(o.shape[-1]), row(d), pl.BlockSpec((None, 6, d), lambda i, t: (i, 0, 0)),
                  pl.BlockSpec(w_out.shape, lambda i, t: (0, 0))],
        out_specs=row(d),
        out_shape=jax.ShapeDtypeStruct((b, l, d), F32),
        compiler_params=_cparams("parallel", "parallel"),
        name="out_att",
    )(o, x, mod, w_out)


def _ffn_body(on_grid, n_chunk, has_final, *refs):
    if on_grid:
        xp_ref, xm_ref, xn_ref = refs[:3]
        refs = refs[3:]
    else:
        xm_ref = refs[0]
        refs = refs[1:]
    mod_ref, wg_ref, wv_ref, cw_ref, cb_ref, wd_ref = refs[:6]
    fg_ref = refs[6] if has_final else None
    o_ref, h_buf = refs[-2:]
    tm = xm_ref.shape[0]
    ck = wg_ref.shape[-1]
    halo = GRID_W if on_grid else 0
    rows = tm + 2 * halo
    mod = mod_ref[...]
    md = lambda x: _modulate(x, mod[3:4], mod[4:5])
    if on_grid:
        h_buf[0:halo, :] = md(xp_ref[...])
        h_buf[halo:halo + tm, :] = md(xm_ref[...])
        h_buf[halo + tm:rows, :] = md(xn_ref[...])
    else:
        h_buf[...] = md(xm_ref[...])
    r = lax.broadcasted_iota(jnp.int32, (rows, ck), 0)
    if on_grid:
        t = pl.program_id(1)
        nt = pl.num_programs(1)
        col = r & (GRID_W - 1)
        has_left = col != 0
        has_right = col != GRID_W - 1
        inside = jnp.logical_and(jnp.logical_or(r >= halo, t > 0),
                                 jnp.logical_or(r < halo + tm, t < nt - 1))
    else:
        has_left = r != 0
        has_right = r != rows - 1

    hm_sl = slice(halo, halo + tm)

    def up(c):
        gate = _dot(h_buf[...], wg_ref[c])
        val = _dot(h_buf[hm_sl, :], wv_ref[c])
        return gate, val

    def down(c, gate, val):
        if on_grid:
            gate = jnp.where(inside, gate, 0.0)
        gl = jnp.where(has_left, pltpu.roll(gate, 1, axis=0), 0.0)
        gr = jnp.where(has_right, pltpu.roll(gate, rows - 1, axis=0), 0.0)
        cw = cw_ref[c]
        if on_grid:
            conv = None
            for dy in range(3):
                sl = slice(dy * halo, dy * halo + tm)
                term = (cw[3 * dy:3 * dy + 1] * gl[sl] + cw[3 * dy + 1:3 * dy + 2] * gate[sl]
                        + cw[3 * dy + 2:3 * dy + 3] * gr[sl])
                conv = term if conv is None else conv + term
        else:
            conv = cw[3:4] * gl + cw[4:5] * gate + cw[5:6] * gr
        act = (_silu(conv + cb_ref[c]) * val).astype(BF)
        return _dot(act, wd_ref[c])

    nxt = up(0)
    acc = None
    for c in range(n_chunk):
        cur = nxt
        if c + 1 < n_chunk:
            nxt = up(c + 1)
        part = down(c, *cur)
        acc = part if acc is None else acc + part
    y = xm_ref[...] + mod[5:6] * acc
    if has_final:
        y = _rms(y) * fg_ref[...]
    o_ref[...] = y


def _ffn(x, mod, wg, wv, cw, cb, wd, on_grid, final_g=None):
    b, l, d = x.shape
    n_chunk, _, ck = wg.shape
    extra = [] if final_g is None else [final_g.reshape(1, d)]
    const = lambda a: pl.BlockSpec(a.shape, lambda i, t: (0,) * a.ndim, pipeline_mode=pl.Buffered(1))
    modspec = pl.BlockSpec((None, 6, d), lambda i, t: (i, 0, 0))
    if on_grid:
        tm = 512
        per = tm // GRID_W
        n_rows = l // GRID_W
        x_specs = [pl.BlockSpec((None, GRID_W, d), lambda i, t: (i, jnp.maximum(t * per - 1, 0), 0)),
                   pl.BlockSpec((None, tm, d), lambda i, t: (i, t, 0)),
                   pl.BlockSpec((None, GRID_W, d), lambda i, t: (i, jnp.minimum((t + 1) * per, n_rows - 1), 0))]
        xs = [x, x, x]
        rows = tm + 2 * GRID_W
    else:
        tm = l
        x_specs = [pl.BlockSpec((None, tm, d), lambda i, t: (i, t, 0))]
        xs = [x]
        rows = tm
    return pl.pallas_call(
        functools.partial(_ffn_body, on_grid, n_chunk, final_g is not None),
        grid=(b, l // tm),
        in_specs=(x_specs + [modspec, const(wg), const(wv), const(cw), const(cb), const(wd)]
                  + [const(a) for a in extra]),
        out_specs=pl.BlockSpec((None, tm, d), lambda i, t: (i, t, 0)),
        out_shape=jax.ShapeDtypeStruct((b, l, d), F32),
        scratch_shapes=[pltpu.VMEM((rows, d), BF)],
        compiler_params=_cparams("parallel", "parallel"),
        name="ffn_grid" if on_grid else "ffn_seq",
    )(*xs, mod, wg, wv, cw, cb, wd, *extra)


def _ffn_chunk(d_ff):
    for ck in (256, 128):
        if d_ff % ck == 0:
            return ck
    return d_ff


def _even_mixer(x, ctx, ml, mc, w_in, w_out, lb_logits, gn, e, need_ctx):
    w_in = w_in.astype(BF)
    w_out = w_out.astype(BF)
    gn = gn.reshape(1, -1)
    lbl = lb_logits.reshape(lb_logits.shape[0], -1).astype(F32)
    a_l, q_l, lf_l, k_l, v_l, g_l = _proj_even(x, ml, w_in, lbl, e)
    a_c, q_c, lf_c, k_c, v_c, g_c = _proj_even(ctx, mc, w_in, lbl, e)
    pc, pl_ = (q_c, lf_c, k_c, v_c), (q_l, lf_l, k_l, v_l)
    (of_c, of_l), (ob_c, ob_l) = _scan(pc, pl_)
    x = _out_even(_dft_mix(a_l), of_l, ob_l, g_l, x, ml, gn, w_out)
    if need_ctx:
        ctx = _out_even(_dft_mix(a_c), of_c, ob_c, g_c, ctx, mc, gn, w_out)
    return x, ctx


def _att_mixer(x, ctx, ml, mc, w_qkv, qn_g, kn_g, w_out, need_ctx):
    l, lc = x.shape[1], ctx.shape[1]
    perm = np.concatenate([np.arange(0, HEAD_DIM, 2), np.arange(1, HEAD_DIM, 2)])
    n_qk = ATT_HEADS + ATT_KV_HEADS
    cols = np.concatenate([h * HEAD_DIM + perm for h in range(n_qk)]
                          + [np.arange(n_qk * HEAD_DIM, w_qkv.shape[-1])])
    w_in = w_qkv[:, cols].astype(BF)
    w_out = w_out.astype(BF)
    qg = qn_g[perm].reshape(1, -1)
    kg = kn_g[perm].reshape(1, -1)
    cos_l, sin_l = (jnp.asarray(t) for t in _rope_tables(l))
    cos_c, sin_c = jnp.ones((lc, HEAD_DIM), F32), jnp.zeros((lc, HEAD_DIM), F32)
    q_l, k_l, v_l = _proj_att(x, ml, w_in, qg, kg, cos_l, sin_l)
    q_c, k_c, v_c = _proj_att(ctx, mc, w_in, qg, kg, cos_c, sin_c)
    vt_l, vt_c = _values_t(v_l), _values_t(v_c)
    x = _out_att(_attention(q_l, [(k_l, vt_l), (k_c, vt_c)], min(l, 128)), x, ml, w_out)
    if need_ctx:
        ctx = _out_att(_attention(q_c, [(k_c, vt_c)], lc), ctx, mc, w_out)
    return x, ctx


def _ffn_layer(x, ctx, ml, mc, w_up, conv_w, conv_b, w_down, need_ctx, final_g=None):
    d = x.shape[-1]
    d_ff = w_down.shape[0]
    ck = _ffn_chunk(d_ff)
    n_chunk = d_ff // ck
    wu = w_up.astype(BF)
    wg = wu[:, :d_ff].reshape(d, n_chunk, ck).transpose(1, 0, 2)
    wv = wu[:, d_ff:].reshape(d, n_chunk, ck).transpose(1, 0, 2)
    cw = conv_w.reshape(9, n_chunk, ck).transpose(1, 0, 2)
    cb = conv_b.reshape(n_chunk, 1, ck)
    wd = w_down.astype(BF).reshape(n_chunk, ck, d)
    x = _ffn(x, ml, wg, wv, cw, cb, wd, True, final_g)
    if need_ctx:
        ctx = _ffn(ctx, mc, wg, wv, cw, cb, wd, False)
    return x, ctx


def _mod_vectors(c, c_ctx, w_mod, b_mod):
    b, d = c.shape
    depth = w_mod.shape[0]
    pad = (-(b + 1)) % 8
    cc = jnp.concatenate([c, c_ctx[None, :], jnp.zeros((pad, d), F32)], axis=0)
    mods = _modulation(cc, w_mod, b_mod)
    mod_lat = mods[:, :b].reshape(depth, b, 6, d)
    mod_ctx = jnp.broadcast_to(mods[:, b].reshape(depth, 1, 6, d), (depth, b, 6, d))
    return mod_lat, mod_ctx


def kernel(x, c, ctx, c_ctx, w_mod, b_mod, w_in_ab, w_out_ab, hg_lb_logits, hg_norm_g, w_qkv, q_norm_g,
           k_norm_g, w_out_att, w_up, conv_w, conv_b, w_down, final_norm_g):
    depth = w_mod.shape[0]
    mod_lat, mod_ctx = _mod_vectors(c, c_ctx, w_mod, b_mod)
    for layer in range(depth):
        need_ctx = layer != depth - 1
        ml, mc = mod_lat[layer], mod_ctx[layer]
        if layer % 2 == 0:
            e = layer // 2
            x, ctx = _even_mixer(x, ctx, ml, mc, w_in_ab[e], w_out_ab[e], hg_lb_logits, hg_norm_g[e], e, need_ctx)
        else:
            o = layer // 2
            x, ctx = _att_mixer(x, ctx, ml, mc, w_qkv[o], q_norm_g[o], k_norm_g[o], w_out_att[o], need_ctx)
        x, ctx = _ffn_layer(x, ctx, ml, mc, w_up[layer], conv_w[layer], conv_b[layer], w_down[layer], need_ctx,
                            None if need_ctx else final_norm_g)
    return x
```

```python
import functools
import math

import numpy as np
import jax
import jax.numpy as jnp
from jax import lax
from jax.experimental import pallas as pl
from jax.experimental.pallas import tpu as pltpu

F32 = jnp.float32
BF = jnp.bfloat16

EPS = 1e-6
GRID_W = 64
LANE = 128
HEAD_DIM = 128
FNET_GROUPS = 4
HG_HEADS = 4
HG_W = HG_HEADS * HEAD_DIM
ATT_HEADS = 8
ATT_KV_HEADS = 2
ATT_GROUP = ATT_HEADS // ATT_KV_HEADS
ROPE_THETA = 10000.0
SCAN_CHUNK = 128
FFN_TILE = 512
VMEM_LIMIT = 56 * 1024 * 1024


def _cparams(*sem):
    return pltpu.CompilerParams(dimension_semantics=sem, vmem_limit_bytes=VMEM_LIMIT)


def _dot(a, b):
    return jnp.dot(a, b, preferred_element_type=F32)


def _dot_nt(a, b):
    return lax.dot_general(a, b, (((1,), (1,)), ((), ())), preferred_element_type=F32)


def _dot_tn(a, b):
    return lax.dot_general(a, b, (((0,), (0,)), ((), ())), preferred_element_type=F32)


def _rms(x):
    return x * lax.rsqrt(jnp.mean(x * x, axis=-1, keepdims=True) + EPS)


def _silu(x):
    return x * jax.nn.sigmoid(x)


def _modulate(x, shift, scale):
    return (_rms(x) * (1.0 + scale) + shift).astype(BF)


def _row_tile(n, pref):
    return pref if n % pref == 0 else n


def _mod_body(c_ref, w_ref, b_ref, o_ref):
    s = _silu(c_ref[...]).astype(BF)
    o_ref[...] = _dot(s, w_ref[...].astype(BF)) + b_ref[...]


def _modulation(cc, w_mod, b_mod):
    depth, d, six_d = w_mod.shape
    r = cc.shape[0]
    return pl.pallas_call(
        _mod_body,
        grid=(depth, six_d // d),
        in_specs=[pl.BlockSpec((r, d), lambda l, j: (0, 0)),
                  pl.BlockSpec((None, d, d), lambda l, j: (l, 0, j)),
                  pl.BlockSpec((None, 1, d), lambda l, j: (l, 0, j))],
        out_specs=pl.BlockSpec((None, r, d), lambda l, j: (l, 0, j)),
        out_shape=jax.ShapeDtypeStruct((depth, r, six_d), F32),
        compiler_params=_cparams("parallel", "parallel"),
        name="modulation",
    )(cc, w_mod, b_mod.reshape(depth, 1, six_d))


def _proj_even_body(e, n_even, x_ref, mod_ref, w_ref, lbl_ref, a_ref, q_ref, lf_ref, k_ref, v_ref, g_ref):
    mod = mod_ref[...]
    h = _modulate(x_ref[...], mod[0:1], mod[1:2])
    rows = [lbl_ref[i:i + 1, :] for i in range(n_even)]
    mx = functools.reduce(jnp.maximum, rows)
    ex = [jnp.exp(r - mx) for r in rows]
    den = functools.reduce(lambda a, b: a + b, ex)
    lb = jnp.zeros_like(mx)
    for i in range(1, e + 1):
        lb = lb + ex[i] / den
    w = HG_W
    a_ref[...] = _dot(h, w_ref[:, 0:w]).astype(BF)
    q_ref[...] = _silu(_dot(h, w_ref[:, w:2 * w])).astype(BF)
    for d in range(2):
        z = _dot(h, w_ref[:, (2 + d) * w:(3 + d) * w])
        lbd = lb[:, d * w:(d + 1) * w]
        f = lbd + (1.0 - lbd) * jax.nn.sigmoid(z)
        lf_ref[:, d * w:(d + 1) * w] = jnp.log2(f)
        k_ref[:, d * w:(d + 1) * w] = (1.0 - f).astype(BF)
    v_ref[...] = _dot(h, w_ref[:, 4 * w:5 * w]).astype(BF)
    g_ref[...] = _dot(h, w_ref[:, 5 * w:6 * w]).astype(BF)


def _proj_even(x, mod, w_in, lb_logits, e):
    b, l, d = x.shape
    n_even = lb_logits.shape[0]
    tm = _row_tile(l, 512)
    w = HG_W
    row = lambda width: pl.BlockSpec((None, tm, width), lambda i, t: (i, t, 0))
    shp = lambda width, dt: jax.ShapeDtypeStruct((b, l, width), dt)
    return pl.pallas_call(
        functools.partial(_proj_even_body, e, n_even),
        grid=(b, l // tm),
        in_specs=[row(d),
                  pl.BlockSpec((None, 6, d), lambda i, t: (i, 0, 0)),
                  pl.BlockSpec(w_in.shape, lambda i, t: (0, 0)),
                  pl.BlockSpec(lb_logits.shape, lambda i, t: (0, 0))],
        out_specs=[row(w), row(w), row(2 * w), row(2 * w), row(w), row(w)],
        out_shape=[shp(w, BF), shp(w, BF), shp(2 * w, F32), shp(2 * w, BF), shp(w, BF), shp(w, BF)],
        compiler_params=_cparams("parallel", "parallel"),
        name="proj_even",
    )(x, mod, w_in, lb_logits)


@functools.lru_cache(maxsize=None)
def _dft_tables(n):
    idx = np.arange(n, dtype=np.int64)
    ang = 2.0 * np.pi * ((idx[:, None] * idx[None, :]) % n).astype(np.float64) / n
    s = 1.0 / math.sqrt(n)
    return (np.cos(ang) * s).astype(np.float32), (np.sin(ang) * s).astype(np.float32)


def _dft_body(c_ref, s_ref, a_ref, cs_ref, o_ref):
    a = a_ref[...]
    p = _dot(c_ref[...], a).astype(BF)
    q = _dot(s_ref[...], a).astype(BF)
    cs = cs_ref[...]
    for g in range(FNET_GROUPS):
        sl = slice(g * LANE, (g + 1) * LANE)
        pq = jnp.concatenate([p[:, sl], q[:, sl]], axis=1)
        o_ref[:, sl] = _dot(pq, cs).astype(BF)


def _dft_mix(a):
    b, l, w = a.shape
    cl, sl_ = _dft_tables(l)
    cc, sc = _dft_tables(LANE)
    cos_l = jnp.asarray(cl).astype(BF)
    sin_l = jnp.asarray(sl_).astype(BF)
    cs = jnp.concatenate([jnp.asarray(cc), -jnp.asarray(sc)], axis=0).astype(BF)
    tm = _row_tile(l, 512)
    return pl.pallas_call(
        _dft_body,
        grid=(l // tm, b),
        in_specs=[pl.BlockSpec((tm, l), lambda m, i: (m, 0)),
                  pl.BlockSpec((tm, l), lambda m, i: (m, 0)),
                  pl.BlockSpec((None, l, w), lambda m, i: (i, 0, 0)),
                  pl.BlockSpec(cs.shape, lambda m, i: (0, 0))],
        out_specs=pl.BlockSpec((None, tm, w), lambda m, i: (i, m, 0)),
        out_shape=jax.ShapeDtypeStruct((b, l, w), BF),
        compiler_params=_cparams("parallel", "parallel"),
        name="dft_mix",
    )(cos_l, sin_l, a, cs)


def _scan_chunk(rev, q_ref, k_ref, lf_ref, v_ref, o_ref, st_ref):
    c, w = lf_ref.shape
    nlev = c.bit_length() - 1
    lf = lf_ref[...]
    r = lax.broadcasted_iota(jnp.int32, (c, w), 0)
    pos = (c - 1 - r) if rev else r

    def from_prev(x, sh):
        return pltpu.roll(x, (c - sh) if rev else sh, axis=0)

    def from_next(x, sh):
        return pltpu.roll(x, sh if rev else (c - sh), axis=0)

    def neg_abs(x):
        return pltpu.bitcast(pltpu.bitcast(x, jnp.uint32) | jnp.uint32(0x80000000), F32)

    def spread_half(x, sh, low):
        parts = []
        for i in range(c // (2 * sh)):
            first, second = x[2 * i * sh:(2 * i + 1) * sh], x[(2 * i + 1) * sh:(2 * i + 2) * sh]
            src = (second if rev else first) if low else (first if rev else second)
            parts += [src, src]
        return jnp.concatenate(parts, axis=0)

    b = lf
    for j in range(nlev):
        sh = 1 << j
        if sh % 8 == 0:
            zero = jnp.zeros((sh, w), F32)
            b = b + (jnp.concatenate([b[sh:], zero], axis=0) if rev else jnp.concatenate([zero, b[:c - sh]], axis=0))
        else:
            b = b + jnp.where(pos >= sh, from_prev(b, sh), 0.0)

    q = q_ref[...]
    k = k_ref[...]
    ti = lax.broadcasted_iota(jnp.int32, (c, c), 0)
    si = lax.broadcasted_iota(jnp.int32, (c, c), 1)
    later = (ti < si) if rev else (ti > si)
    diff = ti ^ si
    att = [jnp.zeros((c, c), F32) for _ in range(HG_HEADS)]
    y = b
    for j in range(nlev):
        sh = 1 << j
        upper = ((pos >> j) & 1) == 1
        bound = spread_half(y, sh, True) if sh % 8 == 0 else jnp.where(upper, from_prev(y, sh), y)
        e = jnp.exp2(neg_abs(b - bound)).astype(BF)
        qm = q * e
        km = k * e
        level = jnp.logical_and((diff >> j) == 1, later)
        for h in range(HG_HEADS):
            sl = slice(h * LANE, (h + 1) * LANE)
            att[h] = att[h] + jnp.where(level, _dot_nt(qm[:, sl], km[:, sl]), 0.0)
        y = spread_half(y, sh, False) if sh % 8 == 0 else jnp.where(upper, y, from_next(y, sh))
    b_last = y
    qe = q * jnp.exp2(b).astype(BF)
    kd = k * jnp.exp2(b_last - b).astype(BF)
    carry = jnp.exp2(b_last[0:1, :])
    v = v_ref[...]
    vf = v.astype(F32)
    for h in range(HG_HEADS):
        sl = slice(h * LANE, (h + 1) * LANE)
        st = st_ref[h]
        inter = _dot_nt(qe[:, sl], st.astype(BF))
        diag = jnp.sum(q[:, sl].astype(F32) * k[:, sl].astype(F32), axis=-1, keepdims=True)
        o_ref[:, sl] = (inter + _dot(att[h].astype(BF), v[:, sl]) + diag * vf[:, sl]).astype(o_ref.dtype)
        st_ref[h] = st * carry[:, sl] + _dot_tn(v[:, sl], kd[:, sl])


def _scan_body(n_ctx, *refs):
    ins, outs, states = refs[:16], refs[16:20], refs[20:]
    s = pl.program_id(1)

    @pl.when(s == 0)
    def _():
        for st_ref in states:
            st_ref[...] = jnp.zeros_like(st_ref)

    def run(stream):
        for d in range(2):
            base = 8 * d + 4 * stream
            _scan_chunk(d == 1, *ins[base:base + 4], outs[2 * d + stream], states[d])

    @pl.when(s < n_ctx)
    def _():
        run(0)

    @pl.when(s >= n_ctx)
    def _():
        run(1)


def _scan(pc, pl_):
    qc, lfc, kc, vc = pc
    ql, lfl, kl, vl = pl_
    b, lc, w = qc.shape
    ll = ql.shape[1]
    c = SCAN_CHUNK
    n_ctx, n_lat = lc // c, ll // c

    def spec(idx, col):
        return pl.BlockSpec((None, c, w), lambda i, s: (i, idx(s), col))

    in_specs, out_specs, args = [], [], []
    for d in range(2):
        if d:
            ci = lambda s: jnp.maximum(n_ctx - 1 - s, 0)
            li = lambda s: n_lat - 1 - jnp.maximum(s - n_ctx, 0)
        else:
            ci = lambda s: jnp.minimum(s, n_ctx - 1)
            li = lambda s: jnp.maximum(s - n_ctx, 0)
        in_specs += [spec(ci, 0), spec(ci, d), spec(ci, d), spec(ci, 0),
                     spec(li, 0), spec(li, d), spec(li, d), spec(li, 0)]
        args += [qc, kc, lfc, vc, ql, kl, lfl, vl]
        out_specs += [spec(ci, 0), spec(li, 0)]
    shp = lambda l: jax.ShapeDtypeStruct((b, l, w), BF)
    of_c, of_l, ob_c, ob_l = pl.pallas_call(
        functools.partial(_scan_body, n_ctx),
        grid=(b, n_ctx + n_lat),
        in_specs=in_specs,
        out_specs=out_specs,
        out_shape=[shp(lc), shp(ll), shp(lc), shp(ll)],
        scratch_shapes=[pltpu.VMEM((HG_HEADS, HEAD_DIM, HEAD_DIM), F32) for _ in range(2)],
        compiler_params=_cparams("parallel", "arbitrary"),
        name="hgrn_scan",
    )(*args)
    return (of_c, of_l), (ob_c, ob_l)


def _out_even_body(yf_ref, of_ref, ob_ref, g_ref, x_ref, mod_ref, gn_ref, w_ref, o_ref):
    o = of_ref[...].astype(F32) + ob_ref[...].astype(F32)
    g = g_ref[...].astype(F32)
    gn = gn_ref[...]
    parts = []
    for h in range(HG_HEADS):
        sl = slice(h * LANE, (h + 1) * LANE)
        parts.append((_rms(o[:, sl]) * gn[:, sl] * _silu(g[:, sl])).astype(BF))
    gated = jnp.concatenate(parts, axis=1)
    w = HG_W
    y = _dot(yf_ref[...], w_ref[0:w, :]) + _dot(gated, w_ref[w:2 * w, :])
    o_ref[...] = x_ref[...] + mod_ref[2:3, :] * y


def _out_even(yf, o_f, o_b, g, x, mod, gn, w_out):
    b, l, d = x.shape
    tm = _row_tile(l, 512)
    row = lambda width: pl.BlockSpec((None, tm, width), lambda i, t: (i, t, 0))
    return pl.pallas_call(
        _out_even_body,
        grid=(b, l // tm),
        in_specs=[row(HG_W), row(HG_W), row(HG_W), row(HG_W), row(d),
                  pl.BlockSpec((None, 6, d), lambda i, t: (i, 0, 0)),
                  pl.BlockSpec(gn.shape, lambda i, t: (0, 0)),
                  pl.BlockSpec(w_out.shape, lambda i, t: (0, 0))],
        out_specs=row(d),
        out_shape=jax.ShapeDtypeStruct((b, l, d), F32),
        compiler_params=_cparams("parallel", "parallel"),
        name="out_even",
    )(yf, o_f, o_b, g, x, mod, gn, w_out)


def _proj_att_body(x_ref, mod_ref, w_ref, qg_ref, kg_ref, cos_ref, sin_ref, q_ref, k_ref, v_ref):
    mod = mod_ref[...]
    h = _modulate(x_ref[...], mod[0:1], mod[1:2])
    cos = cos_ref[...]
    sin = sin_ref[...]
    scale = HEAD_DIM ** -0.5 * math.log2(math.e)
    for i in range(ATT_HEADS + ATT_KV_HEADS):
        sl = slice(i * LANE, (i + 1) * LANE)
        p = _dot(h, w_ref[:, sl])
        gain = qg_ref[...] if i < ATT_HEADS else kg_ref[...]
        p = _rms(p) * gain
        p = p * cos + pltpu.roll(p, HEAD_DIM // 2, axis=1) * sin
        if i < ATT_HEADS:
            q_ref[:, sl] = (p * scale).astype(BF)
        else:
            k_ref[:, (i - ATT_HEADS) * LANE:(i - ATT_HEADS + 1) * LANE] = p.astype(BF)
    nqk = (ATT_HEADS + ATT_KV_HEADS) * LANE
    v_ref[...] = _dot(h, w_ref[:, nqk:nqk + ATT_KV_HEADS * LANE]).astype(BF)


def _proj_att(x, mod, w_qkv, qg, kg, cos, sin):
    b, l, d = x.shape
    tm = _row_tile(l, 512)
    qw, kw = ATT_HEADS * LANE, ATT_KV_HEADS * LANE
    row = lambda width: pl.BlockSpec((None, tm, width), lambda i, t: (i, t, 0))
    const = lambda a: pl.BlockSpec(a.shape, lambda i, t: (0, 0))
    tab = pl.BlockSpec((tm, LANE), lambda i, t: (t, 0))
    return pl.pallas_call(
        _proj_att_body,
        grid=(b, l // tm),
        in_specs=[row(d), pl.BlockSpec((None, 6, d), lambda i, t: (i, 0, 0)),
                  const(w_qkv), const(qg), const(kg), tab, tab],
        out_specs=[row(qw), row(kw), row(kw)],
        out_shape=[jax.ShapeDtypeStruct((b, l, qw), BF), jax.ShapeDtypeStruct((b, l, kw), BF),
                   jax.ShapeDtypeStruct((b, l, kw), BF)],
        compiler_params=_cparams("parallel", "parallel"),
        name="proj_att",
    )(x, mod, w_qkv, qg, kg, cos, sin)


@functools.lru_cache(maxsize=None)
def _rope_tables(l):
    t = np.arange(l)
    row = (t // GRID_W).astype(np.float64)
    col = (t % GRID_W).astype(np.float64)
    n_freq = HEAD_DIM // 4
    freqs = ROPE_THETA ** (-np.arange(n_freq, dtype=np.float64) / n_freq)
    ang = np.concatenate([row[:, None] * freqs, col[:, None] * freqs], axis=-1)
    cos, sin = np.cos(ang), np.sin(ang)
    return (np.concatenate([cos, cos], axis=-1).astype(np.float32),
            np.concatenate([-sin, sin], axis=-1).astype(np.float32))


ATT_KV_BLOCK = 256


def _attn_body(n_src, q_ref, *refs):
    o_ref = refs[2 * n_src]
    tq = q_ref.shape[0]
    n = ATT_GROUP * tq
    vt_rows = refs[1].shape[0]
    q = q_ref[...]
    qs = jnp.concatenate([q[:, g * LANE:(g + 1) * LANE] for g in range(ATT_GROUP)], axis=0)
    blocks = []
    for i in range(n_src):
        k_ref, vt_ref = refs[2 * i], refs[2 * i + 1]
        lk = k_ref.shape[0]
        bk = min(ATT_KV_BLOCK, lk)
        blocks += [(k_ref, vt_ref, j * bk, bk) for j in range(lk // bk)]

    def scores(blk):
        k_ref, _, off, bk = blk
        return _dot_nt(k_ref[off:off + bk, :], qs)

    m = jnp.full((1, n), -1e30, F32)
    acc = jnp.zeros((vt_rows, n), F32)
    def accumulate(acc, blk, alpha, p):
        _, vt_ref, off, bk = blk
        return alpha * acc + _dot(vt_ref[:, off:off + bk], p)

    nb = len(blocks)
    ahead = [scores(blocks[j]) for j in range(min(2, nb))]
    pending = None
    for j in range(nb):
        s = ahead.pop(0)
        if j + 2 < nb:
            ahead.append(scores(blocks[j + 2]))
        m_new = jnp.maximum(m, jnp.max(s, axis=0, keepdims=True))
        alpha = jnp.exp2(m - m_new)
        p = jnp.exp2((s - m_new).astype(BF))
        m = m_new
        if pending is not None:
            acc = accumulate(acc, *pending)
        pending = (blocks[j], alpha, p)
    acc = accumulate(acc, *pending)

    o = acc[:HEAD_DIM] * (1.0 / acc[HEAD_DIM:HEAD_DIM + 1])
    for g in range(ATT_GROUP):
        o_ref[:, g * LANE:(g + 1) * LANE] = o[:, g * tq:(g + 1) * tq].T.astype(BF)


def _values_t(v):
    b, l, _ = v.shape
    vt = v.reshape(b, l, ATT_KV_HEADS, HEAD_DIM).transpose(0, 2, 3, 1)
    return jnp.concatenate([vt, jnp.ones((b, ATT_KV_HEADS, 16, l), v.dtype)], axis=2)


def _attention(q, kvs, tq):
    b, lq, qw = q.shape
    gw = ATT_GROUP * LANE
    in_specs = [pl.BlockSpec((None, tq, gw), lambda i, h, t: (i, t, h))]
    args = [q]
    for k, vt in kvs:
        lk = k.shape[1]
        in_specs.append(pl.BlockSpec((None, lk, LANE), lambda i, h, t: (i, 0, h)))
        in_specs.append(pl.BlockSpec((None, None, vt.shape[2], lk), lambda i, h, t: (i, h, 0, 0)))
        args += [k, vt]
    return pl.pallas_call(
        functools.partial(_attn_body, len(kvs)),
        grid=(b, ATT_KV_HEADS, lq // tq),
        in_specs=in_specs,
        out_specs=pl.BlockSpec((None, tq, gw), lambda i, h, t: (i, t, h)),
        out_shape=jax.ShapeDtypeStruct((b, lq, qw), BF),
        compiler_params=_cparams("parallel", "parallel", "parallel"),
        name="attention",
    )(*args)


def _out_att_body(o_ref, x_ref, mod_ref, w_ref, y_ref):
    y_ref[...] = x_ref[...] + mod_ref[2:3, :] * _dot(o_ref[...], w_ref[...])


def _out_att(o, x, mod, w_out):
    b, l, d = x.shape
    tm = _row_tile(l, 512)
    row = lambda width: pl.BlockSpec((None, tm, width), lambda i, t: (i, t, 0))
    return pl.pallas_call(
        _out_att_body,
        grid=(b, l // tm),
        in_specs=[row(o.shape[-1]), row(d), pl.BlockSpec((None, 6, d), lambda i, t: (i, 0, 0)),
                  pl.BlockSpec(w_out.shape, lambda i, t: (0, 0))],
        out_specs=row(d),
        out_shape=jax.ShapeDtypeStruct((b, l, d), F32),
        compiler_params=_cparams("parallel", "parallel"),
        name="out_att",
    )(o, x, mod, w_out)


def _ffn_body(on_grid, n_chunk, has_final, *refs):
    if on_grid:
        xp_ref, xm_ref, xn_ref = refs[:3]
        refs = refs[3:]
    else:
        xm_ref = refs[0]
        refs = refs[1:]
    mod_ref, wg_ref, wv_ref, cw_ref, cb_ref, wd_ref = refs[:6]
    fg_ref = refs[6] if has_final else None
    o_ref, h_buf = refs[-2:]
    tm = xm_ref.shape[0]
    ck = wg_ref.shape[-1]
    halo = GRID_W if on_grid else 0
    rows = tm + 2 * halo
    mod = mod_ref[...]
    md = lambda x: _modulate(x, mod[3:4], mod[4:5])
    if on_grid:
        h_buf[0:halo, :] = md(xp_ref[...])
        h_buf[halo:halo + tm, :] = md(xm_ref[...])
        h_buf[halo + tm:rows, :] = md(xn_ref[...])
    else:
        h_buf[...] = md(xm_ref[...])
    r = lax.broadcasted_iota(jnp.int32, (rows, ck), 0)
    if on_grid:
        t = pl.program_id(1)
        nt = pl.num_programs(1)
        col = r & (GRID_W - 1)
        has_left = col != 0
        has_right = col != GRID_W - 1
        inside = jnp.logical_and(jnp.logical_or(r >= halo, t > 0),
                                 jnp.logical_or(r < halo + tm, t < nt - 1))
    else:
        has_left = r != 0
        has_right = r != rows - 1

    hm_sl = slice(halo, halo + tm)

    def up(c):
        gate = _dot(h_buf[...], wg_ref[c])
        val = _dot(h_buf[hm_sl, :], wv_ref[c])
        return gate, val

    def down(c, gate, val):
        if on_grid:
            gate = jnp.where(inside, gate, 0.0)
        gl = jnp.where(has_left, pltpu.roll(gate, 1, axis=0), 0.0)
        gr = jnp.where(has_right, pltpu.roll(gate, rows - 1, axis=0), 0.0)
        cw = cw_ref[c]
        if on_grid:
            conv = None
            for dy in range(3):
                sl = slice(dy * halo, dy * halo + tm)
                term = (cw[3 * dy:3 * dy + 1] * gl[sl] + cw[3 * dy + 1:3 * dy + 2] * gate[sl]
                        + cw[3 * dy + 2:3 * dy + 3] * gr[sl])
                conv = term if conv is None else conv + term
        else:
            conv = cw[3:4] * gl + cw[4:5] * gate + cw[5:6] * gr
        act = (_silu(conv + cb_ref[c]) * val).astype(BF)
        return _dot(act, wd_ref[c])

    nxt = up(0)
    acc = None
    for c in range(n_chunk):
        cur = nxt
        if c + 1 < n_chunk:
            nxt = up(c + 1)
        part = down(c, *cur)
        acc = part if acc is None else acc + part
    y = xm_ref[...] + mod[5:6] * acc
    if has_final:
        y = _rms(y) * fg_ref[...]
    o_ref[...] = y


def _ffn(x, mod, wg, wv, cw, cb, wd, on_grid, final_g=None):
    b, l, d = x.shape
    n_chunk, _, ck = wg.shape
    extra = [] if final_g is None else [final_g.reshape(1, d)]
    const = lambda a: pl.BlockSpec(a.shape, lambda i, t: (0,) * a.ndim, pipeline_mode=pl.Buffered(1))
    modspec = pl.BlockSpec((None, 6, d), lambda i, t: (i, 0, 0))
    if on_grid:
        tm = FFN_TILE if l % FFN_TILE == 0 else 512
        per = tm // GRID_W
        n_rows = l // GRID_W
        x_specs = [pl.BlockSpec((None, GRID_W, d), lambda i, t: (i, jnp.maximum(t * per - 1, 0), 0)),
                   pl.BlockSpec((None, tm, d), lambda i, t: (i, t, 0)),
                   pl.BlockSpec((None, GRID_W, d), lambda i, t: (i, jnp.minimum((t + 1) * per, n_rows - 1), 0))]
        xs = [x, x, x]
        rows = tm + 2 * GRID_W
    else:
        tm = l
        x_specs = [pl.BlockSpec((None, tm, d), lambda i, t: (i, t, 0))]
        xs = [x]
        rows = tm
    return pl.pallas_call(
        functools.partial(_ffn_body, on_grid, n_chunk, final_g is not None),
        grid=(b, l // tm),
        in_specs=(x_specs + [modspec, const(wg), const(wv), const(cw), const(cb), const(wd)]
                  + [const(a) for a in extra]),
        out_specs=pl.BlockSpec((None, tm, d), lambda i, t: (i, t, 0)),
        out_shape=jax.ShapeDtypeStruct((b, l, d), F32),
        scratch_shapes=[pltpu.VMEM((rows, d), BF)],
        compiler_params=_cparams("parallel", "parallel"),
        name="ffn_grid" if on_grid else "ffn_seq",
    )(*xs, mod, wg, wv, cw, cb, wd, *extra)


def _ffn_chunk(d_ff):
    for ck in (256, 128):
        if d_ff % ck == 0:
            return ck
    return d_ff


def _even_mixer(x, ctx, ml, mc, w_in, w_out, lb_logits, gn, e, need_ctx):
    w_in = w_in.astype(BF)
    w_out = w_out.astype(BF)
    gn = gn.reshape(1, -1)
    lbl = lb_logits.reshape(lb_logits.shape[0], -1).astype(F32)
    a_l, q_l, lf_l, k_l, v_l, g_l = _proj_even(x, ml, w_in, lbl, e)
    a_c, q_c, lf_c, k_c, v_c, g_c = _proj_even(ctx, mc, w_in, lbl, e)
    pc, pl_ = (q_c, lf_c, k_c, v_c), (q_l, lf_l, k_l, v_l)
    (of_c, of_l), (ob_c, ob_l) = _scan(pc, pl_)
    x = _out_even(_dft_mix(a_l), of_l, ob_l, g_l, x, ml, gn, w_out)
    if need_ctx:
        ctx = _out_even(_dft_mix(a_c), of_c, ob_c, g_c, ctx, mc, gn, w_out)
    return x, ctx


def _att_mixer(x, ctx, ml, mc, w_qkv, qn_g, kn_g, w_out, need_ctx):
    l, lc = x.shape[1], ctx.shape[1]
    perm = np.concatenate([np.arange(0, HEAD_DIM, 2), np.arange(1, HEAD_DIM, 2)])
    n_qk = ATT_HEADS + ATT_KV_HEADS
    cols = np.concatenate([h * HEAD_DIM + perm for h in range(n_qk)]
                          + [np.arange(n_qk * HEAD_DIM, w_qkv.shape[-1])])
    w_in = w_qkv[:, cols].astype(BF)
    w_out = w_out.astype(BF)
    qg = qn_g[perm].reshape(1, -1)
    kg = kn_g[perm].reshape(1, -1)
    cos_l, sin_l = (jnp.asarray(t) for t in _rope_tables(l))
    cos_c, sin_c = jnp.ones((lc, HEAD_DIM), F32), jnp.zeros((lc, HEAD_DIM), F32)
    q_l, k_l, v_l = _proj_att(x, ml, w_in, qg, kg, cos_l, sin_l)
    q_c, k_c, v_c = _proj_att(ctx, mc, w_in, qg, kg, cos_c, sin_c)
    vt_l, vt_c = _values_t(v_l), _values_t(v_c)
    x = _out_att(_attention(q_l, [(k_l, vt_l), (k_c, vt_c)], min(l, 256)), x, ml, w_out)
    if need_ctx:
        ctx = _out_att(_attention(q_c, [(k_c, vt_c)], lc), ctx, mc, w_out)
    return x, ctx


def _ffn_layer(x, ctx, ml, mc, w_up, conv_w, conv_b, w_down, need_ctx, final_g=None):
    d = x.shape[-1]
    d_ff = w_down.shape[0]
    ck = _ffn_chunk(d_ff)
    n_chunk = d_ff // ck
    wu = w_up.astype(BF)
    wg = wu[:, :d_ff].reshape(d, n_chunk, ck).transpose(1, 0, 2)
    wv = wu[:, d_ff:].reshape(d, n_chunk, ck).transpose(1, 0, 2)
    cw = conv_w.reshape(9, n_chunk, ck).transpose(1, 0, 2)
    cb = conv_b.reshape(n_chunk, 1, ck)
    wd = w_down.astype(BF).reshape(n_chunk, ck, d)
    x = _ffn(x, ml, wg, wv, cw, cb, wd, True, final_g)
    if need_ctx:
        ctx = _ffn(ctx, mc, wg, wv, cw, cb, wd, False)
    return x, ctx


def _mod_vectors(c, c_ctx, w_mod, b_mod):
    b, d = c.shape
    depth = w_mod.shape[0]
    pad = (-(b + 1)) % 8
    cc = jnp.concatenate([c, c_ctx[None, :], jnp.zeros((pad, d), F32)], axis=0)
    mods = _modulation(cc, w_mod, b_mod)
    mod_lat = mods[:, :b].reshape(depth, b, 6, d)
    mod_ctx = jnp.broadcast_to(mods[:, b].reshape(depth, 1, 6, d), (depth, b, 6, d))
    return mod_lat, mod_ctx


def kernel(x, c, ctx, c_ctx, w_mod, b_mod, w_in_ab, w_out_ab, hg_lb_logits, hg_norm_g, w_qkv, q_norm_g,
           k_norm_g, w_out_att, w_up, conv_w, conv_b, w_down, final_norm_g):
    depth = w_mod.shape[0]
    mod_lat, mod_ctx = _mod_vectors(c, c_ctx, w_mod, b_mod)
    for layer in range(depth):
        need_ctx = layer != depth - 1
        ml, mc = mod_lat[layer], mod_ctx[layer]
        if layer % 2 == 0:
            e = layer // 2
            x, ctx = _even_mixer(x, ctx, ml, mc, w_in_ab[e], w_out_ab[e], hg_lb_logits, hg_norm_g[e], e, need_ctx)
        else:
            o = layer // 2
            x, ctx = _att_mixer(x, ctx, ml, mc, w_qkv[o], q_norm_g[o], k_norm_g[o], w_out_att[o], need_ctx)
        x, ctx = _ffn_layer(x, ctx, ml, mc, w_up[layer], conv_w[layer], conv_b[layer], w_down[layer], need_ctx,
                            None if need_ctx else final_norm_g)
    return x
```

```python
import functools
import math

import numpy as np
import jax
import jax.numpy as jnp
from jax import lax
from jax.experimental import pallas as pl
from jax.experimental.pallas import tpu as pltpu

F32 = jnp.float32
BF = jnp.bfloat16

EPS = 1e-6
GRID_W = 64
LANE = 128
HEAD_DIM = 128
FNET_GROUPS = 4
HG_HEADS = 4
HG_W = HG_HEADS * HEAD_DIM
ATT_HEADS = 8
ATT_KV_HEADS = 2
ATT_GROUP = ATT_HEADS // ATT_KV_HEADS
ROPE_THETA = 10000.0
SCAN_CHUNK = 128
FFN_TILE = 512
FFN_DOWN_GROUP = 3
VMEM_LIMIT = 56 * 1024 * 1024


def _cparams(*sem):
    return pltpu.CompilerParams(dimension_semantics=sem, vmem_limit_bytes=VMEM_LIMIT)


def _dot(a, b):
    return jnp.dot(a, b, preferred_element_type=F32)


def _dot_nt(a, b):
    return lax.dot_general(a, b, (((1,), (1,)), ((), ())), preferred_element_type=F32)


def _dot_tn(a, b):
    return lax.dot_general(a, b, (((0,), (0,)), ((), ())), preferred_element_type=F32)


def _rms(x):
    return x * lax.rsqrt(jnp.mean(x * x, axis=-1, keepdims=True) + EPS)


def _silu(x):
    return x * jax.nn.sigmoid(x)


def _modulate(x, shift, scale):
    return (_rms(x) * (1.0 + scale) + shift).astype(BF)


def _row_tile(n, pref):
    return pref if n % pref == 0 else n


def _mod_body(c_ref, w_ref, b_ref, o_ref):
    s = _silu(c_ref[...]).astype(BF)
    o_ref[...] = _dot(s, w_ref[...].astype(BF)) + b_ref[...]


def _modulation(cc, w_mod, b_mod):
    depth, d, six_d = w_mod.shape
    r = cc.shape[0]
    return pl.pallas_call(
        _mod_body,
        grid=(depth, six_d // d),
        in_specs=[pl.BlockSpec((r, d), lambda l, j: (0, 0)),
                  pl.BlockSpec((None, d, d), lambda l, j: (l, 0, j)),
                  pl.BlockSpec((None, 1, d), lambda l, j: (l, 0, j))],
        out_specs=pl.BlockSpec((None, r, d), lambda l, j: (l, 0, j)),
        out_shape=jax.ShapeDtypeStruct((depth, r, six_d), F32),
        compiler_params=_cparams("parallel", "parallel"),
        name="modulation",
    )(cc, w_mod, b_mod.reshape(depth, 1, six_d))


def _proj_even_body(e, n_even, x_ref, mod_ref, w_ref, lbl_ref, a_ref, q_ref, lf_ref, k_ref, v_ref, g_ref):
    mod = mod_ref[...]
    h = _modulate(x_ref[...], mod[0:1], mod[1:2])
    rows = [lbl_ref[i:i + 1, :] for i in range(n_even)]
    mx = functools.reduce(jnp.maximum, rows)
    ex = [jnp.exp(r - mx) for r in rows]
    den = functools.reduce(lambda a, b: a + b, ex)
    lb = jnp.zeros_like(mx)
    for i in range(1, e + 1):
        lb = lb + ex[i] / den
    w = HG_W
    a_ref[...] = _dot(h, w_ref[:, 0:w]).astype(BF)
    q_ref[...] = _silu(_dot(h, w_ref[:, w:2 * w])).astype(BF)
    for d in range(2):
        z = _dot(h, w_ref[:, (2 + d) * w:(3 + d) * w])
        lbd = lb[:, d * w:(d + 1) * w]
        f = lbd + (1.0 - lbd) * jax.nn.sigmoid(z)
        lf_ref[:, d * w:(d + 1) * w] = jnp.log2(f)
        k_ref[:, d * w:(d + 1) * w] = (1.0 - f).astype(BF)
    v_ref[...] = _dot(h, w_ref[:, 4 * w:5 * w]).astype(BF)
    g_ref[...] = _dot(h, w_ref[:, 5 * w:6 * w]).astype(BF)


def _proj_even(x, mod, w_in, lb_logits, e):
    b, l, d = x.shape
    n_even = lb_logits.shape[0]
    tm = _row_tile(l, 512)
    w = HG_W
    row = lambda width: pl.BlockSpec((None, tm, width), lambda i, t: (i, t, 0))
    shp = lambda width, dt: jax.ShapeDtypeStruct((b, l, width), dt)
    return pl.pallas_call(
        functools.partial(_proj_even_body, e, n_even),
        grid=(b, l // tm),
        in_specs=[row(d),
                  pl.BlockSpec((None, 6, d), lambda i, t: (i, 0, 0)),
                  pl.BlockSpec(w_in.shape, lambda i, t: (0, 0)),
                  pl.BlockSpec(lb_logits.shape, lambda i, t: (0, 0))],
        out_specs=[row(w), row(w), row(2 * w), row(2 * w), row(w), row(w)],
        out_shape=[shp(w, BF), shp(w, BF), shp(2 * w, F32), shp(2 * w, BF), shp(w, BF), shp(w, BF)],
        compiler_params=_cparams("parallel", "parallel"),
        name="proj_even",
    )(x, mod, w_in, lb_logits)


@functools.lru_cache(maxsize=None)
def _dft_tables(n):
    idx = np.arange(n, dtype=np.int64)
    ang = 2.0 * np.pi * ((idx[:, None] * idx[None, :]) % n).astype(np.float64) / n
    s = 1.0 / math.sqrt(n)
    return (np.cos(ang) * s).astype(np.float32), (np.sin(ang) * s).astype(np.float32)


def _dft_body(c_ref, s_ref, a_ref, cs_ref, o_ref):
    a = a_ref[...]
    p = _dot(c_ref[...], a).astype(BF)
    q = _dot(s_ref[...], a).astype(BF)
    cs = cs_ref[...]
    for g in range(FNET_GROUPS):
        sl = slice(g * LANE, (g + 1) * LANE)
        pq = jnp.concatenate([p[:, sl], q[:, sl]], axis=1)
        o_ref[:, sl] = _dot(pq, cs).astype(BF)


def _dft_mix(a):
    b, l, w = a.shape
    cl, sl_ = _dft_tables(l)
    cc, sc = _dft_tables(LANE)
    cos_l = jnp.asarray(cl).astype(BF)
    sin_l = jnp.asarray(sl_).astype(BF)
    cs = jnp.concatenate([jnp.asarray(cc), -jnp.asarray(sc)], axis=0).astype(BF)
    tm = _row_tile(l, 512)
    return pl.pallas_call(
        _dft_body,
        grid=(l // tm, b),
        in_specs=[pl.BlockSpec((tm, l), lambda m, i: (m, 0)),
                  pl.BlockSpec((tm, l), lambda m, i: (m, 0)),
                  pl.BlockSpec((None, l, w), lambda m, i: (i, 0, 0)),
                  pl.BlockSpec(cs.shape, lambda m, i: (0, 0))],
        out_specs=pl.BlockSpec((None, tm, w), lambda m, i: (i, m, 0)),
        out_shape=jax.ShapeDtypeStruct((b, l, w), BF),
        compiler_params=_cparams("parallel", "parallel"),
        name="dft_mix",
    )(cos_l, sin_l, a, cs)


def _scan_chunk(rev, q_ref, k_ref, lf_ref, v_ref, o_ref, st_ref):
    c, w = lf_ref.shape
    nlev = c.bit_length() - 1
    lf = lf_ref[...]
    r = lax.broadcasted_iota(jnp.int32, (c, w), 0)
    pos = (c - 1 - r) if rev else r

    def from_prev(x, sh):
        return pltpu.roll(x, (c - sh) if rev else sh, axis=0)

    def from_next(x, sh):
        return pltpu.roll(x, sh if rev else (c - sh), axis=0)

    def neg_abs(x):
        return pltpu.bitcast(pltpu.bitcast(x, jnp.uint32) | jnp.uint32(0x80000000), F32)

    def spread_half(x, sh, low):
        parts = []
        for i in range(c // (2 * sh)):
            first, second = x[2 * i * sh:(2 * i + 1) * sh], x[(2 * i + 1) * sh:(2 * i + 2) * sh]
            src = (second if rev else first) if low else (first if rev else second)
            parts += [src, src]
        return jnp.concatenate(parts, axis=0)

    b = lf
    for j in range(nlev):
        sh = 1 << j
        if sh % 8 == 0:
            zero = jnp.zeros((sh, w), F32)
            b = b + (jnp.concatenate([b[sh:], zero], axis=0) if rev else jnp.concatenate([zero, b[:c - sh]], axis=0))
        else:
            b = b + jnp.where(pos >= sh, from_prev(b, sh), 0.0)

    q = q_ref[...]
    k = k_ref[...]
    ti = lax.broadcasted_iota(jnp.int32, (c, c), 0)
    si = lax.broadcasted_iota(jnp.int32, (c, c), 1)
    later = (ti < si) if rev else (ti > si)
    diff = ti ^ si
    att = [jnp.zeros((c, c), F32) for _ in range(HG_HEADS)]
    y = b
    for j in range(nlev):
        sh = 1 << j
        upper = ((pos >> j) & 1) == 1
        bound = spread_half(y, sh, True) if sh % 8 == 0 else jnp.where(upper, from_prev(y, sh), y)
        e = jnp.exp2(neg_abs(b - bound)).astype(BF)
        qm = q * e
        km = k * e
        level = jnp.logical_and((diff >> j) == 1, later)
        for h in range(HG_HEADS):
            sl = slice(h * LANE, (h + 1) * LANE)
            att[h] = att[h] + jnp.where(level, _dot_nt(qm[:, sl], km[:, sl]), 0.0)
        y = spread_half(y, sh, False) if sh % 8 == 0 else jnp.where(upper, y, from_next(y, sh))
    b_last = y
    qe = q * jnp.exp2(b).astype(BF)
    kd = k * jnp.exp2(b_last - b).astype(BF)
    carry = jnp.exp2(b_last[0:1, :])
    v = v_ref[...]
    vf = v.astype(F32)
    for h in range(HG_HEADS):
        sl = slice(h * LANE, (h + 1) * LANE)
        st = st_ref[h]
        inter = _dot_nt(qe[:, sl], st.astype(BF))
        diag = jnp.sum(q[:, sl].astype(F32) * k[:, sl].astype(F32), axis=-1, keepdims=True)
        o_ref[:, sl] = (inter + _dot(att[h].astype(BF), v[:, sl]) + diag * vf[:, sl]).astype(o_ref.dtype)
        st_ref[h] = st * carry[:, sl] + _dot_tn(v[:, sl], kd[:, sl])


def _scan_body(n_ctx, *refs):
    ins, outs, states = refs[:16], refs[16:20], refs[20:]
    s = pl.program_id(1)

    @pl.when(s == 0)
    def _():
        for st_ref in states:
            st_ref[...] = jnp.zeros_like(st_ref)

    def run(stream):
        for d in range(2):
            base = 8 * d + 4 * stream
            _scan_chunk(d == 1, *ins[base:base + 4], outs[2 * d + stream], states[d])

    @pl.when(s < n_ctx)
    def _():
        run(0)

    @pl.when(s >= n_ctx)
    def _():
        run(1)


def _scan(pc, pl_):
    qc, lfc, kc, vc = pc
    ql, lfl, kl, vl = pl_
    b, lc, w = qc.shape
    ll = ql.shape[1]
    c = SCAN_CHUNK
    n_ctx, n_lat = lc // c, ll // c

    def spec(idx, col):
        return pl.BlockSpec((None, c, w), lambda i, s: (i, idx(s), col))

    in_specs, out_specs, args = [], [], []
    for d in range(2):
        if d:
            ci = lambda s: jnp.maximum(n_ctx - 1 - s, 0)
            li = lambda s: n_lat - 1 - jnp.maximum(s - n_ctx, 0)
        else:
            ci = lambda s: jnp.minimum(s, n_ctx - 1)
            li = lambda s: jnp.maximum(s - n_ctx, 0)
        in_specs += [spec(ci, 0), spec(ci, d), spec(ci, d), spec(ci, 0),
                     spec(li, 0), spec(li, d), spec(li, d), spec(li, 0)]
        args += [qc, kc, lfc, vc, ql, kl, lfl, vl]
        out_specs += [spec(ci, 0), spec(li, 0)]
    shp = lambda l: jax.ShapeDtypeStruct((b, l, w), BF)
    of_c, of_l, ob_c, ob_l = pl.pallas_call(
        functools.partial(_scan_body, n_ctx),
        grid=(b, n_ctx + n_lat),
        in_specs=in_specs,
        out_specs=out_specs,
        out_shape=[shp(lc), shp(ll), shp(lc), shp(ll)],
        scratch_shapes=[pltpu.VMEM((HG_HEADS, HEAD_DIM, HEAD_DIM), F32) for _ in range(2)],
        compiler_params=_cparams("parallel", "arbitrary"),
        name="hgrn_scan",
    )(*args)
    return (of_c, of_l), (ob_c, ob_l)


def _out_even_body(yf_ref, of_ref, ob_ref, g_ref, x_ref, mod_ref, gn_ref, w_ref, o_ref):
    o = of_ref[...].astype(F32) + ob_ref[...].astype(F32)
    g = g_ref[...].astype(F32)
    gn = gn_ref[...]
    parts = []
    for h in range(HG_HEADS):
        sl = slice(h * LANE, (h + 1) * LANE)
        parts.append((_rms(o[:, sl]) * gn[:, sl] * _silu(g[:, sl])).astype(BF))
    gated = jnp.concatenate(parts, axis=1)
    w = HG_W
    y = _dot(yf_ref[...], w_ref[0:w, :]) + _dot(gated, w_ref[w:2 * w, :])
    o_ref[...] = x_ref[...] + mod_ref[2:3, :] * y


def _out_even(yf, o_f, o_b, g, x, mod, gn, w_out):
    b, l, d = x.shape
    tm = _row_tile(l, 512)
    row = lambda width: pl.BlockSpec((None, tm, width), lambda i, t: (i, t, 0))
    return pl.pallas_call(
        _out_even_body,
        grid=(b, l // tm),
        in_specs=[row(HG_W), row(HG_W), row(HG_W), row(HG_W), row(d),
                  pl.BlockSpec((None, 6, d), lambda i, t: (i, 0, 0)),
                  pl.BlockSpec(gn.shape, lambda i, t: (0, 0)),
                  pl.BlockSpec(w_out.shape, lambda i, t: (0, 0))],
        out_specs=row(d),
        out_shape=jax.ShapeDtypeStruct((b, l, d), F32),
        compiler_params=_cparams("parallel", "parallel"),
        name="out_even",
    )(yf, o_f, o_b, g, x, mod, gn, w_out)


def _proj_att_body(x_ref, mod_ref, w_ref, wvt_ref, qg_ref, kg_ref, cos_ref, sin_ref, q_ref, k_ref, vt_ref):
    mod = mod_ref[...]
    h = _modulate(x_ref[...], mod[0:1], mod[1:2])
    cos = cos_ref[...]
    sin = sin_ref[...]
    scale = HEAD_DIM ** -0.5 * math.log2(math.e)
    for i in range(ATT_HEADS + ATT_KV_HEADS):
        sl = slice(i * LANE, (i + 1) * LANE)
        p = _dot(h, w_ref[:, sl])
        gain = qg_ref[...] if i < ATT_HEADS else kg_ref[...]
        p = _rms(p) * gain
        p = p * cos + pltpu.roll(p, HEAD_DIM // 2, axis=1) * sin
        if i < ATT_HEADS:
            q_ref[:, sl] = (p * scale).astype(BF)
        else:
            k_ref[:, (i - ATT_HEADS) * LANE:(i - ATT_HEADS + 1) * LANE] = p.astype(BF)
    vt = _dot_nt(wvt_ref[...], h)
    for hk in range(ATT_KV_HEADS):
        vt_ref[hk, 0:HEAD_DIM, :] = vt[hk * LANE:(hk + 1) * LANE].astype(BF)
        vt_ref[hk, HEAD_DIM:, :] = jnp.ones((ATT_ONES_ROWS, vt.shape[1]), BF)


def _proj_att(x, mod, w_qk, w_vt, qg, kg, cos, sin):
    b, l, d = x.shape
    tm = _row_tile(l, 512)
    qw, kw = ATT_HEADS * LANE, ATT_KV_HEADS * LANE
    vt_rows = HEAD_DIM + ATT_ONES_ROWS
    row = lambda width: pl.BlockSpec((None, tm, width), lambda i, t: (i, t, 0))
    const = lambda a: pl.BlockSpec(a.shape, lambda i, t: (0, 0))
    tab = pl.BlockSpec((tm, LANE), lambda i, t: (t, 0))
    return pl.pallas_call(
        _proj_att_body,
        grid=(b, l // tm),
        in_specs=[row(d), pl.BlockSpec((None, 6, d), lambda i, t: (i, 0, 0)),
                  const(w_qk), const(w_vt), const(qg), const(kg), tab, tab],
        out_specs=[row(qw), row(kw),
                   pl.BlockSpec((None, ATT_KV_HEADS, vt_rows, tm), lambda i, t: (i, 0, 0, t))],
        out_shape=[jax.ShapeDtypeStruct((b, l, qw), BF), jax.ShapeDtypeStruct((b, l, kw), BF),
                   jax.ShapeDtypeStruct((b, ATT_KV_HEADS, vt_rows, l), BF)],
        compiler_params=_cparams("parallel", "parallel"),
        name="proj_att",
    )(x, mod, w_qk, w_vt, qg, kg, cos, sin)


@functools.lru_cache(maxsize=None)
def _rope_tables(l):
    t = np.arange(l)
    row = (t // GRID_W).astype(np.float64)
    col = (t % GRID_W).astype(np.float64)
    n_freq = HEAD_DIM // 4
    freqs = ROPE_THETA ** (-np.arange(n_freq, dtype=np.float64) / n_freq)
    ang = np.concatenate([row[:, None] * freqs, col[:, None] * freqs], axis=-1)
    cos, sin = np.cos(ang), np.sin(ang)
    return (np.concatenate([cos, cos], axis=-1).astype(np.float32),
            np.concatenate([-sin, sin], axis=-1).astype(np.float32))


ATT_KV_BLOCK = 256
ATT_ONES_ROWS = 16


def _attn_body(n_src, q_ref, *refs):
    o_ref = refs[2 * n_src]
    tq = q_ref.shape[0]
    n = ATT_GROUP * tq
    vt_rows = refs[1].shape[0]
    q = q_ref[...]
    qs = jnp.concatenate([q[:, g * LANE:(g + 1) * LANE] for g in range(ATT_GROUP)], axis=0)
    blocks = []
    for i in range(n_src):
        k_ref, vt_ref = refs[2 * i], refs[2 * i + 1]
        lk = k_ref.shape[0]
        bk = min(ATT_KV_BLOCK, lk)
        blocks += [(k_ref, vt_ref, j * bk, bk) for j in range(lk // bk)]

    def scores(blk):
        k_ref, _, off, bk = blk
        return _dot_nt(k_ref[off:off + bk, :], qs)

    m = jnp.full((1, n), -1e30, F32)
    acc = jnp.zeros((vt_rows, n), F32)
    def accumulate(acc, blk, alpha, p):
        _, vt_ref, off, bk = blk
        return alpha * acc + _dot(vt_ref[:, off:off + bk], p)

    nb = len(blocks)
    ahead = [scores(blocks[j]) for j in range(min(2, nb))]
    pending = None
    for j in range(nb):
        s = ahead.pop(0)
        if j + 2 < nb:
            ahead.append(scores(blocks[j + 2]))
        m_new = jnp.maximum(m, jnp.max(s, axis=0, keepdims=True))
        alpha = jnp.exp2(m - m_new)
        p = jnp.exp2((s - m_new).astype(BF))
        m = m_new
        if pending is not None:
            acc = accumulate(acc, *pending)
        pending = (blocks[j], alpha, p)
    acc = accumulate(acc, *pending)

    o = acc[:HEAD_DIM] * (1.0 / acc[HEAD_DIM:HEAD_DIM + 1])
    for g in range(ATT_GROUP):
        o_ref[:, g * LANE:(g + 1) * LANE] = o[:, g * tq:(g + 1) * tq].T.astype(BF)


def _attention(q, kvs, tq):
    b, lq, qw = q.shape
    gw = ATT_GROUP * LANE
    in_specs = [pl.BlockSpec((None, tq, gw), lambda i, h, t: (i, t, h))]
    args = [q]
    for k, vt in kvs:
        lk = k.shape[1]
        in_specs.append(pl.BlockSpec((None, lk, LANE), lambda i, h, t: (i, 0, h)))
        in_specs.append(pl.BlockSpec((None, None, vt.shape[2], lk), lambda i, h, t: (i, h, 0, 0)))
        args += [k, vt]
    return pl.pallas_call(
        functools.partial(_attn_body, len(kvs)),
        grid=(b, ATT_KV_HEADS, lq // tq),
        in_specs=in_specs,
        out_specs=pl.BlockSpec((None, tq, gw), lambda i, h, t: (i, t, h)),
        out_shape=jax.ShapeDtypeStruct((b, lq, qw), BF),
        compiler_params=_cparams("parallel", "parallel", "parallel"),
        name="attention",
    )(*args)


def _out_att_body(o_ref, x_ref, mod_ref, w_ref, y_ref):
    y_ref[...] = x_ref[...] + mod_ref[2:3, :] * _dot(o_ref[...], w_ref[...])


def _out_att(o, x, mod, w_out):
    b, l, d = x.shape
    tm = _row_tile(l, 512)
    row = lambda width: pl.BlockSpec((None, tm, width), lambda i, t: (i, t, 0))
    return pl.pallas_call(
        _out_att_body,
        grid=(b, l // tm),
        in_specs=[row(o.shape[-1]), row(d), pl.BlockSpec((None, 6, d), lambda i, t: (i, 0, 0)),
                  pl.BlockSpec(w_out.shape, lambda i, t: (0, 0))],
        out_specs=row(d),
        out_shape=jax.ShapeDtypeStruct((b, l, d), F32),
        compiler_params=_cparams("parallel", "parallel"),
        name="out_att",
    )(o, x, mod, w_out)


def _ffn_body(on_grid, n_chunk, has_final, *refs):
    if on_grid:
        xp_ref, xm_ref, xn_ref = refs[:3]
        refs = refs[3:]
    else:
        xm_ref = refs[0]
        refs = refs[1:]
    mod_ref, wg_ref, wv_ref, cw_ref, cb_ref, wd_ref = refs[:6]
    fg_ref = refs[6] if has_final else None
    o_ref, h_buf = refs[-2:]
    tm = xm_ref.shape[0]
    ck = wg_ref.shape[-1]
    halo = GRID_W if on_grid else 0
    rows = tm + 2 * halo
    mod = mod_ref[...]
    md = lambda x: _modulate(x, mod[3:4], mod[4:5])
    if on_grid:
        h_buf[0:halo, :] = md(xp_ref[...])
        h_buf[halo:halo + tm, :] = md(xm_ref[...])
        h_buf[halo + tm:rows, :] = md(xn_ref[...])
    else:
        h_buf[...] = md(xm_ref[...])
    r = lax.broadcasted_iota(jnp.int32, (rows, ck), 0)
    if on_grid:
        t = pl.program_id(1)
        nt = pl.num_programs(1)
        col = r & (GRID_W - 1)
        has_left = col != 0
        has_right = col != GRID_W - 1
        inside = jnp.logical_and(jnp.logical_or(r >= halo, t > 0),
                                 jnp.logical_or(r < halo + tm, t < nt - 1))
    else:
        has_left = r != 0
        has_right = r != rows - 1

    hm_sl = slice(halo, halo + tm)

    def up(c):
        gate = _dot(h_buf[...], wg_ref[c])
        val = _dot(h_buf[hm_sl, :], wv_ref[c])
        return gate, val

    def down(c, gate, val):
        if on_grid:
            gate = jnp.where(inside, gate, 0.0)
        gl = jnp.where(has_left, pltpu.roll(gate, 1, axis=0), 0.0)
        gr = jnp.where(has_right, pltpu.roll(gate, rows - 1, axis=0), 0.0)
        cw = cw_ref[c]
        if on_grid:
            conv = None
            for dy in range(3):
                sl = slice(dy * halo, dy * halo + tm)
                term = (cw[3 * dy:3 * dy + 1] * gl[sl] + cw[3 * dy + 1:3 * dy + 2] * gate[sl]
                        + cw[3 * dy + 2:3 * dy + 3] * gr[sl])
                conv = term if conv is None else conv + term
        else:
            conv = cw[3:4] * gl + cw[4:5] * gate + cw[5:6] * gr
        return (_silu(conv + cb_ref[c]) * val).astype(BF)

    nxt = up(0)
    acc = None
    acts = []
    for c in range(n_chunk):
        cur = nxt
        if c + 1 < n_chunk:
            nxt = up(c + 1)
        acts.append(down(c, *cur))
        if len(acts) == FFN_DOWN_GROUP or c + 1 == n_chunk:
            c0 = c + 1 - len(acts)
            w = wd_ref[c0:c + 1].reshape(len(acts) * ck, wd_ref.shape[-1])
            part = _dot(jnp.concatenate(acts, axis=1), w)
            acc = part if acc is None else acc + part
            acts = []
    y = xm_ref[...] + mod[5:6] * acc
    if has_final:
        y = _rms(y) * fg_ref[...]
    o_ref[...] = y


def _ffn(x, mod, wg, wv, cw, cb, wd, on_grid, final_g=None):
    b, l, d = x.shape
    n_chunk, _, ck = wg.shape
    extra = [] if final_g is None else [final_g.reshape(1, d)]
    const = lambda a: pl.BlockSpec(a.shape, lambda i, t: (0,) * a.ndim, pipeline_mode=pl.Buffered(1))
    modspec = pl.BlockSpec((None, 6, d), lambda i, t: (i, 0, 0))
    if on_grid:
        tm = FFN_TILE if l % FFN_TILE == 0 else 512
        per = tm // GRID_W
        n_rows = l // GRID_W
        x_specs = [pl.BlockSpec((None, GRID_W, d), lambda i, t: (i, jnp.maximum(t * per - 1, 0), 0)),
                   pl.BlockSpec((None, tm, d), lambda i, t: (i, t, 0)),
                   pl.BlockSpec((None, GRID_W, d), lambda i, t: (i, jnp.minimum((t + 1) * per, n_rows - 1), 0))]
        xs = [x, x, x]
        rows = tm + 2 * GRID_W
    else:
        tm = l
        x_specs = [pl.BlockSpec((None, tm, d), lambda i, t: (i, t, 0))]
        xs = [x]
        rows = tm
    return pl.pallas_call(
        functools.partial(_ffn_body, on_grid, n_chunk, final_g is not None),
        grid=(b, l // tm),
        in_specs=(x_specs + [modspec, const(wg), const(wv), const(cw), const(cb), const(wd)]
                  + [const(a) for a in extra]),
        out_specs=pl.BlockSpec((None, tm, d), lambda i, t: (i, t, 0)),
        out_shape=jax.ShapeDtypeStruct((b, l, d), F32),
        scratch_shapes=[pltpu.VMEM((rows, d), BF)],
        compiler_params=_cparams("parallel", "parallel"),
        name="ffn_grid" if on_grid else "ffn_seq",
    )(*xs, mod, wg, wv, cw, cb, wd, *extra)


def _ffn_chunk(d_ff):
    for ck in (256, 128):
        if d_ff % ck == 0:
            return ck
    return d_ff


def _even_mixer(x, ctx, ml, mc, w_in, w_out, lb_logits, gn, e, need_ctx):
    w_in = w_in.astype(BF)
    w_out = w_out.astype(BF)
    gn = gn.reshape(1, -1)
    lbl = lb_logits.reshape(lb_logits.shape[0], -1).astype(F32)
    a_l, q_l, lf_l, k_l, v_l, g_l = _proj_even(x, ml, w_in, lbl, e)
    a_c, q_c, lf_c, k_c, v_c, g_c = _proj_even(ctx, mc, w_in, lbl, e)
    pc, pl_ = (q_c, lf_c, k_c, v_c), (q_l, lf_l, k_l, v_l)
    (of_c, of_l), (ob_c, ob_l) = _scan(pc, pl_)
    x = _out_even(_dft_mix(a_l), of_l, ob_l, g_l, x, ml, gn, w_out)
    if need_ctx:
        ctx = _out_even(_dft_mix(a_c), of_c, ob_c, g_c, ctx, mc, gn, w_out)
    return x, ctx


def _att_mixer(x, ctx, ml, mc, w_qkv, qn_g, kn_g, w_out, need_ctx):
    l, lc = x.shape[1], ctx.shape[1]
    perm = np.concatenate([np.arange(0, HEAD_DIM, 2), np.arange(1, HEAD_DIM, 2)])
    n_qk = ATT_HEADS + ATT_KV_HEADS
    cols = np.concatenate([h * HEAD_DIM + perm for h in range(n_qk)])
    w_qk = w_qkv[:, cols].astype(BF)
    w_vt = w_qkv[:, n_qk * HEAD_DIM:].T.astype(BF)
    w_out = w_out.astype(BF)
    qg = qn_g[perm].reshape(1, -1)
    kg = kn_g[perm].reshape(1, -1)
    cos_l, sin_l = (jnp.asarray(t) for t in _rope_tables(l))
    cos_c, sin_c = jnp.ones((lc, HEAD_DIM), F32), jnp.zeros((lc, HEAD_DIM), F32)
    q_l, k_l, vt_l = _proj_att(x, ml, w_qk, w_vt, qg, kg, cos_l, sin_l)
    q_c, k_c, vt_c = _proj_att(ctx, mc, w_qk, w_vt, qg, kg, cos_c, sin_c)
    x = _out_att(_attention(q_l, [(k_l, vt_l), (k_c, vt_c)], min(l, 256)), x, ml, w_out)
    if need_ctx:
        ctx = _out_att(_attention(q_c, [(k_c, vt_c)], lc), ctx, mc, w_out)
    return x, ctx


def _ffn_layer(x, ctx, ml, mc, w_up, conv_w, conv_b, w_down, need_ctx, final_g=None):
    d = x.shape[-1]
    d_ff = w_down.shape[0]
    ck = _ffn_chunk(d_ff)
    n_chunk = d_ff // ck
    wu = w_up.astype(BF)
    wg = wu[:, :d_ff].reshape(d, n_chunk, ck).transpose(1, 0, 2)
    wv = wu[:, d_ff:].reshape(d, n_chunk, ck).transpose(1, 0, 2)
    cw = conv_w.reshape(9, n_chunk, ck).transpose(1, 0, 2)
    cb = conv_b.reshape(n_chunk, 1, ck)
    wd = w_down.astype(BF).reshape(n_chunk, ck, d)
    x = _ffn(x, ml, wg, wv, cw, cb, wd, True, final_g)
    if need_ctx:
        ctx = _ffn(ctx, mc, wg, wv, cw, cb, wd, False)
    return x, ctx


def _mod_vectors(c, c_ctx, w_mod, b_mod):
    b, d = c.shape
    depth = w_mod.shape[0]
    pad = (-(b + 1)) % 8
    cc = jnp.concatenate([c, c_ctx[None, :], jnp.zeros((pad, d), F32)], axis=0)
    mods = _modulation(cc, w_mod, b_mod)
    mod_lat = mods[:, :b].reshape(depth, b, 6, d)
    mod_ctx = jnp.broadcast_to(mods[:, b].reshape(depth, 1, 6, d), (depth, b, 6, d))
    return mod_lat, mod_ctx


def kernel(x, c, ctx, c_ctx, w_mod, b_mod, w_in_ab, w_out_ab, hg_lb_logits, hg_norm_g, w_qkv, q_norm_g,
           k_norm_g, w_out_att, w_up, conv_w, conv_b, w_down, final_norm_g):
    depth = w_mod.shape[0]
    mod_lat, mod_ctx = _mod_vectors(c, c_ctx, w_mod, b_mod)
    for layer in range(depth):
        need_ctx = layer != depth - 1
        ml, mc = mod_lat[layer], mod_ctx[layer]
        if layer % 2 == 0:
            e = layer // 2
            x, ctx = _even_mixer(x, ctx, ml, mc, w_in_ab[e], w_out_ab[e], hg_lb_logits, hg_norm_g[e], e, need_ctx)
        else:
            o = layer // 2
            x, ctx = _att_mixer(x, ctx, ml, mc, w_qkv[o], q_norm_g[o], k_norm_g[o], w_out_att[o], need_ctx)
        x, ctx = _ffn_layer(x, ctx, ml, mc, w_up[layer], conv_w[layer], conv_b[layer], w_down[layer], need_ctx,
                            None if need_ctx else final_norm_g)
    return x
```

```python
import functools
import math

import numpy as np
import jax
import jax.numpy as jnp
from jax import lax
from jax.experimental import pallas as pl
from jax.experimental.pallas import tpu as pltpu

F32 = jnp.float32
BF = jnp.bfloat16

EPS = 1e-6
GRID_W = 64
LANE = 128
HEAD_DIM = 128
FNET_GROUPS = 4
HG_HEADS = 4
HG_W = HG_HEADS * HEAD_DIM
ATT_HEADS = 8
ATT_KV_HEADS = 2
ATT_GROUP = ATT_HEADS // ATT_KV_HEADS
ROPE_THETA = 10000.0
SCAN_CHUNK = 128
FFT_COLS_PER_STEP = 8
FFT_K1_PER_STEP = 8
FFN_TILE = 512
FFN_DOWN_GROUP = 3
VMEM_LIMIT = 56 * 1024 * 1024


def _cparams(*sem):
    return pltpu.CompilerParams(dimension_semantics=sem, vmem_limit_bytes=VMEM_LIMIT)


def _dot(a, b):
    return jnp.dot(a, b, preferred_element_type=F32)


def _dot_nt(a, b):
    return lax.dot_general(a, b, (((1,), (1,)), ((), ())), preferred_element_type=F32)


def _dot_tn(a, b):
    return lax.dot_general(a, b, (((0,), (0,)), ((), ())), preferred_element_type=F32)


def _rms(x):
    return x * lax.rsqrt(jnp.mean(x * x, axis=-1, keepdims=True) + EPS)


def _silu(x):
    return x * jax.nn.sigmoid(x)


def _modulate(x, shift, scale):
    return (_rms(x) * (1.0 + scale) + shift).astype(BF)


def _row_tile(n, pref):
    return pref if n % pref == 0 else n


def _mod_body(c_ref, w_ref, b_ref, o_ref):
    s = _silu(c_ref[...]).astype(BF)
    o_ref[...] = _dot(s, w_ref[...].astype(BF)) + b_ref[...]


def _modulation(cc, w_mod, b_mod):
    depth, d, six_d = w_mod.shape
    r = cc.shape[0]
    return pl.pallas_call(
        _mod_body,
        grid=(depth, six_d // d),
        in_specs=[pl.BlockSpec((r, d), lambda l, j: (0, 0)),
                  pl.BlockSpec((None, d, d), lambda l, j: (l, 0, j)),
                  pl.BlockSpec((None, 1, d), lambda l, j: (l, 0, j))],
        out_specs=pl.BlockSpec((None, r, d), lambda l, j: (l, 0, j)),
        out_shape=jax.ShapeDtypeStruct((depth, r, six_d), F32),
        compiler_params=_cparams("parallel", "parallel"),
        name="modulation",
    )(cc, w_mod, b_mod.reshape(depth, 1, six_d))


def _proj_even_body(e, n_even, x_ref, mod_ref, w_ref, lbl_ref, a_ref, q_ref, lf_ref, k_ref, v_ref, g_ref):
    mod = mod_ref[...]
    h = _modulate(x_ref[...], mod[0:1], mod[1:2])
    rows = [lbl_ref[i:i + 1, :] for i in range(n_even)]
    mx = functools.reduce(jnp.maximum, rows)
    ex = [jnp.exp(r - mx) for r in rows]
    den = functools.reduce(lambda a, b: a + b, ex)
    lb = jnp.zeros_like(mx)
    for i in range(1, e + 1):
        lb = lb + ex[i] / den
    w = HG_W
    a_ref[...] = _dot(h, w_ref[:, 0:w]).astype(BF)
    q_ref[...] = _silu(_dot(h, w_ref[:, w:2 * w])).astype(BF)
    for d in range(2):
        z = _dot(h, w_ref[:, (2 + d) * w:(3 + d) * w])
        lbd = lb[:, d * w:(d + 1) * w]
        f = lbd + (1.0 - lbd) * jax.nn.sigmoid(z)
        lf_ref[:, d * w:(d + 1) * w] = jnp.log2(f)
        k_ref[:, d * w:(d + 1) * w] = (1.0 - f).astype(BF)
    v_ref[...] = _dot(h, w_ref[:, 4 * w:5 * w]).astype(BF)
    g_ref[...] = _dot(h, w_ref[:, 5 * w:6 * w]).astype(BF)


def _proj_even(x, mod, w_in, lb_logits, e):
    b, l, d = x.shape
    n_even = lb_logits.shape[0]
    tm = _row_tile(l, 512)
    w = HG_W
    row = lambda width: pl.BlockSpec((None, tm, width), lambda i, t: (i, t, 0))
    shp = lambda width, dt: jax.ShapeDtypeStruct((b, l, width), dt)
    return pl.pallas_call(
        functools.partial(_proj_even_body, e, n_even),
        grid=(b, l // tm),
        in_specs=[row(d),
                  pl.BlockSpec((None, 6, d), lambda i, t: (i, 0, 0)),
                  pl.BlockSpec(w_in.shape, lambda i, t: (0, 0)),
                  pl.BlockSpec(lb_logits.shape, lambda i, t: (0, 0))],
        out_specs=[row(w), row(w), row(2 * w), row(2 * w), row(w), row(w)],
        out_shape=[shp(w, BF), shp(w, BF), shp(2 * w, F32), shp(2 * w, BF), shp(w, BF), shp(w, BF)],
        compiler_params=_cparams("parallel", "parallel"),
        name="proj_even",
    )(x, mod, w_in, lb_logits)


@functools.lru_cache(maxsize=None)
def _dft_tables(n):
    idx = np.arange(n, dtype=np.int64)
    ang = 2.0 * np.pi * ((idx[:, None] * idx[None, :]) % n).astype(np.float64) / n
    s = 1.0 / math.sqrt(n)
    return (np.cos(ang) * s).astype(np.float32), (np.sin(ang) * s).astype(np.float32)


def _dft_body(c_ref, s_ref, a_ref, cs_ref, o_ref):
    a = a_ref[...]
    p = _dot(c_ref[...], a).astype(BF)
    q = _dot(s_ref[...], a).astype(BF)
    cs = cs_ref[...]
    for g in range(FNET_GROUPS):
        sl = slice(g * LANE, (g + 1) * LANE)
        pq = jnp.concatenate([p[:, sl], q[:, sl]], axis=1)
        o_ref[:, sl] = _dot(pq, cs).astype(BF)


def _dft_mix(a):
    b, l, w = a.shape
    cl, sl_ = _dft_tables(l)
    cc, sc = _dft_tables(LANE)
    cos_l = jnp.asarray(cl).astype(BF)
    sin_l = jnp.asarray(sl_).astype(BF)
    cs = jnp.concatenate([jnp.asarray(cc), -jnp.asarray(sc)], axis=0).astype(BF)
    tm = _row_tile(l, 512)
    return pl.pallas_call(
        _dft_body,
        grid=(l // tm, b),
        in_specs=[pl.BlockSpec((tm, l), lambda m, i: (m, 0)),
                  pl.BlockSpec((tm, l), lambda m, i: (m, 0)),
                  pl.BlockSpec((None, l, w), lambda m, i: (i, 0, 0)),
                  pl.BlockSpec(cs.shape, lambda m, i: (0, 0))],
        out_specs=pl.BlockSpec((None, tm, w), lambda m, i: (i, m, 0)),
        out_shape=jax.ShapeDtypeStruct((b, l, w), BF),
        compiler_params=_cparams("parallel", "parallel"),
        name="dft_mix",
    )(cos_l, sin_l, a, cs)


@functools.lru_cache(maxsize=None)
def _fft_tables(rows, cols, w):
    l = rows * cols
    c1, s1 = _dft_tables(rows)
    w1 = np.concatenate([c1, -s1], axis=0)
    c2, s2 = _dft_tables(cols)
    m2 = np.block([[c2, s2], [-s2, c2]])
    k1 = np.arange(rows)[:, None].astype(np.float64)
    n2 = np.arange(cols)[None, :].astype(np.float64)
    phi = 2.0 * np.pi * k1 * n2 / l
    tc = np.repeat(np.cos(phi), w, axis=1).astype(np.float32)
    ts = np.repeat(np.sin(phi), w, axis=1).astype(np.float32)
    return w1.astype(np.float32), m2.astype(np.float32), tc, ts


def _fft_rows_body(w1_ref, a_ref, tc_ref, ts_ref, u_ref):
    rows = a_ref.shape[0]
    u = _dot(w1_ref[...], a_ref[...])
    ur, ui = u[:rows], u[rows:]
    tc, ts = tc_ref[...], ts_ref[...]
    u_ref[0] = (ur * tc + ui * ts).astype(BF)
    u_ref[1] = (ui * tc - ur * ts).astype(BF)


def _fft_cols_body(k1_per, m2_ref, cs_ref, u_ref, o_ref):
    cols = m2_ref.shape[0] // 2
    w = u_ref.shape[-1]
    m2 = m2_ref[...]
    cs = cs_ref[...]
    for j in range(k1_per):
        rs = slice(j * cols, (j + 1) * cols)
        x = jnp.concatenate([u_ref[0, rs, :], u_ref[1, rs, :]], axis=0)
        z = _dot(m2, x).astype(BF)
        for g in range(FNET_GROUPS):
            sl = slice(g * LANE, (g + 1) * LANE)
            zz = jnp.concatenate([z[:cols, sl], z[cols:, sl]], axis=1)
            o_ref[:, j * w + g * LANE:j * w + (g + 1) * LANE] = _dot(zz, cs).astype(BF)


def _fft_mix(a):
    b, l, w = a.shape
    cols = GRID_W
    rows = l // cols
    w1, m2, tc, ts = (jnp.asarray(t) for t in _fft_tables(rows, cols, w))
    cc, sc = _dft_tables(LANE)
    cs = jnp.concatenate([jnp.asarray(cc), jnp.asarray(sc)], axis=0).astype(BF)
    n = cols * w
    nt = FFT_COLS_PER_STEP * w
    u = pl.pallas_call(
        _fft_rows_body,
        grid=(n // nt, b),
        in_specs=[pl.BlockSpec(w1.shape, lambda j, i: (0, 0)),
                  pl.BlockSpec((None, rows, nt), lambda j, i: (i, 0, j)),
                  pl.BlockSpec((rows, nt), lambda j, i: (0, j)),
                  pl.BlockSpec((rows, nt), lambda j, i: (0, j))],
        out_specs=pl.BlockSpec((None, 2, rows, nt), lambda j, i: (i, 0, 0, j)),
        out_shape=jax.ShapeDtypeStruct((b, 2, rows, n), BF),
        compiler_params=_cparams("parallel", "parallel"),
        name="fft_rows",
    )(w1.astype(BF), a.reshape(b, rows, n), tc, ts)
    k1_per = FFT_K1_PER_STEP
    y = pl.pallas_call(
        functools.partial(_fft_cols_body, k1_per),
        grid=(b, rows // k1_per),
        in_specs=[pl.BlockSpec(m2.shape, lambda i, j: (0, 0)),
                  pl.BlockSpec(cs.shape, lambda i, j: (0, 0)),
                  pl.BlockSpec((None, 2, k1_per * cols, w), lambda i, j: (i, 0, j, 0))],
        out_specs=pl.BlockSpec((None, cols, k1_per * w), lambda i, j: (i, 0, j)),
        out_shape=jax.ShapeDtypeStruct((b, cols, rows * w), BF),
        compiler_params=_cparams("parallel", "parallel"),
        name="fft_cols",
    )(m2.astype(BF), cs, u.reshape(b, 2, rows * cols, w))
    return y.reshape(b, l, w)


def _scan_chunk(rev, q_ref, k_ref, lf_ref, v_ref, o_ref, st_ref):
    c, w = lf_ref.shape
    nlev = c.bit_length() - 1
    lf = lf_ref[...]
    r = lax.broadcasted_iota(jnp.int32, (c, w), 0)
    pos = (c - 1 - r) if rev else r

    def from_prev(x, sh):
        return pltpu.roll(x, (c - sh) if rev else sh, axis=0)

    def from_next(x, sh):
        return pltpu.roll(x, sh if rev else (c - sh), axis=0)

    def neg_abs(x):
        return pltpu.bitcast(pltpu.bitcast(x, jnp.uint32) | jnp.uint32(0x80000000), F32)

    def spread_half(x, sh, low):
        parts = []
        for i in range(c // (2 * sh)):
            first, second = x[2 * i * sh:(2 * i + 1) * sh], x[(2 * i + 1) * sh:(2 * i + 2) * sh]
            src = (second if rev else first) if low else (first if rev else second)
            parts += [src, src]
        return jnp.concatenate(parts, axis=0)

    b = lf
    for j in range(nlev):
        sh = 1 << j
        if sh % 8 == 0:
            zero = jnp.zeros((sh, w), F32)
            b = b + (jnp.concatenate([b[sh:], zero], axis=0) if rev else jnp.concatenate([zero, b[:c - sh]], axis=0))
        else:
            b = b + jnp.where(pos >= sh, from_prev(b, sh), 0.0)

    q = q_ref[...]
    k = k_ref[...]
    ti = lax.broadcasted_iota(jnp.int32, (c, c), 0)
    si = lax.broadcasted_iota(jnp.int32, (c, c), 1)
    later = (ti < si) if rev else (ti > si)
    diff = ti ^ si
    att = [jnp.zeros((c, c), F32) for _ in range(HG_HEADS)]
    y = b
    for j in range(nlev):
        sh = 1 << j
        upper = ((pos >> j) & 1) == 1
        bound = spread_half(y, sh, True) if sh % 8 == 0 else jnp.where(upper, from_prev(y, sh), y)
        e = jnp.exp2(neg_abs(b - bound)).astype(BF)
        qm = q * e
        km = k * e
        level = jnp.logical_and((diff >> j) == 1, later)
        for h in range(HG_HEADS):
            sl = slice(h * LANE, (h + 1) * LANE)
            att[h] = att[h] + jnp.where(level, _dot_nt(qm[:, sl], km[:, sl]), 0.0)
        y = spread_half(y, sh, False) if sh % 8 == 0 else jnp.where(upper, y, from_next(y, sh))
    b_last = y
    qe = q * jnp.exp2(b).astype(BF)
    kd = k * jnp.exp2(b_last - b).astype(BF)
    carry = jnp.exp2(b_last[0:1, :])
    v = v_ref[...]
    vf = v.astype(F32)
    for h in range(HG_HEADS):
        sl = slice(h * LANE, (h + 1) * LANE)
        st = st_ref[h]
        inter = _dot_nt(qe[:, sl], st.astype(BF))
        diag = jnp.sum(q[:, sl].astype(F32) * k[:, sl].astype(F32), axis=-1, keepdims=True)
        o_ref[:, sl] = (inter + _dot(att[h].astype(BF), v[:, sl]) + diag * vf[:, sl]).astype(o_ref.dtype)
        st_ref[h] = st * carry[:, sl] + _dot_tn(v[:, sl], kd[:, sl])


def _scan_body(n_ctx, *refs):
    ins, outs, states = refs[:16], refs[16:20], refs[20:]
    s = pl.program_id(1)

    @pl.when(s == 0)
    def _():
        for st_ref in states:
            st_ref[...] = jnp.zeros_like(st_ref)

    def run(stream):
        for d in range(2):
            base = 8 * d + 4 * stream
            _scan_chunk(d == 1, *ins[base:base + 4], outs[2 * d + stream], states[d])

    @pl.when(s < n_ctx)
    def _():
        run(0)

    @pl.when(s >= n_ctx)
    def _():
        run(1)


def _scan(pc, pl_):
    qc, lfc, kc, vc = pc
    ql, lfl, kl, vl = pl_
    b, lc, w = qc.shape
    ll = ql.shape[1]
    c = SCAN_CHUNK
    n_ctx, n_lat = lc // c, ll // c

    def spec(idx, col):
        return pl.BlockSpec((None, c, w), lambda i, s: (i, idx(s), col))

    in_specs, out_specs, args = [], [], []
    for d in range(2):
        if d:
            ci = lambda s: jnp.maximum(n_ctx - 1 - s, 0)
            li = lambda s: n_lat - 1 - jnp.maximum(s - n_ctx, 0)
        else:
            ci = lambda s: jnp.minimum(s, n_ctx - 1)
            li = lambda s: jnp.maximum(s - n_ctx, 0)
        in_specs += [spec(ci, 0), spec(ci, d), spec(ci, d), spec(ci, 0),
                     spec(li, 0), spec(li, d), spec(li, d), spec(li, 0)]
        args += [qc, kc, lfc, vc, ql, kl, lfl, vl]
        out_specs += [spec(ci, 0), spec(li, 0)]
    shp = lambda l: jax.ShapeDtypeStruct((b, l, w), BF)
    of_c, of_l, ob_c, ob_l = pl.pallas_call(
        functools.partial(_scan_body, n_ctx),
        grid=(b, n_ctx + n_lat),
        in_specs=in_specs,
        out_specs=out_specs,
        out_shape=[shp(lc), shp(ll), shp(lc), shp(ll)],
        scratch_shapes=[pltpu.VMEM((HG_HEADS, HEAD_DIM, HEAD_DIM), F32) for _ in range(2)],
        compiler_params=_cparams("parallel", "arbitrary"),
        name="hgrn_scan",
    )(*args)
    return (of_c, of_l), (ob_c, ob_l)


def _out_even_body(yf_ref, of_ref, ob_ref, g_ref, x_ref, mod_ref, gn_ref, w_ref, o_ref):
    o = of_ref[...].astype(F32) + ob_ref[...].astype(F32)
    g = g_ref[...].astype(F32)
    gn = gn_ref[...]
    parts = []
    for h in range(HG_HEADS):
        sl = slice(h * LANE, (h + 1) * LANE)
        parts.append((_rms(o[:, sl]) * gn[:, sl] * _silu(g[:, sl])).astype(BF))
    gated = jnp.concatenate(parts, axis=1)
    w = HG_W
    y = _dot(yf_ref[...], w_ref[0:w, :]) + _dot(gated, w_ref[w:2 * w, :])
    o_ref[...] = x_ref[...] + mod_ref[2:3, :] * y


def _out_even(yf, o_f, o_b, g, x, mod, gn, w_out):
    b, l, d = x.shape
    tm = _row_tile(l, 512)
    row = lambda width: pl.BlockSpec((None, tm, width), lambda i, t: (i, t, 0))
    return pl.pallas_call(
        _out_even_body,
        grid=(b, l // tm),
        in_specs=[row(HG_W), row(HG_W), row(HG_W), row(HG_W), row(d),
                  pl.BlockSpec((None, 6, d), lambda i, t: (i, 0, 0)),
                  pl.BlockSpec(gn.shape, lambda i, t: (0, 0)),
                  pl.BlockSpec(w_out.shape, lambda i, t: (0, 0))],
        out_specs=row(d),
        out_shape=jax.ShapeDtypeStruct((b, l, d), F32),
        compiler_params=_cparams("parallel", "parallel"),
        name="out_even",
    )(yf, o_f, o_b, g, x, mod, gn, w_out)


def _proj_att_body(x_ref, mod_ref, w_ref, wvt_ref, qg_ref, kg_ref, cos_ref, sin_ref, q_ref, k_ref, vt_ref):
    mod = mod_ref[...]
    h = _modulate(x_ref[...], mod[0:1], mod[1:2])
    cos = cos_ref[...]
    sin = sin_ref[...]
    scale = HEAD_DIM ** -0.5 * math.log2(math.e)
    for i in range(ATT_HEADS + ATT_KV_HEADS):
        sl = slice(i * LANE, (i + 1) * LANE)
        p = _dot(h, w_ref[:, sl])
        gain = qg_ref[...] if i < ATT_HEADS else kg_ref[...]
        p = _rms(p) * gain
        p = p * cos + pltpu.roll(p, HEAD_DIM // 2, axis=1) * sin
        if i < ATT_HEADS:
            q_ref[:, sl] = (p * scale).astype(BF)
        else:
            k_ref[:, (i - ATT_HEADS) * LANE:(i - ATT_HEADS + 1) * LANE] = p.astype(BF)
    vt = _dot_nt(wvt_ref[...], h)
    for hk in range(ATT_KV_HEADS):
        vt_ref[hk, 0:HEAD_DIM, :] = vt[hk * LANE:(hk + 1) * LANE].astype(BF)
        vt_ref[hk, HEAD_DIM:, :] = jnp.ones((ATT_ONES_ROWS, vt.shape[1]), BF)


def _proj_att(x, mod, w_qk, w_vt, qg, kg, cos, sin):
    b, l, d = x.shape
    tm = _row_tile(l, 512)
    qw, kw = ATT_HEADS * LANE, ATT_KV_HEADS * LANE
    vt_rows = HEAD_DIM + ATT_ONES_ROWS
    row = lambda width: pl.BlockSpec((None, tm, width), lambda i, t: (i, t, 0))
    const = lambda a: pl.BlockSpec(a.shape, lambda i, t: (0, 0))
    tab = pl.BlockSpec((tm, LANE), lambda i, t: (t, 0))
    return pl.pallas_call(
        _proj_att_body,
        grid=(b, l // tm),
        in_specs=[row(d), pl.BlockSpec((None, 6, d), lambda i, t: (i, 0, 0)),
                  const(w_qk), const(w_vt), const(qg), const(kg), tab, tab],
        out_specs=[row(qw), row(kw),
                   pl.BlockSpec((None, ATT_KV_HEADS, vt_rows, tm), lambda i, t: (i, 0, 0, t))],
        out_shape=[jax.ShapeDtypeStruct((b, l, qw), BF), jax.ShapeDtypeStruct((b, l, kw), BF),
                   jax.ShapeDtypeStruct((b, ATT_KV_HEADS, vt_rows, l), BF)],
        compiler_params=_cparams("parallel", "parallel"),
        name="proj_att",
    )(x, mod, w_qk, w_vt, qg, kg, cos, sin)


@functools.lru_cache(maxsize=None)
def _rope_tables(l):
    t = np.arange(l)
    row = (t // GRID_W).astype(np.float64)
    col = (t % GRID_W).astype(np.float64)
    n_freq = HEAD_DIM // 4
    freqs = ROPE_THETA ** (-np.arange(n_freq, dtype=np.float64) / n_freq)
    ang = np.concatenate([row[:, None] * freqs, col[:, None] * freqs], axis=-1)
    cos, sin = np.cos(ang), np.sin(ang)
    return (np.concatenate([cos, cos], axis=-1).astype(np.float32),
            np.concatenate([-sin, sin], axis=-1).astype(np.float32))


ATT_KV_BLOCK = 256
ATT_ONES_ROWS = 16


def _attn_body(n_src, q_ref, *refs):
    o_ref = refs[2 * n_src]
    tq = q_ref.shape[0]
    n = ATT_GROUP * tq
    vt_rows = refs[1].shape[0]
    q = q_ref[...]
    qs = jnp.concatenate([q[:, g * LANE:(g + 1) * LANE] for g in range(ATT_GROUP)], axis=0)
    blocks = []
    for i in range(n_src):
        k_ref, vt_ref = refs[2 * i], refs[2 * i + 1]
        lk = k_ref.shape[0]
        bk = min(ATT_KV_BLOCK, lk)
        blocks += [(k_ref, vt_ref, j * bk, bk) for j in range(lk // bk)]

    def scores(blk):
        k_ref, _, off, bk = blk
        return _dot_nt(k_ref[off:off + bk, :], qs)

    m = jnp.full((1, n), -1e30, F32)
    acc = jnp.zeros((vt_rows, n), F32)
    def accumulate(acc, blk, alpha, p):
        _, vt_ref, off, bk = blk
        return alpha * acc + _dot(vt_ref[:, off:off + bk], p)

    nb = len(blocks)
    ahead = [scores(blocks[j]) for j in range(min(2, nb))]
    pending = None
    for j in range(nb):
        s = ahead.pop(0)
        if j + 2 < nb:
            ahead.append(scores(blocks[j + 2]))
        m_new = jnp.maximum(m, jnp.max(s, axis=0, keepdims=True))
        alpha = jnp.exp2(m - m_new)
        p = jnp.exp2((s - m_new).astype(BF))
        m = m_new
        if pending is not None:
            acc = accumulate(acc, *pending)
        pending = (blocks[j], alpha, p)
    acc = accumulate(acc, *pending)

    o = acc[:HEAD_DIM] * (1.0 / acc[HEAD_DIM:HEAD_DIM + 1])
    for g in range(ATT_GROUP):
        o_ref[:, g * LANE:(g + 1) * LANE] = o[:, g * tq:(g + 1) * tq].T.astype(BF)


def _attention(q, kvs, tq):
    b, lq, qw = q.shape
    gw = ATT_GROUP * LANE
    in_specs = [pl.BlockSpec((None, tq, gw), lambda i, h, t: (i, t, h))]
    args = [q]
    for k, vt in kvs:
        lk = k.shape[1]
        in_specs.append(pl.BlockSpec((None, lk, LANE), lambda i, h, t: (i, 0, h)))
        in_specs.append(pl.BlockSpec((None, None, vt.shape[2], lk), lambda i, h, t: (i, h, 0, 0)))
        args += [k, vt]
    return pl.pallas_call(
        functools.partial(_attn_body, len(kvs)),
        grid=(b, ATT_KV_HEADS, lq // tq),
        in_specs=in_specs,
        out_specs=pl.BlockSpec((None, tq, gw), lambda i, h, t: (i, t, h)),
        out_shape=jax.ShapeDtypeStruct((b, lq, qw), BF),
        compiler_params=_cparams("parallel", "parallel", "parallel"),
        name="attention",
    )(*args)


def _out_att_body(o_ref, x_ref, mod_ref, w_ref, y_ref):
    y_ref[...] = x_ref[...] + mod_ref[2:3, :] * _dot(o_ref[...], w_ref[...])


def _out_att(o, x, mod, w_out):
    b, l, d = x.shape
    tm = _row_tile(l, 512)
    row = lambda width: pl.BlockSpec((None, tm, width), lambda i, t: (i, t, 0))
    return pl.pallas_call(
        _out_att_body,
        grid=(b, l // tm),
        in_specs=[row(o.shape[-1]), row(d), pl.BlockSpec((None, 6, d), lambda i, t: (i, 0, 0)),
                  pl.BlockSpec(w_out.shape, lambda i, t: (0, 0))],
        out_specs=row(d),
        out_shape=jax.ShapeDtypeStruct((b, l, d), F32),
        compiler_params=_cparams("parallel", "parallel"),
        name="out_att",
    )(o, x, mod, w_out)


def _ffn_body(on_grid, ck, has_final, *refs):
    if on_grid:
        xp_ref, xm_ref, xn_ref = refs[:3]
        refs = refs[3:]
    else:
        xm_ref = refs[0]
        refs = refs[1:]
    mod_ref, wu_ref, cw_ref, cb_ref, wd_ref = refs[:5]
    fg_ref = refs[5] if has_final else None
    o_ref, h_buf = refs[-2:]
    tm = xm_ref.shape[0]
    d_ff = wd_ref.shape[0]
    n_chunk = d_ff // ck
    cols = lambda c: slice(c * ck, (c + 1) * ck)
    halo = GRID_W if on_grid else 0
    rows = tm + 2 * halo
    mod = mod_ref[...]
    md = lambda x: _modulate(x, mod[3:4], mod[4:5])
    if on_grid:
        h_buf[0:halo, :] = md(xp_ref[...])
        h_buf[halo:halo + tm, :] = md(xm_ref[...])
        h_buf[halo + tm:rows, :] = md(xn_ref[...])
    else:
        h_buf[...] = md(xm_ref[...])
    r = lax.broadcasted_iota(jnp.int32, (rows, ck), 0)
    if on_grid:
        t = pl.program_id(1)
        nt = pl.num_programs(1)
        col = r & (GRID_W - 1)
        has_left = col != 0
        has_right = col != GRID_W - 1
        inside = jnp.logical_and(jnp.logical_or(r >= halo, t > 0),
                                 jnp.logical_or(r < halo + tm, t < nt - 1))
    else:
        has_left = r != 0
        has_right = r != rows - 1

    hm_sl = slice(halo, halo + tm)

    def up(c):
        gate = _dot(h_buf[...], wu_ref[:, cols(c)])
        val = _dot(h_buf[hm_sl, :], wu_ref[:, d_ff + c * ck:d_ff + (c + 1) * ck])
        return gate, val

    def down(c, gate, val):
        if on_grid:
            gate = jnp.where(inside, gate, 0.0)
        gl = jnp.where(has_left, pltpu.roll(gate, 1, axis=0), 0.0)
        gr = jnp.where(has_right, pltpu.roll(gate, rows - 1, axis=0), 0.0)
        cw = cw_ref[:, cols(c)]
        if on_grid:
            conv = None
            for dy in range(3):
                sl = slice(dy * halo, dy * halo + tm)
                term = (cw[3 * dy:3 * dy + 1] * gl[sl] + cw[3 * dy + 1:3 * dy + 2] * gate[sl]
                        + cw[3 * dy + 2:3 * dy + 3] * gr[sl])
                conv = term if conv is None else conv + term
        else:
            conv = cw[3:4] * gl + cw[4:5] * gate + cw[5:6] * gr
        return (_silu(conv + cb_ref[:, cols(c)]) * val).astype(BF)

    nxt = up(0)
    acc = None
    acts = []
    for c in range(n_chunk):
        cur = nxt
        if c + 1 < n_chunk:
            nxt = up(c + 1)
        acts.append(down(c, *cur))
        if len(acts) == FFN_DOWN_GROUP or c + 1 == n_chunk:
            c0 = c + 1 - len(acts)
            part = _dot(jnp.concatenate(acts, axis=1), wd_ref[c0 * ck:(c + 1) * ck, :])
            acc = part if acc is None else acc + part
            acts = []
    y = xm_ref[...] + mod[5:6] * acc
    if has_final:
        y = _rms(y) * fg_ref[...]
    o_ref[...] = y


def _ffn(x, mod, wu, cw, cb, wd, on_grid, final_g=None):
    b, l, d = x.shape
    ck = _ffn_chunk(wd.shape[0])
    extra = [] if final_g is None else [final_g.reshape(1, d)]
    const = lambda a: pl.BlockSpec(a.shape, lambda i, t: (0,) * a.ndim, pipeline_mode=pl.Buffered(1))
    modspec = pl.BlockSpec((None, 6, d), lambda i, t: (i, 0, 0))
    if on_grid:
        tm = FFN_TILE if l % FFN_TILE == 0 else 512
        per = tm // GRID_W
        n_rows = l // GRID_W
        x_specs = [pl.BlockSpec((None, GRID_W, d), lambda i, t: (i, jnp.maximum(t * per - 1, 0), 0)),
                   pl.BlockSpec((None, tm, d), lambda i, t: (i, t, 0)),
                   pl.BlockSpec((None, GRID_W, d), lambda i, t: (i, jnp.minimum((t + 1) * per, n_rows - 1), 0))]
        xs = [x, x, x]
        rows = tm + 2 * GRID_W
    else:
        tm = l
        x_specs = [pl.BlockSpec((None, tm, d), lambda i, t: (i, t, 0))]
        xs = [x]
        rows = tm
    return pl.pallas_call(
        functools.partial(_ffn_body, on_grid, ck, final_g is not None),
        grid=(b, l // tm),
        in_specs=(x_specs + [modspec, const(wu), const(cw), const(cb), const(wd)]
                  + [const(a) for a in extra]),
        out_specs=pl.BlockSpec((None, tm, d), lambda i, t: (i, t, 0)),
        out_shape=jax.ShapeDtypeStruct((b, l, d), F32),
        scratch_shapes=[pltpu.VMEM((rows, d), BF)],
        compiler_params=_cparams("parallel", "parallel"),
        name="ffn_grid" if on_grid else "ffn_seq",
    )(*xs, mod, wu, cw, cb, wd, *extra)


def _ffn_chunk(d_ff):
    for ck in (256, 128):
        if d_ff % ck == 0:
            return ck
    return d_ff


def _even_mixer(x, ctx, ml, mc, w_in, w_out, lb_logits, gn, e, need_ctx):
    w_in = w_in.astype(BF)
    w_out = w_out.astype(BF)
    gn = gn.reshape(1, -1)
    lbl = lb_logits.reshape(lb_logits.shape[0], -1).astype(F32)
    a_l, q_l, lf_l, k_l, v_l, g_l = _proj_even(x, ml, w_in, lbl, e)
    a_c, q_c, lf_c, k_c, v_c, g_c = _proj_even(ctx, mc, w_in, lbl, e)
    pc, pl_ = (q_c, lf_c, k_c, v_c), (q_l, lf_l, k_l, v_l)
    (of_c, of_l), (ob_c, ob_l) = _scan(pc, pl_)
    two_stage = x.shape[1] % (GRID_W * FFT_K1_PER_STEP) == 0 and GRID_W % FFT_COLS_PER_STEP == 0
    x = _out_even((_fft_mix if two_stage else _dft_mix)(a_l), of_l, ob_l, g_l, x, ml, gn, w_out)
    if need_ctx:
        ctx = _out_even(_dft_mix(a_c), of_c, ob_c, g_c, ctx, mc, gn, w_out)
    return x, ctx


def _att_mixer(x, ctx, ml, mc, w_qkv, qn_g, kn_g, w_out, need_ctx):
    l, lc = x.shape[1], ctx.shape[1]
    perm = np.concatenate([np.arange(0, HEAD_DIM, 2), np.arange(1, HEAD_DIM, 2)])
    n_qk = ATT_HEADS + ATT_KV_HEADS
    cols = np.concatenate([h * HEAD_DIM + perm for h in range(n_qk)])
    w_qk = w_qkv[:, cols].astype(BF)
    w_vt = w_qkv[:, n_qk * HEAD_DIM:].T.astype(BF)
    w_out = w_out.astype(BF)
    qg = qn_g[perm].reshape(1, -1)
    kg = kn_g[perm].reshape(1, -1)
    cos_l, sin_l = (jnp.asarray(t) for t in _rope_tables(l))
    cos_c, sin_c = jnp.ones((lc, HEAD_DIM), F32), jnp.zeros((lc, HEAD_DIM), F32)
    q_l, k_l, vt_l = _proj_att(x, ml, w_qk, w_vt, qg, kg, cos_l, sin_l)
    q_c, k_c, vt_c = _proj_att(ctx, mc, w_qk, w_vt, qg, kg, cos_c, sin_c)
    x = _out_att(_attention(q_l, [(k_l, vt_l), (k_c, vt_c)], min(l, 256)), x, ml, w_out)
    if need_ctx:
        ctx = _out_att(_attention(q_c, [(k_c, vt_c)], lc), ctx, mc, w_out)
    return x, ctx


def _ffn_layer(x, ctx, ml, mc, w_up, conv_w, conv_b, w_down, need_ctx, final_g=None):
    d = x.shape[-1]
    d_ff = w_down.shape[0]
    wu = w_up.astype(BF)
    cw = conv_w.reshape(9, d_ff)
    cb = conv_b.reshape(1, d_ff)
    wd = w_down.astype(BF)
    x = _ffn(x, ml, wu, cw, cb, wd, True, final_g)
    if need_ctx:
        ctx = _ffn(ctx, mc, wu, cw, cb, wd, False)
    return x, ctx


def _mod_vectors(c, c_ctx, w_mod, b_mod):
    b, d = c.shape
    depth = w_mod.shape[0]
    pad = (-(b + 1)) % 8
    cc = jnp.concatenate([c, c_ctx[None, :], jnp.zeros((pad, d), F32)], axis=0)
    mods = _modulation(cc, w_mod, b_mod)
    mod_lat = mods[:, :b].reshape(depth, b, 6, d)
    mod_ctx = jnp.broadcast_to(mods[:, b].reshape(depth, 1, 6, d), (depth, b, 6, d))
    return mod_lat, mod_ctx


def kernel(x, c, ctx, c_ctx, w_mod, b_mod, w_in_ab, w_out_ab, hg_lb_logits, hg_norm_g, w_qkv, q_norm_g,
           k_norm_g, w_out_att, w_up, conv_w, conv_b, w_down, final_norm_g):
    depth = w_mod.shape[0]
    mod_lat, mod_ctx = _mod_vectors(c, c_ctx, w_mod, b_mod)
    for layer in range(depth):
        need_ctx = layer != depth - 1
        ml, mc = mod_lat[layer], mod_ctx[layer]
        if layer % 2 == 0:
            e = layer // 2
            x, ctx = _even_mixer(x, ctx, ml, mc, w_in_ab[e], w_out_ab[e], hg_lb_logits, hg_norm_g[e], e, need_ctx)
        else:
            o = layer // 2
            x, ctx = _att_mixer(x, ctx, ml, mc, w_qkv[o], q_norm_g[o], k_norm_g[o], w_out_att[o], need_ctx)
        x, ctx = _ffn_layer(x, ctx, ml, mc, w_up[layer], conv_w[layer], conv_b[layer], w_down[layer], need_ctx,
                            None if need_ctx else final_norm_g)
    return x
```

```python
import functools
import math

import numpy as np
import jax
import jax.numpy as jnp
from jax import lax
from jax.experimental import pallas as pl
from jax.experimental.pallas import tpu as pltpu

F32 = jnp.float32
BF = jnp.bfloat16

EPS = 1e-6
GRID_W = 64
LANE = 128
SUBLANES = 8
HEAD_DIM = 128
FNET_GROUPS = 4
HG_HEADS = 4
HG_W = HG_HEADS * HEAD_DIM
ATT_HEADS = 8
ATT_KV_HEADS = 2
ATT_GROUP = ATT_HEADS // ATT_KV_HEADS
ROPE_THETA = 10000.0
SCAN_CHUNK = 128
FFN_TILE = 512
FFN_DOWN_GROUP = 6
VMEM_LIMIT = 56 * 1024 * 1024


def _cparams(*sem):
    return pltpu.CompilerParams(dimension_semantics=sem, vmem_limit_bytes=VMEM_LIMIT)


def _dot(a, b):
    return jnp.dot(a, b, preferred_element_type=F32)


def _dot_nt(a, b):
    return lax.dot_general(a, b, (((1,), (1,)), ((), ())), preferred_element_type=F32)


def _dot_tn(a, b):
    return lax.dot_general(a, b, (((0,), (0,)), ((), ())), preferred_element_type=F32)


def _rms(x):
    return x * lax.rsqrt(jnp.mean(x * x, axis=-1, keepdims=True) + EPS)


def _silu(x):
    return x * jax.nn.sigmoid(x)


def _modulate(x, shift, scale):
    return (_rms(x) * (1.0 + scale) + shift).astype(BF)


def _row_tile(n, pref):
    return pref if n % pref == 0 else n


def _mod_body(c_ref, w_ref, b_ref, o_ref):
    s = _silu(c_ref[...]).astype(BF)
    o_ref[...] = _dot(s, w_ref[...].astype(BF)) + b_ref[...]


def _modulation(cc, w_mod, b_mod):
    depth, d, six_d = w_mod.shape
    r = cc.shape[0]
    return pl.pallas_call(
        _mod_body,
        grid=(depth, six_d // d),
        in_specs=[pl.BlockSpec((r, d), lambda l, j: (0, 0)),
                  pl.BlockSpec((None, d, d), lambda l, j: (l, 0, j)),
                  pl.BlockSpec((None, 1, d), lambda l, j: (l, 0, j))],
        out_specs=pl.BlockSpec((None, r, d), lambda l, j: (l, 0, j)),
        out_shape=jax.ShapeDtypeStruct((depth, r, six_d), F32),
        compiler_params=_cparams("parallel", "parallel"),
        name="modulation",
    )(cc, w_mod, b_mod.reshape(depth, 1, six_d))


def _proj_even_body(e, n_even, x_ref, mod_ref, w_ref, lbl_ref, a_ref, q_ref, lf_ref, k_ref, v_ref, g_ref):
    mod = mod_ref[...]
    h = _modulate(x_ref[...], mod[0:1], mod[1:2])
    rows = [lbl_ref[i:i + 1, :] for i in range(n_even)]
    mx = functools.reduce(jnp.maximum, rows)
    ex = [jnp.exp(r - mx) for r in rows]
    den = functools.reduce(lambda a, b: a + b, ex)
    lb = jnp.zeros_like(mx)
    for i in range(1, e + 1):
        lb = lb + ex[i] / den
    w = HG_W
    a_ref[...] = _dot(h, w_ref[:, 0:w]).astype(BF)
    q_ref[...] = _silu(_dot(h, w_ref[:, w:2 * w])).astype(BF)
    for d in range(2):
        z = _dot(h, w_ref[:, (2 + d) * w:(3 + d) * w])
        lbd = lb[:, d * w:(d + 1) * w]
        f = lbd + (1.0 - lbd) * jax.nn.sigmoid(z)
        lf_ref[:, d * w:(d + 1) * w] = jnp.log2(f)
        k_ref[:, d * w:(d + 1) * w] = (1.0 - f).astype(BF)
    v_ref[...] = _dot(h, w_ref[:, 4 * w:5 * w]).astype(BF)
    g_ref[...] = _dot(h, w_ref[:, 5 * w:6 * w]).astype(BF)


def _proj_even(x, mod, w_in, lb_logits, e):
    b, l, d = x.shape
    n_even = lb_logits.shape[0]
    tm = _row_tile(l, 512)
    w = HG_W
    row = lambda width: pl.BlockSpec((None, tm, width), lambda i, t: (i, t, 0))
    shp = lambda width, dt: jax.ShapeDtypeStruct((b, l, width), dt)
    return pl.pallas_call(
        functools.partial(_proj_even_body, e, n_even),
        grid=(b, l // tm),
        in_specs=[row(d),
                  pl.BlockSpec((None, 6, d), lambda i, t: (i, 0, 0)),
                  pl.BlockSpec(w_in.shape, lambda i, t: (0, 0)),
                  pl.BlockSpec(lb_logits.shape, lambda i, t: (0, 0))],
        out_specs=[row(w), row(w), row(2 * w), row(2 * w), row(w), row(w)],
        out_shape=[shp(w, BF), shp(w, BF), shp(2 * w, F32), shp(2 * w, BF), shp(w, BF), shp(w, BF)],
        compiler_params=_cparams("parallel", "parallel"),
        name="proj_even",
    )(x, mod, w_in, lb_logits)


@functools.lru_cache(maxsize=None)
def _dft_tables(n):
    idx = np.arange(n, dtype=np.int64)
    ang = 2.0 * np.pi * ((idx[:, None] * idx[None, :]) % n).astype(np.float64) / n
    s = 1.0 / math.sqrt(n)
    return (np.cos(ang) * s).astype(np.float32), (np.sin(ang) * s).astype(np.float32)


def _dft_body(c_ref, s_ref, a_ref, cs_ref, o_ref):
    a = a_ref[...]
    p = _dot(c_ref[...], a).astype(BF)
    q = _dot(s_ref[...], a).astype(BF)
    cs = cs_ref[...]
    for g in range(FNET_GROUPS):
        sl = slice(g * LANE, (g + 1) * LANE)
        pq = jnp.concatenate([p[:, sl], q[:, sl]], axis=1)
        o_ref[:, sl] = _dot(pq, cs).astype(BF)


def _dft_mix(a):
    b, l, w = a.shape
    cl, sl_ = _dft_tables(l)
    cc, sc = _dft_tables(LANE)
    cos_l = jnp.asarray(cl).astype(BF)
    sin_l = jnp.asarray(sl_).astype(BF)
    cs = jnp.concatenate([jnp.asarray(cc), -jnp.asarray(sc)], axis=0).astype(BF)
    tm = _row_tile(l, 512)
    return pl.pallas_call(
        _dft_body,
        grid=(l // tm, b),
        in_specs=[pl.BlockSpec((tm, l), lambda m, i: (m, 0)),
                  pl.BlockSpec((tm, l), lambda m, i: (m, 0)),
                  pl.BlockSpec((None, l, w), lambda m, i: (i, 0, 0)),
                  pl.BlockSpec(cs.shape, lambda m, i: (0, 0))],
        out_specs=pl.BlockSpec((None, tm, w), lambda m, i: (i, m, 0)),
        out_shape=jax.ShapeDtypeStruct((b, l, w), BF),
        compiler_params=_cparams("parallel", "parallel"),
        name="dft_mix",
    )(cos_l, sin_l, a, cs)


def _scan_chunk(rev, q_ref, k_ref, lf_ref, v_ref, o_ref, st_ref):
    c, w = lf_ref.shape
    nlev = c.bit_length() - 1
    lf = lf_ref[...]
    r = lax.broadcasted_iota(jnp.int32, (c, w), 0)
    pos = (c - 1 - r) if rev else r

    def tile_roll(x, sh):
        return pltpu.roll(x.reshape(c // SUBLANES, SUBLANES, w), sh, axis=1).reshape(c, w)

    def from_prev(x, sh):
        return tile_roll(x, (SUBLANES - sh) if rev else sh)

    def from_next(x, sh):
        return tile_roll(x, sh if rev else (SUBLANES - sh))

    def shift_tiles(x, sh):
        zero = jnp.zeros((sh, w), F32)
        return jnp.concatenate([x[sh:], zero], axis=0) if rev else jnp.concatenate([zero, x[:c - sh]], axis=0)

    def neg_abs(x):
        return pltpu.bitcast(pltpu.bitcast(x, jnp.uint32) | jnp.uint32(0x80000000), F32)

    def spread_half(x, sh, low):
        parts = []
        for i in range(c // (2 * sh)):
            first, second = x[2 * i * sh:(2 * i + 1) * sh], x[(2 * i + 1) * sh:(2 * i + 2) * sh]
            src = (second if rev else first) if low else (first if rev else second)
            parts += [src, src]
        return jnp.concatenate(parts, axis=0)

    pos8 = pos & (SUBLANES - 1)
    b = lf
    sh = 1
    while sh < SUBLANES:
        b = b + jnp.where(pos8 >= sh, from_prev(b, sh), 0.0)
        sh *= 2
    last = 0 if rev else SUBLANES - 1
    b3 = b.reshape(c // SUBLANES, SUBLANES, w)
    tot = jnp.broadcast_to(b3[:, last:last + 1, :], b3.shape).reshape(c, w)
    while sh < c:
        tot = tot + shift_tiles(tot, sh)
        sh *= 2
    b = b + shift_tiles(tot, SUBLANES)

    q = q_ref[...]
    k = k_ref[...]
    ti = lax.broadcasted_iota(jnp.int32, (c, c), 0)
    si = lax.broadcasted_iota(jnp.int32, (c, c), 1)
    later = (ti < si) if rev else (ti > si)
    diff = ti ^ si
    att = [jnp.zeros((c, c), F32) for _ in range(HG_HEADS)]
    y = b
    for j in range(nlev):
        sh = 1 << j
        upper = ((pos >> j) & 1) == 1
        bound = spread_half(y, sh, True) if sh % 8 == 0 else jnp.where(upper, from_prev(y, sh), y)
        e = jnp.exp2(neg_abs(b - bound)).astype(BF)
        qm = q * e
        km = k * e
        level = jnp.logical_and((diff >> j) == 1, later)
        for h in range(HG_HEADS):
            sl = slice(h * LANE, (h + 1) * LANE)
            att[h] = att[h] + jnp.where(level, _dot_nt(qm[:, sl], km[:, sl]), 0.0)
        y = spread_half(y, sh, False) if sh % 8 == 0 else jnp.where(upper, y, from_next(y, sh))
    b_last = y
    qe = q * jnp.exp2(b).astype(BF)
    kd = k * jnp.exp2(b_last - b).astype(BF)
    carry = jnp.exp2(b_last[0:1, :])
    v = v_ref[...]
    vf = v.astype(F32)
    for h in range(HG_HEADS):
        sl = slice(h * LANE, (h + 1) * LANE)
        st = st_ref[h]
        inter = _dot_nt(qe[:, sl], st.astype(BF))
        diag = jnp.sum(q[:, sl].astype(F32) * k[:, sl].astype(F32), axis=-1, keepdims=True)
        o_ref[:, sl] = (inter + _dot(att[h].astype(BF), v[:, sl]) + diag * vf[:, sl]).astype(o_ref.dtype)
        st_ref[h] = st * carry[:, sl] + _dot_tn(v[:, sl], kd[:, sl])


def _scan_body(n_ctx, *refs):
    ins, outs, states = refs[:16], refs[16:20], refs[20:]
    s = pl.program_id(1)

    @pl.when(s == 0)
    def _():
        for st_ref in states:
            st_ref[...] = jnp.zeros_like(st_ref)

    def run(stream):
        for d in range(2):
            base = 8 * d + 4 * stream
            _scan_chunk(d == 1, *ins[base:base + 4], outs[2 * d + stream], states[d])

    @pl.when(s < n_ctx)
    def _():
        run(0)

    @pl.when(s >= n_ctx)
    def _():
        run(1)


def _scan(pc, pl_):
    qc, lfc, kc, vc = pc
    ql, lfl, kl, vl = pl_
    b, lc, w = qc.shape
    ll = ql.shape[1]
    c = SCAN_CHUNK
    n_ctx, n_lat = lc // c, ll // c

    def spec(idx, col):
        return pl.BlockSpec((None, c, w), lambda i, s: (i, idx(s), col))

    in_specs, out_specs, args = [], [], []
    for d in range(2):
        if d:
            ci = lambda s: jnp.maximum(n_ctx - 1 - s, 0)
            li = lambda s: n_lat - 1 - jnp.maximum(s - n_ctx, 0)
        else:
            ci = lambda s: jnp.minimum(s, n_ctx - 1)
            li = lambda s: jnp.maximum(s - n_ctx, 0)
        in_specs += [spec(ci, 0), spec(ci, d), spec(ci, d), spec(ci, 0),
                     spec(li, 0), spec(li, d), spec(li, d), spec(li, 0)]
        args += [qc, kc, lfc, vc, ql, kl, lfl, vl]
        out_specs += [spec(ci, 0), spec(li, 0)]
    shp = lambda l: jax.ShapeDtypeStruct((b, l, w), BF)
    of_c, of_l, ob_c, ob_l = pl.pallas_call(
        functools.partial(_scan_body, n_ctx),
        grid=(b, n_ctx + n_lat),
        in_specs=in_specs,
        out_specs=out_specs,
        out_shape=[shp(lc), shp(ll), shp(lc), shp(ll)],
        scratch_shapes=[pltpu.VMEM((HG_HEADS, HEAD_DIM, HEAD_DIM), F32) for _ in range(2)],
        compiler_params=_cparams("parallel", "arbitrary"),
        name="hgrn_scan",
    )(*args)
    return (of_c, of_l), (ob_c, ob_l)


def _out_even_body(yf_ref, of_ref, ob_ref, g_ref, x_ref, mod_ref, gn_ref, w_ref, o_ref):
    o = of_ref[...].astype(F32) + ob_ref[...].astype(F32)
    g = g_ref[...].astype(F32)
    gn = gn_ref[...]
    parts = []
    for h in range(HG_HEADS):
        sl = slice(h * LANE, (h + 1) * LANE)
        parts.append((_rms(o[:, sl]) * gn[:, sl] * _silu(g[:, sl])).astype(BF))
    gated = jnp.concatenate(parts, axis=1)
    w = HG_W
    y = _dot(yf_ref[...], w_ref[0:w, :]) + _dot(gated, w_ref[w:2 * w, :])
    o_ref[...] = x_ref[...] + mod_ref[2:3, :] * y


def _out_even(yf, o_f, o_b, g, x, mod, gn, w_out):
    b, l, d = x.shape
    tm = _row_tile(l, 512)
    row = lambda width: pl.BlockSpec((None, tm, width), lambda i, t: (i, t, 0))
    return pl.pallas_call(
        _out_even_body,
        grid=(b, l // tm),
        in_specs=[row(HG_W), row(HG_W), row(HG_W), row(HG_W), row(d),
                  pl.BlockSpec((None, 6, d), lambda i, t: (i, 0, 0)),
                  pl.BlockSpec(gn.shape, lambda i, t: (0, 0)),
                  pl.BlockSpec(w_out.shape, lambda i, t: (0, 0))],
        out_specs=row(d),
        out_shape=jax.ShapeDtypeStruct((b, l, d), F32),
        compiler_params=_cparams("parallel", "parallel"),
        name="out_even",
    )(yf, o_f, o_b, g, x, mod, gn, w_out)


def _proj_att_body(x_ref, mod_ref, w_ref, wvt_ref, qg_ref, kg_ref, cos_ref, sin_ref, q_ref, k_ref, vt_ref):
    mod = mod_ref[...]
    h = _modulate(x_ref[...], mod[0:1], mod[1:2])
    cos = cos_ref[...]
    sin = sin_ref[...]
    scale = HEAD_DIM ** -0.5 * math.log2(math.e)
    for i in range(ATT_HEADS + ATT_KV_HEADS):
        sl = slice(i * LANE, (i + 1) * LANE)
        p = _dot(h, w_ref[:, sl])
        gain = qg_ref[...] if i < ATT_HEADS else kg_ref[...]
        p = _rms(p) * gain
        p = p * cos + pltpu.roll(p, HEAD_DIM // 2, axis=1) * sin
        if i < ATT_HEADS:
            q_ref[:, sl] = (p * scale).astype(BF)
        else:
            k_ref[:, (i - ATT_HEADS) * LANE:(i - ATT_HEADS + 1) * LANE] = p.astype(BF)
    vt = _dot_nt(wvt_ref[...], h)
    for hk in range(ATT_KV_HEADS):
        vt_ref[hk, 0:HEAD_DIM, :] = vt[hk * LANE:(hk + 1) * LANE].astype(BF)
        vt_ref[hk, HEAD_DIM:, :] = jnp.ones((ATT_ONES_ROWS, vt.shape[1]), BF)


def _proj_att(x, mod, w_qk, w_vt, qg, kg, cos, sin):
    b, l, d = x.shape
    tm = _row_tile(l, 512)
    qw, kw = ATT_HEADS * LANE, ATT_KV_HEADS * LANE
    vt_rows = HEAD_DIM + ATT_ONES_ROWS
    row = lambda width: pl.BlockSpec((None, tm, width), lambda i, t: (i, t, 0))
    const = lambda a: pl.BlockSpec(a.shape, lambda i, t: (0, 0))
    tab = pl.BlockSpec((tm, LANE), lambda i, t: (t, 0))
    return pl.pallas_call(
        _proj_att_body,
        grid=(b, l // tm),
        in_specs=[row(d), pl.BlockSpec((None, 6, d), lambda i, t: (i, 0, 0)),
                  const(w_qk), const(w_vt), const(qg), const(kg), tab, tab],
        out_specs=[row(qw), row(kw),
                   pl.BlockSpec((None, ATT_KV_HEADS, vt_rows, tm), lambda i, t: (i, 0, 0, t))],
        out_shape=[jax.ShapeDtypeStruct((b, l, qw), BF), jax.ShapeDtypeStruct((b, l, kw), BF),
                   jax.ShapeDtypeStruct((b, ATT_KV_HEADS, vt_rows, l), BF)],
        compiler_params=_cparams("parallel", "parallel"),
        name="proj_att",
    )(x, mod, w_qk, w_vt, qg, kg, cos, sin)


@functools.lru_cache(maxsize=None)
def _rope_tables(l):
    t = np.arange(l)
    row = (t // GRID_W).astype(np.float64)
    col = (t % GRID_W).astype(np.float64)
    n_freq = HEAD_DIM // 4
    freqs = ROPE_THETA ** (-np.arange(n_freq, dtype=np.float64) / n_freq)
    ang = np.concatenate([row[:, None] * freqs, col[:, None] * freqs], axis=-1)
    cos, sin = np.cos(ang), np.sin(ang)
    return (np.concatenate([cos, cos], axis=-1).astype(np.float32),
            np.concatenate([-sin, sin], axis=-1).astype(np.float32))


ATT_KV_BLOCK = 256
ATT_ONES_ROWS = 16


def _attn_body(n_src, q_ref, *refs):
    o_ref = refs[2 * n_src]
    tq = q_ref.shape[0]
    n = ATT_GROUP * tq
    vt_rows = refs[1].shape[0]
    q = q_ref[...]
    qs = jnp.concatenate([q[:, g * LANE:(g + 1) * LANE] for g in range(ATT_GROUP)], axis=0)
    blocks = []
    for i in range(n_src):
        k_ref, vt_ref = refs[2 * i], refs[2 * i + 1]
        lk = k_ref.shape[0]
        bk = min(ATT_KV_BLOCK, lk)
        blocks += [(k_ref, vt_ref, j * bk, bk) for j in range(lk // bk)]

    def scores(blk):
        k_ref, _, off, bk = blk
        return _dot_nt(k_ref[off:off + bk, :], qs)

    m = jnp.full((1, n), -1e30, F32)
    acc = jnp.zeros((vt_rows, n), F32)
    def accumulate(acc, blk, alpha, p):
        _, vt_ref, off, bk = blk
        return alpha * acc + _dot(vt_ref[:, off:off + bk], p)

    nb = len(blocks)
    ahead = [scores(blocks[j]) for j in range(min(2, nb))]
    pending = None
    for j in range(nb):
        s = ahead.pop(0)
        if j + 2 < nb:
            ahead.append(scores(blocks[j + 2]))
        m_new = jnp.maximum(m, jnp.max(s, axis=0, keepdims=True))
        alpha = jnp.exp2(m - m_new)
        p = jnp.exp2((s - m_new).astype(BF))
        m = m_new
        if pending is not None:
            acc = accumulate(acc, *pending)
        pending = (blocks[j], alpha, p)
    acc = accumulate(acc, *pending)

    o = acc[:HEAD_DIM] * (1.0 / acc[HEAD_DIM:HEAD_DIM + 1])
    for g in range(ATT_GROUP):
        o_ref[:, g * LANE:(g + 1) * LANE] = o[:, g * tq:(g + 1) * tq].T.astype(BF)


def _attention(q, kvs, tq):
    b, lq, qw = q.shape
    gw = ATT_GROUP * LANE
    in_specs = [pl.BlockSpec((None, tq, gw), lambda i, h, t: (i, t, h))]
    args = [q]
    for k, vt in kvs:
        lk = k.shape[1]
        in_specs.append(pl.BlockSpec((None, lk, LANE), lambda i, h, t: (i, 0, h)))
        in_specs.append(pl.BlockSpec((None, None, vt.shape[2], lk), lambda i, h, t: (i, h, 0, 0)))
        args += [k, vt]
    return pl.pallas_call(
        functools.partial(_attn_body, len(kvs)),
        grid=(b, ATT_KV_HEADS, lq // tq),
        in_specs=in_specs,
        out_specs=pl.BlockSpec((None, tq, gw), lambda i, h, t: (i, t, h)),
        out_shape=jax.ShapeDtypeStruct((b, lq, qw), BF),
        compiler_params=_cparams("parallel", "parallel", "parallel"),
        name="attention",
    )(*args)


def _out_att_body(o_ref, x_ref, mod_ref, w_ref, y_ref):
    y_ref[...] = x_ref[...] + mod_ref[2:3, :] * _dot(o_ref[...], w_ref[...])


def _out_att(o, x, mod, w_out):
    b, l, d = x.shape
    tm = _row_tile(l, 512)
    row = lambda width: pl.BlockSpec((None, tm, width), lambda i, t: (i, t, 0))
    return pl.pallas_call(
        _out_att_body,
        grid=(b, l // tm),
        in_specs=[row(o.shape[-1]), row(d), pl.BlockSpec((None, 6, d), lambda i, t: (i, 0, 0)),
                  pl.BlockSpec(w_out.shape, lambda i, t: (0, 0))],
        out_specs=row(d),
        out_shape=jax.ShapeDtypeStruct((b, l, d), F32),
        compiler_params=_cparams("parallel", "parallel"),
        name="out_att",
    )(o, x, mod, w_out)


def _ffn_body(on_grid, ck, has_final, *refs):
    if on_grid:
        xp_ref, xm_ref, xn_ref = refs[:3]
        refs = refs[3:]
    else:
        xm_ref = refs[0]
        refs = refs[1:]
    mod_ref, wu_ref, cw_ref, cb_ref, wd_ref = refs[:5]
    fg_ref = refs[5] if has_final else None
    o_ref, h_buf = refs[-2:]
    tm = xm_ref.shape[0]
    d_ff = wd_ref.shape[0]
    n_chunk = d_ff // ck
    cols = lambda c: slice(c * ck, (c + 1) * ck)
    halo = GRID_W if on_grid else 0
    rows = tm + 2 * halo
    mod = mod_ref[...]
    md = lambda x: _modulate(x, mod[3:4], mod[4:5])
    if on_grid:
        h_buf[0:halo, :] = md(xp_ref[...])
        h_buf[halo:halo + tm, :] = md(xm_ref[...])
        h_buf[halo + tm:rows, :] = md(xn_ref[...])
    else:
        h_buf[...] = md(xm_ref[...])
    r = lax.broadcasted_iota(jnp.int32, (rows, ck), 0)
    if on_grid:
        t = pl.program_id(1)
        nt = pl.num_programs(1)
        col = r & (GRID_W - 1)
        has_left = col != 0
        has_right = col != GRID_W - 1
        inside = jnp.logical_and(jnp.logical_or(r >= halo, t > 0),
                                 jnp.logical_or(r < halo + tm, t < nt - 1))
    else:
        has_left = r != 0
        has_right = r != rows - 1

    hm_sl = slice(halo, halo + tm)

    def up(c):
        gate = _dot(h_buf[...], wu_ref[:, cols(c)])
        val = _dot(h_buf[hm_sl, :], wu_ref[:, d_ff + c * ck:d_ff + (c + 1) * ck])
        return gate, val

    def down(c, gate, val):
        if on_grid:
            gate = jnp.where(inside, gate, 0.0)
        gl = jnp.where(has_left, pltpu.roll(gate, 1, axis=0), 0.0)
        gr = jnp.where(has_right, pltpu.roll(gate, rows - 1, axis=0), 0.0)
        cw = cw_ref[:, cols(c)]
        if on_grid:
            conv = None
            for dy in range(3):
                sl = slice(dy * halo, dy * halo + tm)
                term = (cw[3 * dy:3 * dy + 1] * gl[sl] + cw[3 * dy + 1:3 * dy + 2] * gate[sl]
                        + cw[3 * dy + 2:3 * dy + 3] * gr[sl])
                conv = term if conv is None else conv + term
        else:
            conv = cw[3:4] * gl + cw[4:5] * gate + cw[5:6] * gr
        return (_silu(conv + cb_ref[:, cols(c)]) * val).astype(BF)

    nxt = up(0)
    acc = None
    acts = []
    for c in range(n_chunk):
        cur = nxt
        if c + 1 < n_chunk:
            nxt = up(c + 1)
        acts.append(down(c, *cur))
        if len(acts) == FFN_DOWN_GROUP or c + 1 == n_chunk:
            c0 = c + 1 - len(acts)
            part = _dot(jnp.concatenate(acts, axis=1), wd_ref[c0 * ck:(c + 1) * ck, :])
            acc = part if acc is None else acc + part
            acts = []
    y = xm_ref[...] + mod[5:6] * acc
    if has_final:
        y = _rms(y) * fg_ref[...]
    o_ref[...] = y


def _ffn(x, mod, wu, cw, cb, wd, on_grid, final_g=None):
    b, l, d = x.shape
    ck = _ffn_chunk(wd.shape[0])
    extra = [] if final_g is None else [final_g.reshape(1, d)]
    const = lambda a: pl.BlockSpec(a.shape, lambda i, t: (0,) * a.ndim, pipeline_mode=pl.Buffered(1))
    modspec = pl.BlockSpec((None, 6, d), lambda i, t: (i, 0, 0))
    if on_grid:
        tm = FFN_TILE if l % FFN_TILE == 0 else 512
        per = tm // GRID_W
        n_rows = l // GRID_W
        x_specs = [pl.BlockSpec((None, GRID_W, d), lambda i, t: (i, jnp.maximum(t * per - 1, 0), 0)),
                   pl.BlockSpec((None, tm, d), lambda i, t: (i, t, 0)),
                   pl.BlockSpec((None, GRID_W, d), lambda i, t: (i, jnp.minimum((t + 1) * per, n_rows - 1), 0))]
        xs = [x, x, x]
        rows = tm + 2 * GRID_W
    else:
        tm = l
        x_specs = [pl.BlockSpec((None, tm, d), lambda i, t: (i, t, 0))]
        xs = [x]
        rows = tm
    return pl.pallas_call(
        functools.partial(_ffn_body, on_grid, ck, final_g is not None),
        grid=(b, l // tm),
        in_specs=(x_specs + [modspec, const(wu), const(cw), const(cb), const(wd)]
                  + [const(a) for a in extra]),
        out_specs=pl.BlockSpec((None, tm, d), lambda i, t: (i, t, 0)),
        out_shape=jax.ShapeDtypeStruct((b, l, d), F32),
        scratch_shapes=[pltpu.VMEM((rows, d), BF)],
        compiler_params=_cparams("parallel", "parallel"),
        name="ffn_grid" if on_grid else "ffn_seq",
    )(*xs, mod, wu, cw, cb, wd, *extra)


def _ffn_chunk(d_ff):
    for ck in (256, 128):
        if d_ff % ck == 0:
            return ck
    return d_ff


def _even_mixer(x, ctx, ml, mc, w_in, w_out, lb_logits, gn, e, need_ctx):
    w_in = w_in.astype(BF)
    w_out = w_out.astype(BF)
    gn = gn.reshape(1, -1)
    lbl = lb_logits.reshape(lb_logits.shape[0], -1).astype(F32)
    a_l, q_l, lf_l, k_l, v_l, g_l = _proj_even(x, ml, w_in, lbl, e)
    a_c, q_c, lf_c, k_c, v_c, g_c = _proj_even(ctx, mc, w_in, lbl, e)
    pc, pl_ = (q_c, lf_c, k_c, v_c), (q_l, lf_l, k_l, v_l)
    (of_c, of_l), (ob_c, ob_l) = _scan(pc, pl_)
    x = _out_even(_dft_mix(a_l), of_l, ob_l, g_l, x, ml, gn, w_out)
    if need_ctx:
        ctx = _out_even(_dft_mix(a_c), of_c, ob_c, g_c, ctx, mc, gn, w_out)
    return x, ctx


def _att_mixer(x, ctx, ml, mc, w_qkv, qn_g, kn_g, w_out, need_ctx):
    l, lc = x.shape[1], ctx.shape[1]
    perm = np.concatenate([np.arange(0, HEAD_DIM, 2), np.arange(1, HEAD_DIM, 2)])
    n_qk = ATT_HEADS + ATT_KV_HEADS
    cols = np.concatenate([h * HEAD_DIM + perm for h in range(n_qk)])
    w_qk = w_qkv[:, cols].astype(BF)
    w_vt = w_qkv[:, n_qk * HEAD_DIM:].T.astype(BF)
    w_out = w_out.astype(BF)
    qg = qn_g[perm].reshape(1, -1)
    kg = kn_g[perm].reshape(1, -1)
    cos_l, sin_l = (jnp.asarray(t) for t in _rope_tables(l))
    cos_c, sin_c = jnp.ones((lc, HEAD_DIM), F32), jnp.zeros((lc, HEAD_DIM), F32)
    q_l, k_l, vt_l = _proj_att(x, ml, w_qk, w_vt, qg, kg, cos_l, sin_l)
    q_c, k_c, vt_c = _proj_att(ctx, mc, w_qk, w_vt, qg, kg, cos_c, sin_c)
    x = _out_att(_attention(q_l, [(k_l, vt_l), (k_c, vt_c)], min(l, 256)), x, ml, w_out)
    if need_ctx:
        ctx = _out_att(_attention(q_c, [(k_c, vt_c)], lc), ctx, mc, w_out)
    return x, ctx


def _ffn_layer(x, ctx, ml, mc, w_up, conv_w, conv_b, w_down, need_ctx, final_g=None):
    d = x.shape[-1]
    d_ff = w_down.shape[0]
    wu = w_up.astype(BF)
    cw = conv_w.reshape(9, d_ff)
    cb = conv_b.reshape(1, d_ff)
    wd = w_down.astype(BF)
    x = _ffn(x, ml, wu, cw, cb, wd, True, final_g)
    if need_ctx:
        ctx = _ffn(ctx, mc, wu, cw, cb, wd, False)
    return x, ctx


def _mod_vectors(c, c_ctx, w_mod, b_mod):
    b, d = c.shape
    depth = w_mod.shape[0]
    pad = (-(b + 1)) % 8
    cc = jnp.concatenate([c, c_ctx[None, :], jnp.zeros((pad, d), F32)], axis=0)
    mods = _modulation(cc, w_mod, b_mod)
    mod_lat = mods[:, :b].reshape(depth, b, 6, d)
    mod_ctx = jnp.broadcast_to(mods[:, b].reshape(depth, 1, 6, d), (depth, b, 6, d))
    return mod_lat, mod_ctx


def kernel(x, c, ctx, c_ctx, w_mod, b_mod, w_in_ab, w_out_ab, hg_lb_logits, hg_norm_g, w_qkv, q_norm_g,
           k_norm_g, w_out_att, w_up, conv_w, conv_b, w_down, final_norm_g):
    depth = w_mod.shape[0]
    mod_lat, mod_ctx = _mod_vectors(c, c_ctx, w_mod, b_mod)
    for layer in range(depth):
        need_ctx = layer != depth - 1
        ml, mc = mod_lat[layer], mod_ctx[layer]
        if layer % 2 == 0:
            e = layer // 2
            x, ctx = _even_mixer(x, ctx, ml, mc, w_in_ab[e], w_out_ab[e], hg_lb_logits, hg_norm_g[e], e, need_ctx)
        else:
            o = layer // 2
            x, ctx = _att_mixer(x, ctx, ml, mc, w_qkv[o], q_norm_g[o], k_norm_g[o], w_out_att[o], need_ctx)
        x, ctx = _ffn_layer(x, ctx, ml, mc, w_up[layer], conv_w[layer], conv_b[layer], w_down[layer], need_ctx,
                            None if need_ctx else final_norm_g)
    return x
```

```python
import functools
import math

import numpy as np
import jax
import jax.numpy as jnp
from jax import lax
from jax.experimental import pallas as pl
from jax.experimental.pallas import tpu as pltpu

F32 = jnp.float32
BF = jnp.bfloat16

EPS = 1e-6
GRID_W = 64
LANE = 128
SUBLANES = 8
HEAD_DIM = 128
FNET_GROUPS = 4
HG_HEADS = 4
HG_W = HG_HEADS * HEAD_DIM
ATT_HEADS = 8
ATT_KV_HEADS = 2
ATT_GROUP = ATT_HEADS // ATT_KV_HEADS
ROPE_THETA = 10000.0
SCAN_CHUNK = 128
FFN_TILE = 512
FFN_DOWN_GROUP = 6
VMEM_LIMIT = 56 * 1024 * 1024


def _cparams(*sem):
    return pltpu.CompilerParams(dimension_semantics=sem, vmem_limit_bytes=VMEM_LIMIT)


def _dot(a, b):
    return jnp.dot(a, b, preferred_element_type=F32)


def _dot_nt(a, b):
    return lax.dot_general(a, b, (((1,), (1,)), ((), ())), preferred_element_type=F32)


def _dot_tn(a, b):
    return lax.dot_general(a, b, (((0,), (0,)), ((), ())), preferred_element_type=F32)


def _rms(x):
    return x * lax.rsqrt(jnp.mean(x * x, axis=-1, keepdims=True) + EPS)


def _silu(x):
    return x * jax.nn.sigmoid(x)


def _modulate(x, shift, scale):
    return (_rms(x) * (1.0 + scale) + shift).astype(BF)


def _row_tile(n, pref):
    return pref if n % pref == 0 else n


def _mod_body(c_ref, w_ref, b_ref, o_ref):
    s = _silu(c_ref[...]).astype(BF)
    o_ref[...] = _dot(s, w_ref[...].astype(BF)) + b_ref[...]


def _modulation(cc, w_mod, b_mod):
    depth, d, six_d = w_mod.shape
    r = cc.shape[0]
    return pl.pallas_call(
        _mod_body,
        grid=(depth, six_d // d),
        in_specs=[pl.BlockSpec((r, d), lambda l, j: (0, 0)),
                  pl.BlockSpec((None, d, d), lambda l, j: (l, 0, j)),
                  pl.BlockSpec((None, 1, d), lambda l, j: (l, 0, j))],
        out_specs=pl.BlockSpec((None, r, d), lambda l, j: (l, 0, j)),
        out_shape=jax.ShapeDtypeStruct((depth, r, six_d), F32),
        compiler_params=_cparams("parallel", "parallel"),
        name="modulation",
    )(cc, w_mod, b_mod.reshape(depth, 1, six_d))


def _proj_even_body(e, n_even, x_ref, mod_ref, w_ref, lbl_ref, a_ref, q_ref, lf_ref, k_ref, v_ref, g_ref):
    mod = mod_ref[...]
    h = _modulate(x_ref[...], mod[0:1], mod[1:2])
    rows = [lbl_ref[i:i + 1, :] for i in range(n_even)]
    mx = functools.reduce(jnp.maximum, rows)
    ex = [jnp.exp(r - mx) for r in rows]
    den = functools.reduce(lambda a, b: a + b, ex)
    lb = jnp.zeros_like(mx)
    for i in range(1, e + 1):
        lb = lb + ex[i] / den
    w = HG_W
    a_ref[...] = _dot(h, w_ref[:, 0:w]).astype(BF)
    q_ref[...] = _silu(_dot(h, w_ref[:, w:2 * w])).astype(BF)
    for d in range(2):
        z = _dot(h, w_ref[:, (2 + d) * w:(3 + d) * w])
        lbd = lb[:, d * w:(d + 1) * w]
        f = lbd + (1.0 - lbd) * jax.nn.sigmoid(z)
        lf_ref[:, d * w:(d + 1) * w] = jnp.log2(f)
        k_ref[:, d * w:(d + 1) * w] = (1.0 - f).astype(BF)
    v_ref[...] = _dot(h, w_ref[:, 4 * w:5 * w]).astype(BF)
    g_ref[...] = _dot(h, w_ref[:, 5 * w:6 * w]).astype(BF)


def _proj_even(x, mod, w_in, lb_logits, e):
    b, l, d = x.shape
    n_even = lb_logits.shape[0]
    tm = _row_tile(l, 512)
    w = HG_W
    row = lambda width: pl.BlockSpec((None, tm, width), lambda i, t: (i, t, 0))
    shp = lambda width, dt: jax.ShapeDtypeStruct((b, l, width), dt)
    return pl.pallas_call(
        functools.partial(_proj_even_body, e, n_even),
        grid=(b, l // tm),
        in_specs=[row(d),
                  pl.BlockSpec((None, 6, d), lambda i, t: (i, 0, 0)),
                  pl.BlockSpec(w_in.shape, lambda i, t: (0, 0)),
                  pl.BlockSpec(lb_logits.shape, lambda i, t: (0, 0))],
        out_specs=[row(w), row(w), row(2 * w), row(2 * w), row(w), row(w)],
        out_shape=[shp(w, BF), shp(w, BF), shp(2 * w, F32), shp(2 * w, BF), shp(w, BF), shp(w, BF)],
        compiler_params=_cparams("parallel", "parallel"),
        name="proj_even",
    )(x, mod, w_in, lb_logits)


@functools.lru_cache(maxsize=None)
def _dft_tables(n):
    idx = np.arange(n, dtype=np.int64)
    ang = 2.0 * np.pi * ((idx[:, None] * idx[None, :]) % n).astype(np.float64) / n
    s = 1.0 / math.sqrt(n)
    return (np.cos(ang) * s).astype(np.float32), (np.sin(ang) * s).astype(np.float32)


def _dft_body(c_ref, s_ref, a_ref, cs_ref, o_ref):
    a = a_ref[...]
    p = _dot(c_ref[...], a).astype(BF)
    q = _dot(s_ref[...], a).astype(BF)
    cs = cs_ref[...]
    for g in range(FNET_GROUPS):
        sl = slice(g * LANE, (g + 1) * LANE)
        pq = jnp.concatenate([p[:, sl], q[:, sl]], axis=1)
        o_ref[:, sl] = _dot(pq, cs).astype(BF)


def _dft_mix(a):
    b, l, w = a.shape
    cl, sl_ = _dft_tables(l)
    cc, sc = _dft_tables(LANE)
    cos_l = jnp.asarray(cl).astype(BF)
    sin_l = jnp.asarray(sl_).astype(BF)
    cs = jnp.concatenate([jnp.asarray(cc), -jnp.asarray(sc)], axis=0).astype(BF)
    tm = _row_tile(l, 512)
    return pl.pallas_call(
        _dft_body,
        grid=(l // tm, b),
        in_specs=[pl.BlockSpec((tm, l), lambda m, i: (m, 0)),
                  pl.BlockSpec((tm, l), lambda m, i: (m, 0)),
                  pl.BlockSpec((None, l, w), lambda m, i: (i, 0, 0)),
                  pl.BlockSpec(cs.shape, lambda m, i: (0, 0))],
        out_specs=pl.BlockSpec((None, tm, w), lambda m, i: (i, m, 0)),
        out_shape=jax.ShapeDtypeStruct((b, l, w), BF),
        compiler_params=_cparams("parallel", "parallel"),
        name="dft_mix",
    )(cos_l, sin_l, a, cs)


DFT_TILE = 512
DFT_EXTRA = 16


@functools.lru_cache(maxsize=None)
def _dft_half_tables(l):
    cl, sl_ = _dft_tables(l)
    t, e = DFT_TILE, DFT_EXTRA
    n = l // (2 * t)
    cos = np.stack([cl[j * t:j * t + t + e] for j in range(n)])
    sin = np.stack([sl_[j * t:j * t + t + e] for j in range(n)])
    rev = np.zeros((t, t + e), np.float32)
    rev[np.arange(t), t - np.arange(t)] = 1.0
    return cos, sin, rev


def _dft_half_body(c_ref, s_ref, a_ref, cs_ref, rev_ref, lo_ref, hi_ref):
    t = lo_ref.shape[0]
    a = a_ref[...]
    p = _dot(c_ref[...], a).astype(BF)
    q = _dot(s_ref[...], a).astype(BF)
    cs = cs_ref[...]
    rev = rev_ref[...]
    for g in range(FNET_GROUPS):
        sl = slice(g * LANE, (g + 1) * LANE)
        y = _dot(jnp.concatenate([p[:, sl], q[:, sl]], axis=1), cs)
        lo_ref[:, sl] = y[:t, :LANE].astype(BF)
        hi_ref[:, sl] = _dot(rev, y[:, LANE:].astype(BF)).astype(BF)


def _dft_mix_half(a):
    b, l, w = a.shape
    t, e = DFT_TILE, DFT_EXTRA
    n = l // (2 * t)
    cos, sin, rev = (jnp.asarray(x).astype(BF) for x in _dft_half_tables(l))
    cc, sc = (jnp.asarray(x) for x in _dft_tables(LANE))
    cs = jnp.concatenate([jnp.concatenate([cc, -sc], axis=0), jnp.concatenate([cc, sc], axis=0)], axis=1).astype(BF)
    half = jax.ShapeDtypeStruct((b, l // 2, w), BF)
    lo, hi = pl.pallas_call(
        _dft_half_body,
        grid=(n, b),
        in_specs=[pl.BlockSpec((None, t + e, l), lambda j, i: (j, 0, 0)),
                  pl.BlockSpec((None, t + e, l), lambda j, i: (j, 0, 0)),
                  pl.BlockSpec((None, l, w), lambda j, i: (i, 0, 0)),
                  pl.BlockSpec(cs.shape, lambda j, i: (0, 0)),
                  pl.BlockSpec(rev.shape, lambda j, i: (0, 0))],
        out_specs=[pl.BlockSpec((None, t, w), lambda j, i: (i, j, 0)),
                   pl.BlockSpec((None, t, w), lambda j, i: (i, n - 1 - j, 0))],
        out_shape=[half, half],
        compiler_params=_cparams("parallel", "parallel"),
        name="dft_mix_half",
    )(cos, sin, a, cs, rev)
    return lo, hi


def _scan_chunk(rev, q_ref, k_ref, lf_ref, v_ref, o_ref, st_ref):
    c, w = lf_ref.shape
    nlev = c.bit_length() - 1
    lf = lf_ref[...]
    r = lax.broadcasted_iota(jnp.int32, (c, w), 0)
    pos = (c - 1 - r) if rev else r

    def tile_roll(x, sh):
        return pltpu.roll(x.reshape(c // SUBLANES, SUBLANES, w), sh, axis=1).reshape(c, w)

    def from_prev(x, sh):
        return tile_roll(x, (SUBLANES - sh) if rev else sh)

    def from_next(x, sh):
        return tile_roll(x, sh if rev else (SUBLANES - sh))

    def shift_tiles(x, sh):
        zero = jnp.zeros((sh, w), F32)
        return jnp.concatenate([x[sh:], zero], axis=0) if rev else jnp.concatenate([zero, x[:c - sh]], axis=0)

    def neg_abs(x):
        return pltpu.bitcast(pltpu.bitcast(x, jnp.uint32) | jnp.uint32(0x80000000), F32)

    def spread_half(x, sh, low):
        parts = []
        for i in range(c // (2 * sh)):
            first, second = x[2 * i * sh:(2 * i + 1) * sh], x[(2 * i + 1) * sh:(2 * i + 2) * sh]
            src = (second if rev else first) if low else (first if rev else second)
            parts += [src, src]
        return jnp.concatenate(parts, axis=0)

    pos8 = pos & (SUBLANES - 1)
    b = lf
    sh = 1
    while sh < SUBLANES:
        b = b + jnp.where(pos8 >= sh, from_prev(b, sh), 0.0)
        sh *= 2
    last = 0 if rev else SUBLANES - 1
    b3 = b.reshape(c // SUBLANES, SUBLANES, w)
    tot = jnp.broadcast_to(b3[:, last:last + 1, :], b3.shape).reshape(c, w)
    while sh < c:
        tot = tot + shift_tiles(tot, sh)
        sh *= 2
    b = b + shift_tiles(tot, SUBLANES)

    q = q_ref[...]
    k = k_ref[...]
    ti = lax.broadcasted_iota(jnp.int32, (c, c), 0)
    si = lax.broadcasted_iota(jnp.int32, (c, c), 1)
    later = (ti < si) if rev else (ti > si)
    diff = ti ^ si
    att = [jnp.zeros((c, c), F32) for _ in range(HG_HEADS)]
    y = b
    for j in range(nlev):
        sh = 1 << j
        upper = ((pos >> j) & 1) == 1
        bound = spread_half(y, sh, True) if sh % 8 == 0 else jnp.where(upper, from_prev(y, sh), y)
        e = jnp.exp2(neg_abs(b - bound)).astype(BF)
        qm = q * e
        km = k * e
        level = jnp.logical_and((diff >> j) == 1, later)
        for h in range(HG_HEADS):
            sl = slice(h * LANE, (h + 1) * LANE)
            att[h] = att[h] + jnp.where(level, _dot_nt(qm[:, sl], km[:, sl]), 0.0)
        y = spread_half(y, sh, False) if sh % 8 == 0 else jnp.where(upper, y, from_next(y, sh))
    b_last = y
    qe = q * jnp.exp2(b).astype(BF)
    kd = k * jnp.exp2(b_last - b).astype(BF)
    carry = jnp.exp2(b_last[0:1, :])
    v = v_ref[...]
    vf = v.astype(F32)
    for h in range(HG_HEADS):
        sl = slice(h * LANE, (h + 1) * LANE)
        st = st_ref[h]
        inter = _dot_nt(qe[:, sl], st.astype(BF))
        diag = jnp.sum(q[:, sl].astype(F32) * k[:, sl].astype(F32), axis=-1, keepdims=True)
        o_ref[:, sl] = (inter + _dot(att[h].astype(BF), v[:, sl]) + diag * vf[:, sl]).astype(o_ref.dtype)
        st_ref[h] = st * carry[:, sl] + _dot_tn(v[:, sl], kd[:, sl])


def _scan_body(n_ctx, *refs):
    ins, outs, states = refs[:16], refs[16:20], refs[20:]
    s = pl.program_id(1)

    @pl.when(s == 0)
    def _():
        for st_ref in states:
            st_ref[...] = jnp.zeros_like(st_ref)

    def run(stream):
        for d in range(2):
            base = 8 * d + 4 * stream
            _scan_chunk(d == 1, *ins[base:base + 4], outs[2 * d + stream], states[d])

    @pl.when(s < n_ctx)
    def _():
        run(0)

    @pl.when(s >= n_ctx)
    def _():
        run(1)


def _scan(pc, pl_):
    qc, lfc, kc, vc = pc
    ql, lfl, kl, vl = pl_
    b, lc, w = qc.shape
    ll = ql.shape[1]
    c = SCAN_CHUNK
    n_ctx, n_lat = lc // c, ll // c

    def spec(idx, col):
        return pl.BlockSpec((None, c, w), lambda i, s: (i, idx(s), col))

    in_specs, out_specs, args = [], [], []
    for d in range(2):
        if d:
            ci = lambda s: jnp.maximum(n_ctx - 1 - s, 0)
            li = lambda s: n_lat - 1 - jnp.maximum(s - n_ctx, 0)
        else:
            ci = lambda s: jnp.minimum(s, n_ctx - 1)
            li = lambda s: jnp.maximum(s - n_ctx, 0)
        in_specs += [spec(ci, 0), spec(ci, d), spec(ci, d), spec(ci, 0),
                     spec(li, 0), spec(li, d), spec(li, d), spec(li, 0)]
        args += [qc, kc, lfc, vc, ql, kl, lfl, vl]
        out_specs += [spec(ci, 0), spec(li, 0)]
    shp = lambda l: jax.ShapeDtypeStruct((b, l, w), BF)
    of_c, of_l, ob_c, ob_l = pl.pallas_call(
        functools.partial(_scan_body, n_ctx),
        grid=(b, n_ctx + n_lat),
        in_specs=in_specs,
        out_specs=out_specs,
        out_shape=[shp(lc), shp(ll), shp(lc), shp(ll)],
        scratch_shapes=[pltpu.VMEM((HG_HEADS, HEAD_DIM, HEAD_DIM), F32) for _ in range(2)],
        compiler_params=_cparams("parallel", "arbitrary"),
        name="hgrn_scan",
    )(*args)
    return (of_c, of_l), (ob_c, ob_l)


def _out_even_body(n_lo, *refs):
    yf_refs, (of_ref, ob_ref, g_ref, x_ref, mod_ref, gn_ref, w_ref, o_ref) = refs[:-8], refs[-8:]
    if n_lo:
        yf = jnp.where(pl.program_id(1) < n_lo, yf_refs[0][...], yf_refs[1][...])
    else:
        yf = yf_refs[0][...]
    o = of_ref[...].astype(F32) + ob_ref[...].astype(F32)
    g = g_ref[...].astype(F32)
    gn = gn_ref[...]
    parts = []
    for h in range(HG_HEADS):
        sl = slice(h * LANE, (h + 1) * LANE)
        parts.append((_rms(o[:, sl]) * gn[:, sl] * _silu(g[:, sl])).astype(BF))
    gated = jnp.concatenate(parts, axis=1)
    w = HG_W
    y = _dot(yf, w_ref[0:w, :]) + _dot(gated, w_ref[w:2 * w, :])
    o_ref[...] = x_ref[...] + mod_ref[2:3, :] * y


def _out_even(yf, o_f, o_b, g, x, mod, gn, w_out):
    b, l, d = x.shape
    tm = _row_tile(l, 512)
    row = lambda width: pl.BlockSpec((None, tm, width), lambda i, t: (i, t, 0))
    if isinstance(yf, tuple):
        n_lo = yf[0].shape[1] // tm
        yf_specs = [pl.BlockSpec((None, tm, HG_W), lambda i, t: (i, jnp.minimum(t, n_lo - 1), 0)),
                    pl.BlockSpec((None, tm, HG_W), lambda i, t: (i, jnp.maximum(t - n_lo, 0), 0))]
    else:
        n_lo, yf, yf_specs = 0, (yf,), [row(HG_W)]
    return pl.pallas_call(
        functools.partial(_out_even_body, n_lo),
        grid=(b, l // tm),
        in_specs=yf_specs + [row(HG_W), row(HG_W), row(HG_W), row(d),
                  pl.BlockSpec((None, 6, d), lambda i, t: (i, 0, 0)),
                  pl.BlockSpec(gn.shape, lambda i, t: (0, 0)),
                  pl.BlockSpec(w_out.shape, lambda i, t: (0, 0))],
        out_specs=row(d),
        out_shape=jax.ShapeDtypeStruct((b, l, d), F32),
        compiler_params=_cparams("parallel", "parallel"),
        name="out_even",
    )(*yf, o_f, o_b, g, x, mod, gn, w_out)


def _proj_att_body(x_ref, mod_ref, w_ref, wvt_ref, qg_ref, kg_ref, cos_ref, sin_ref, q_ref, k_ref, vt_ref):
    mod = mod_ref[...]
    h = _modulate(x_ref[...], mod[0:1], mod[1:2])
    cos = cos_ref[...]
    sin = sin_ref[...]
    scale = HEAD_DIM ** -0.5 * math.log2(math.e)
    for i in range(ATT_HEADS + ATT_KV_HEADS):
        sl = slice(i * LANE, (i + 1) * LANE)
        p = _dot(h, w_ref[:, sl])
        gain = qg_ref[...] if i < ATT_HEADS else kg_ref[...]
        p = _rms(p) * gain
        p = p * cos + pltpu.roll(p, HEAD_DIM // 2, axis=1) * sin
        if i < ATT_HEADS:
            q_ref[:, sl] = (p * scale).astype(BF)
        else:
            k_ref[:, (i - ATT_HEADS) * LANE:(i - ATT_HEADS + 1) * LANE] = p.astype(BF)
    vt = _dot_nt(wvt_ref[...], h)
    for hk in range(ATT_KV_HEADS):
        vt_ref[hk, 0:HEAD_DIM, :] = vt[hk * LANE:(hk + 1) * LANE].astype(BF)
        vt_ref[hk, HEAD_DIM:, :] = jnp.ones((ATT_ONES_ROWS, vt.shape[1]), BF)


def _proj_att(x, mod, w_qk, w_vt, qg, kg, cos, sin):
    b, l, d = x.shape
    tm = _row_tile(l, 512)
    qw, kw = ATT_HEADS * LANE, ATT_KV_HEADS * LANE
    vt_rows = HEAD_DIM + ATT_ONES_ROWS
    row = lambda width: pl.BlockSpec((None, tm, width), lambda i, t: (i, t, 0))
    const = lambda a: pl.BlockSpec(a.shape, lambda i, t: (0, 0))
    tab = pl.BlockSpec((tm, LANE), lambda i, t: (t, 0))
    return pl.pallas_call(
        _proj_att_body,
        grid=(b, l // tm),
        in_specs=[row(d), pl.BlockSpec((None, 6, d), lambda i, t: (i, 0, 0)),
                  const(w_qk), const(w_vt), const(qg), const(kg), tab, tab],
        out_specs=[row(qw), row(kw),
                   pl.BlockSpec((None, ATT_KV_HEADS, vt_rows, tm), lambda i, t: (i, 0, 0, t))],
        out_shape=[jax.ShapeDtypeStruct((b, l, qw), BF), jax.ShapeDtypeStruct((b, l, kw), BF),
                   jax.ShapeDtypeStruct((b, ATT_KV_HEADS, vt_rows, l), BF)],
        compiler_params=_cparams("parallel", "parallel"),
        name="proj_att",
    )(x, mod, w_qk, w_vt, qg, kg, cos, sin)


@functools.lru_cache(maxsize=None)
def _rope_tables(l):
    t = np.arange(l)
    row = (t // GRID_W).astype(np.float64)
    col = (t % GRID_W).astype(np.float64)
    n_freq = HEAD_DIM // 4
    freqs = ROPE_THETA ** (-np.arange(n_freq, dtype=np.float64) / n_freq)
    ang = np.concatenate([row[:, None] * freqs, col[:, None] * freqs], axis=-1)
    cos, sin = np.cos(ang), np.sin(ang)
    return (np.concatenate([cos, cos], axis=-1).astype(np.float32),
            np.concatenate([-sin, sin], axis=-1).astype(np.float32))


ATT_KV_BLOCK = 256
ATT_ONES_ROWS = 16
ATT_Q_TILE = 256


def _attn_body(n_src, q_ref, *refs):
    o_ref = refs[2 * n_src]
    tq = q_ref.shape[0]
    n = ATT_GROUP * tq
    vt_rows = refs[1].shape[0]
    q = q_ref[...]
    qs = jnp.concatenate([q[:, g * LANE:(g + 1) * LANE] for g in range(ATT_GROUP)], axis=0)
    blocks = []
    for i in range(n_src):
        k_ref, vt_ref = refs[2 * i], refs[2 * i + 1]
        lk = k_ref.shape[0]
        bk = min(ATT_KV_BLOCK, lk)
        blocks += [(k_ref, vt_ref, j * bk, bk) for j in range(lk // bk)]

    def scores(blk):
        k_ref, _, off, bk = blk
        return _dot_nt(k_ref[off:off + bk, :], qs)

    m = jnp.full((1, n), -1e30, F32)
    acc = jnp.zeros((vt_rows, n), F32)
    def accumulate(acc, blk, alpha, p):
        _, vt_ref, off, bk = blk
        return alpha * acc + _dot(vt_ref[:, off:off + bk], p)

    nb = len(blocks)
    ahead = [scores(blocks[j]) for j in range(min(2, nb))]
    pending = None
    for j in range(nb):
        s = ahead.pop(0)
        if j + 2 < nb:
            ahead.append(scores(blocks[j + 2]))
        m_new = jnp.maximum(m, jnp.max(s, axis=0, keepdims=True))
        alpha = jnp.exp2(m - m_new)
        p = jnp.exp2((s - m_new).astype(BF))
        m = m_new
        if pending is not None:
            acc = accumulate(acc, *pending)
        pending = (blocks[j], alpha, p)
    acc = accumulate(acc, *pending)

    o = acc[:HEAD_DIM] * (1.0 / acc[HEAD_DIM:HEAD_DIM + 1])
    for g in range(ATT_GROUP):
        o_ref[:, g * LANE:(g + 1) * LANE] = o[:, g * tq:(g + 1) * tq].T.astype(BF)


def _attention(q, kvs, tq):
    b, lq, qw = q.shape
    gw = ATT_GROUP * LANE
    in_specs = [pl.BlockSpec((None, tq, gw), lambda i, h, t: (i, t, h))]
    args = [q]
    for k, vt in kvs:
        lk = k.shape[1]
        in_specs.append(pl.BlockSpec((None, lk, LANE), lambda i, h, t: (i, 0, h)))
        in_specs.append(pl.BlockSpec((None, None, vt.shape[2], lk), lambda i, h, t: (i, h, 0, 0)))
        args += [k, vt]
    return pl.pallas_call(
        functools.partial(_attn_body, len(kvs)),
        grid=(b, ATT_KV_HEADS, lq // tq),
        in_specs=in_specs,
        out_specs=pl.BlockSpec((None, tq, gw), lambda i, h, t: (i, t, h)),
        out_shape=jax.ShapeDtypeStruct((b, lq, qw), BF),
        compiler_params=_cparams("parallel", "parallel", "parallel"),
        name="attention",
    )(*args)


def _out_att_body(o_ref, x_ref, mod_ref, w_ref, y_ref):
    y_ref[...] = x_ref[...] + mod_ref[2:3, :] * _dot(o_ref[...], w_ref[...])


def _out_att(o, x, mod, w_out):
    b, l, d = x.shape
    tm = _row_tile(l, 512)
    row = lambda width: pl.BlockSpec((None, tm, width), lambda i, t: (i, t, 0))
    return pl.pallas_call(
        _out_att_body,
        grid=(b, l // tm),
        in_specs=[row(o.shape[-1]), row(d), pl.BlockSpec((None, 6, d), lambda i, t: (i, 0, 0)),
                  pl.BlockSpec(w_out.shape, lambda i, t: (0, 0))],
        out_specs=row(d),
        out_shape=jax.ShapeDtypeStruct((b, l, d), F32),
        compiler_params=_cparams("parallel", "parallel"),
        name="out_att",
    )(o, x, mod, w_out)


def _ffn_body(on_grid, ck, has_final, *refs):
    if on_grid:
        xp_ref, xm_ref, xn_ref = refs[:3]
        refs = refs[3:]
    else:
        xm_ref = refs[0]
        refs = refs[1:]
    mod_ref, wu_ref, cw_ref, cb_ref, wd_ref = refs[:5]
    fg_ref = refs[5] if has_final else None
    o_ref, h_buf = refs[-2:]
    tm = xm_ref.shape[0]
    d_ff = wd_ref.shape[0]
    n_chunk = d_ff // ck
    cols = lambda c: slice(c * ck, (c + 1) * ck)
    halo = GRID_W if on_grid else 0
    rows = tm + 2 * halo
    mod = mod_ref[...]
    md = lambda x: _modulate(x, mod[3:4], mod[4:5])
    if on_grid:
        h_buf[0:halo, :] = md(xp_ref[...])
        h_buf[halo:halo + tm, :] = md(xm_ref[...])
        h_buf[halo + tm:rows, :] = md(xn_ref[...])
    else:
        h_buf[...] = md(xm_ref[...])
    r = lax.broadcasted_iota(jnp.int32, (rows, ck), 0)
    if on_grid:
        t = pl.program_id(1)
        nt = pl.num_programs(1)
        col = r & (GRID_W - 1)
        has_left = col != 0
        has_right = col != GRID_W - 1
        inside = jnp.logical_and(jnp.logical_or(r >= halo, t > 0),
                                 jnp.logical_or(r < halo + tm, t < nt - 1))
    else:
        has_left = r != 0
        has_right = r != rows - 1

    hm_sl = slice(halo, halo + tm)

    def up(c):
        gate = _dot(h_buf[...], wu_ref[:, cols(c)])
        val = _dot(h_buf[hm_sl, :], wu_ref[:, d_ff + c * ck:d_ff + (c + 1) * ck])
        return gate, val

    def down(c, gate, val):
        if on_grid:
            gate = jnp.where(inside, gate, 0.0)
        gl = jnp.where(has_left, pltpu.roll(gate, 1, axis=0), 0.0)
        gr = jnp.where(has_right, pltpu.roll(gate, rows - 1, axis=0), 0.0)
        cw = cw_ref[:, cols(c)]
        if on_grid:
            conv = None
            for dy in range(3):
                sl = slice(dy * halo, dy * halo + tm)
                term = (cw[3 * dy:3 * dy + 1] * gl[sl] + cw[3 * dy + 1:3 * dy + 2] * gate[sl]
                        + cw[3 * dy + 2:3 * dy + 3] * gr[sl])
                conv = term if conv is None else conv + term
        else:
            conv = cw[3:4] * gl + cw[4:5] * gate + cw[5:6] * gr
        return (_silu(conv + cb_ref[:, cols(c)]) * val).astype(BF)

    nxt = up(0)
    acc = None
    acts = []
    for c in range(n_chunk):
        cur = nxt
        if c + 1 < n_chunk:
            nxt = up(c + 1)
        acts.append(down(c, *cur))
        if len(acts) == FFN_DOWN_GROUP or c + 1 == n_chunk:
            c0 = c + 1 - len(acts)
            part = _dot(jnp.concatenate(acts, axis=1), wd_ref[c0 * ck:(c + 1) * ck, :])
            acc = part if acc is None else acc + part
            acts = []
    y = xm_ref[...] + mod[5:6] * acc
    if has_final:
        y = _rms(y) * fg_ref[...]
    o_ref[...] = y


def _ffn(x, mod, wu, cw, cb, wd, on_grid, final_g=None):
    b, l, d = x.shape
    ck = _ffn_chunk(wd.shape[0])
    extra = [] if final_g is None else [final_g.reshape(1, d)]
    const = lambda a: pl.BlockSpec(a.shape, lambda i, t: (0,) * a.ndim, pipeline_mode=pl.Buffered(1))
    modspec = pl.BlockSpec((None, 6, d), lambda i, t: (i, 0, 0))
    if on_grid:
        tm = FFN_TILE if l % FFN_TILE == 0 else 512
        per = tm // GRID_W
        n_rows = l // GRID_W
        x_specs = [pl.BlockSpec((None, GRID_W, d), lambda i, t: (i, jnp.maximum(t * per - 1, 0), 0)),
                   pl.BlockSpec((None, tm, d), lambda i, t: (i, t, 0)),
                   pl.BlockSpec((None, GRID_W, d), lambda i, t: (i, jnp.minimum((t + 1) * per, n_rows - 1), 0))]
        xs = [x, x, x]
        rows = tm + 2 * GRID_W
    else:
        tm = l
        x_specs = [pl.BlockSpec((None, tm, d), lambda i, t: (i, t, 0))]
        xs = [x]
        rows = tm
    return pl.pallas_call(
        functools.partial(_ffn_body, on_grid, ck, final_g is not None),
        grid=(b, l // tm),
        in_specs=(x_specs + [modspec, const(wu), const(cw), const(cb), const(wd)]
                  + [const(a) for a in extra]),
        out_specs=pl.BlockSpec((None, tm, d), lambda i, t: (i, t, 0)),
        out_shape=jax.ShapeDtypeStruct((b, l, d), F32),
        scratch_shapes=[pltpu.VMEM((rows, d), BF)],
        compiler_params=_cparams("parallel", "parallel"),
        name="ffn_grid" if on_grid else "ffn_seq",
    )(*xs, mod, wu, cw, cb, wd, *extra)


def _ffn_chunk(d_ff):
    for ck in (256, 128):
        if d_ff % ck == 0:
            return ck
    return d_ff


def _even_mixer(x, ctx, ml, mc, w_in, w_out, lb_logits, gn, e, need_ctx):
    w_in = w_in.astype(BF)
    w_out = w_out.astype(BF)
    gn = gn.reshape(1, -1)
    lbl = lb_logits.reshape(lb_logits.shape[0], -1).astype(F32)
    a_l, q_l, lf_l, k_l, v_l, g_l = _proj_even(x, ml, w_in, lbl, e)
    a_c, q_c, lf_c, k_c, v_c, g_c = _proj_even(ctx, mc, w_in, lbl, e)
    pc, pl_ = (q_c, lf_c, k_c, v_c), (q_l, lf_l, k_l, v_l)
    (of_c, of_l), (ob_c, ob_l) = _scan(pc, pl_)
    mix_l = _dft_mix_half if x.shape[1] % (2 * DFT_TILE) == 0 else _dft_mix
    x = _out_even(mix_l(a_l), of_l, ob_l, g_l, x, ml, gn, w_out)
    if need_ctx:
        ctx = _out_even(_dft_mix(a_c), of_c, ob_c, g_c, ctx, mc, gn, w_out)
    return x, ctx


def _att_mixer(x, ctx, ml, mc, w_qkv, qn_g, kn_g, w_out, need_ctx):
    l, lc = x.shape[1], ctx.shape[1]
    perm = np.concatenate([np.arange(0, HEAD_DIM, 2), np.arange(1, HEAD_DIM, 2)])
    n_qk = ATT_HEADS + ATT_KV_HEADS
    cols = np.concatenate([h * HEAD_DIM + perm for h in range(n_qk)])
    w_qk = w_qkv[:, cols].astype(BF)
    w_vt = w_qkv[:, n_qk * HEAD_DIM:].T.astype(BF)
    w_out = w_out.astype(BF)
    qg = qn_g[perm].reshape(1, -1)
    kg = kn_g[perm].reshape(1, -1)
    cos_l, sin_l = (jnp.asarray(t) for t in _rope_tables(l))
    cos_c, sin_c = jnp.ones((lc, HEAD_DIM), F32), jnp.zeros((lc, HEAD_DIM), F32)
    q_l, k_l, vt_l = _proj_att(x, ml, w_qk, w_vt, qg, kg, cos_l, sin_l)
    q_c, k_c, vt_c = _proj_att(ctx, mc, w_qk, w_vt, qg, kg, cos_c, sin_c)
    x = _out_att(_attention(q_l, [(k_l, vt_l), (k_c, vt_c)], min(l, ATT_Q_TILE)), x, ml, w_out)
    if need_ctx:
        ctx = _out_att(_attention(q_c, [(k_c, vt_c)], lc), ctx, mc, w_out)
    return x, ctx


def _ffn_layer(x, ctx, ml, mc, w_up, conv_w, conv_b, w_down, need_ctx, final_g=None):
    d = x.shape[-1]
    d_ff = w_down.shape[0]
    wu = w_up.astype(BF)
    cw = conv_w.reshape(9, d_ff)
    cb = conv_b.reshape(1, d_ff)
    wd = w_down.astype(BF)
    x = _ffn(x, ml, wu, cw, cb, wd, True, final_g)
    if need_ctx:
        ctx = _ffn(ctx, mc, wu, cw, cb, wd, False)
    return x, ctx


def _mod_vectors(c, c_ctx, w_mod, b_mod):
    b, d = c.shape
    depth = w_mod.shape[0]
    pad = (-(b + 1)) % 8
    cc = jnp.concatenate([c, c_ctx[None, :], jnp.zeros((pad, d), F32)], axis=0)
    mods = _modulation(cc, w_mod, b_mod)
    mod_lat = mods[:, :b].reshape(depth, b, 6, d)
    mod_ctx = jnp.broadcast_to(mods[:, b].reshape(depth, 1, 6, d), (depth, b, 6, d))
    return mod_lat, mod_ctx


def kernel(x, c, ctx, c_ctx, w_mod, b_mod, w_in_ab, w_out_ab, hg_lb_logits, hg_norm_g, w_qkv, q_norm_g,
           k_norm_g, w_out_att, w_up, conv_w, conv_b, w_down, final_norm_g):
    depth = w_mod.shape[0]
    mod_lat, mod_ctx = _mod_vectors(c, c_ctx, w_mod, b_mod)
    for layer in range(depth):
        need_ctx = layer != depth - 1
        ml, mc = mod_lat[layer], mod_ctx[layer]
        if layer % 2 == 0:
            e = layer // 2
            x, ctx = _even_mixer(x, ctx, ml, mc, w_in_ab[e], w_out_ab[e], hg_lb_logits, hg_norm_g[e], e, need_ctx)
        else:
            o = layer // 2
            x, ctx = _att_mixer(x, ctx, ml, mc, w_qkv[o], q_norm_g[o], k_norm_g[o], w_out_att[o], need_ctx)
        x, ctx = _ffn_layer(x, ctx, ml, mc, w_up[layer], conv_w[layer], conv_b[layer], w_down[layer], need_ctx,
                            None if need_ctx else final_norm_g)
    return x
```

```python
import functools
import math

import numpy as np
import jax
import jax.numpy as jnp
from jax import lax
from jax.experimental import pallas as pl
from jax.experimental.pallas import tpu as pltpu

F32 = jnp.float32
BF = jnp.bfloat16

EPS = 1e-6
GRID_W = 64
LANE = 128
SUBLANES = 8
HEAD_DIM = 128
FNET_GROUPS = 4
HG_HEADS = 4
HG_W = HG_HEADS * HEAD_DIM
ATT_HEADS = 8
ATT_KV_HEADS = 2
ATT_GROUP = ATT_HEADS // ATT_KV_HEADS
ROPE_THETA = 10000.0
SCAN_CHUNK = 128
ROW_TILE = 1024
FFN_TILE = 512
FFN_DOWN_GROUP = 6
VMEM_LIMIT = 56 * 1024 * 1024


def _cparams(*sem):
    return pltpu.CompilerParams(dimension_semantics=sem, vmem_limit_bytes=VMEM_LIMIT)


def _dot(a, b):
    return jnp.dot(a, b, preferred_element_type=F32)


def _dot_nt(a, b):
    return lax.dot_general(a, b, (((1,), (1,)), ((), ())), preferred_element_type=F32)


def _dot_tn(a, b):
    return lax.dot_general(a, b, (((0,), (0,)), ((), ())), preferred_element_type=F32)


def _rms(x):
    return x * lax.rsqrt(jnp.mean(x * x, axis=-1, keepdims=True) + EPS)


def _silu(x):
    return x * jax.nn.sigmoid(x)


def _modulate(x, shift, scale):
    return (_rms(x) * (1.0 + scale) + shift).astype(BF)


def _row_tile(n, pref):
    return pref if n % pref == 0 else n


def _mod_body(c_ref, w_ref, b_ref, o_ref):
    s = _silu(c_ref[...]).astype(BF)
    o_ref[...] = _dot(s, w_ref[...].astype(BF)) + b_ref[...]


def _modulation(cc, w_mod, b_mod):
    depth, d, six_d = w_mod.shape
    r = cc.shape[0]
    return pl.pallas_call(
        _mod_body,
        grid=(depth, six_d // d),
        in_specs=[pl.BlockSpec((r, d), lambda l, j: (0, 0)),
                  pl.BlockSpec((None, d, d), lambda l, j: (l, 0, j)),
                  pl.BlockSpec((None, 1, d), lambda l, j: (l, 0, j))],
        out_specs=pl.BlockSpec((None, r, d), lambda l, j: (l, 0, j)),
        out_shape=jax.ShapeDtypeStruct((depth, r, six_d), F32),
        compiler_params=_cparams("parallel", "parallel"),
        name="modulation",
    )(cc, w_mod, b_mod.reshape(depth, 1, six_d))


def _proj_even_body(e, n_even, x_ref, mod_ref, w_ref, lbl_ref, a_ref, q_ref, lf_ref, k_ref, v_ref, g_ref):
    mod = mod_ref[...]
    h = _modulate(x_ref[...], mod[0:1], mod[1:2])
    rows = [lbl_ref[i:i + 1, :] for i in range(n_even)]
    mx = functools.reduce(jnp.maximum, rows)
    ex = [jnp.exp(r - mx) for r in rows]
    den = functools.reduce(lambda a, b: a + b, ex)
    lb = jnp.zeros_like(mx)
    for i in range(1, e + 1):
        lb = lb + ex[i] / den
    w = HG_W
    a_ref[...] = _dot(h, w_ref[:, 0:w]).astype(BF)
    q_ref[...] = _silu(_dot(h, w_ref[:, w:2 * w])).astype(BF)
    for d in range(2):
        z = _dot(h, w_ref[:, (2 + d) * w:(3 + d) * w])
        lbd = lb[:, d * w:(d + 1) * w]
        f = lbd + (1.0 - lbd) * jax.nn.sigmoid(z)
        lf_ref[:, d * w:(d + 1) * w] = jnp.log2(f)
        k_ref[:, d * w:(d + 1) * w] = (1.0 - f).astype(BF)
    v_ref[...] = _dot(h, w_ref[:, 4 * w:5 * w]).astype(BF)
    g_ref[...] = _dot(h, w_ref[:, 5 * w:6 * w]).astype(BF)


def _proj_even(x, mod, w_in, lb_logits, e):
    b, l, d = x.shape
    n_even = lb_logits.shape[0]
    tm = _row_tile(l, ROW_TILE)
    w = HG_W
    row = lambda width: pl.BlockSpec((None, tm, width), lambda i, t: (i, t, 0))
    shp = lambda width, dt: jax.ShapeDtypeStruct((b, l, width), dt)
    return pl.pallas_call(
        functools.partial(_proj_even_body, e, n_even),
        grid=(b, l // tm),
        in_specs=[row(d),
                  pl.BlockSpec((None, 6, d), lambda i, t: (i, 0, 0)),
                  pl.BlockSpec(w_in.shape, lambda i, t: (0, 0)),
                  pl.BlockSpec(lb_logits.shape, lambda i, t: (0, 0))],
        out_specs=[row(w), row(w), row(2 * w), row(2 * w), row(w), row(w)],
        out_shape=[shp(w, BF), shp(w, BF), shp(2 * w, F32), shp(2 * w, BF), shp(w, BF), shp(w, BF)],
        compiler_params=_cparams("parallel", "parallel"),
        name="proj_even",
    )(x, mod, w_in, lb_logits)


@functools.lru_cache(maxsize=None)
def _dft_tables(n):
    idx = np.arange(n, dtype=np.int64)
    ang = 2.0 * np.pi * ((idx[:, None] * idx[None, :]) % n).astype(np.float64) / n
    s = 1.0 / math.sqrt(n)
    return (np.cos(ang) * s).astype(np.float32), (np.sin(ang) * s).astype(np.float32)


def _dft_body(c_ref, s_ref, a_ref, cs_ref, o_ref):
    a = a_ref[...]
    p = _dot(c_ref[...], a).astype(BF)
    q = _dot(s_ref[...], a).astype(BF)
    cs = cs_ref[...]
    for g in range(FNET_GROUPS):
        sl = slice(g * LANE, (g + 1) * LANE)
        pq = jnp.concatenate([p[:, sl], q[:, sl]], axis=1)
        o_ref[:, sl] = _dot(pq, cs).astype(BF)


def _dft_mix(a):
    b, l, w = a.shape
    cl, sl_ = _dft_tables(l)
    cc, sc = _dft_tables(LANE)
    cos_l = jnp.asarray(cl).astype(BF)
    sin_l = jnp.asarray(sl_).astype(BF)
    cs = jnp.concatenate([jnp.asarray(cc), -jnp.asarray(sc)], axis=0).astype(BF)
    tm = _row_tile(l, ROW_TILE)
    return pl.pallas_call(
        _dft_body,
        grid=(l // tm, b),
        in_specs=[pl.BlockSpec((tm, l), lambda m, i: (m, 0)),
                  pl.BlockSpec((tm, l), lambda m, i: (m, 0)),
                  pl.BlockSpec((None, l, w), lambda m, i: (i, 0, 0)),
                  pl.BlockSpec(cs.shape, lambda m, i: (0, 0))],
        out_specs=pl.BlockSpec((None, tm, w), lambda m, i: (i, m, 0)),
        out_shape=jax.ShapeDtypeStruct((b, l, w), BF),
        compiler_params=_cparams("parallel", "parallel"),
        name="dft_mix",
    )(cos_l, sin_l, a, cs)


DFT_TILE = 512
DFT_EXTRA = 16


@functools.lru_cache(maxsize=None)
def _dft_half_tables(l):
    cl, sl_ = _dft_tables(l)
    t, e = DFT_TILE, DFT_EXTRA
    n = l // (2 * t)
    cos = np.stack([cl[j * t:j * t + t + e] for j in range(n)])
    sin = np.stack([sl_[j * t:j * t + t + e] for j in range(n)])
    rev = np.zeros((t, t + e), np.float32)
    rev[np.arange(t), t - np.arange(t)] = 1.0
    return cos, sin, rev


def _dft_half_body(c_ref, s_ref, a_ref, cs_ref, rev_ref, lo_ref, hi_ref):
    t = lo_ref.shape[0]
    a = a_ref[...]
    p = _dot(c_ref[...], a).astype(BF)
    q = _dot(s_ref[...], a).astype(BF)
    cs = cs_ref[...]
    rev = rev_ref[...]
    for g in range(FNET_GROUPS):
        sl = slice(g * LANE, (g + 1) * LANE)
        y = _dot(jnp.concatenate([p[:, sl], q[:, sl]], axis=1), cs)
        lo_ref[:, sl] = y[:t, :LANE].astype(BF)
        hi_ref[:, sl] = _dot(rev, y[:, LANE:].astype(BF)).astype(BF)


def _dft_mix_half(a):
    b, l, w = a.shape
    t, e = DFT_TILE, DFT_EXTRA
    n = l // (2 * t)
    cos, sin, rev = (jnp.asarray(x).astype(BF) for x in _dft_half_tables(l))
    cc, sc = (jnp.asarray(x) for x in _dft_tables(LANE))
    cs = jnp.concatenate([jnp.concatenate([cc, -sc], axis=0), jnp.concatenate([cc, sc], axis=0)], axis=1).astype(BF)
    half = jax.ShapeDtypeStruct((b, l // 2, w), BF)
    lo, hi = pl.pallas_call(
        _dft_half_body,
        grid=(n, b),
        in_specs=[pl.BlockSpec((None, t + e, l), lambda j, i: (j, 0, 0)),
                  pl.BlockSpec((None, t + e, l), lambda j, i: (j, 0, 0)),
                  pl.BlockSpec((None, l, w), lambda j, i: (i, 0, 0)),
                  pl.BlockSpec(cs.shape, lambda j, i: (0, 0)),
                  pl.BlockSpec(rev.shape, lambda j, i: (0, 0))],
        out_specs=[pl.BlockSpec((None, t, w), lambda j, i: (i, j, 0)),
                   pl.BlockSpec((None, t, w), lambda j, i: (i, n - 1 - j, 0))],
        out_shape=[half, half],
        compiler_params=_cparams("parallel", "parallel"),
        name="dft_mix_half",
    )(cos, sin, a, cs, rev)
    return lo, hi


def _scan_chunk(rev, q_ref, k_ref, lf_ref, v_ref, o_ref, st_ref):
    c, w = lf_ref.shape
    nlev = c.bit_length() - 1
    lf = lf_ref[...]
    r = lax.broadcasted_iota(jnp.int32, (c, w), 0)
    pos = (c - 1 - r) if rev else r

    def tile_roll(x, sh):
        return pltpu.roll(x.reshape(c // SUBLANES, SUBLANES, w), sh, axis=1).reshape(c, w)

    def from_prev(x, sh):
        return tile_roll(x, (SUBLANES - sh) if rev else sh)

    def from_next(x, sh):
        return tile_roll(x, sh if rev else (SUBLANES - sh))

    def shift_tiles(x, sh):
        zero = jnp.zeros((sh, w), F32)
        return jnp.concatenate([x[sh:], zero], axis=0) if rev else jnp.concatenate([zero, x[:c - sh]], axis=0)

    def neg_abs(x):
        return pltpu.bitcast(pltpu.bitcast(x, jnp.uint32) | jnp.uint32(0x80000000), F32)

    def spread_half(x, sh, low):
        parts = []
        for i in range(c // (2 * sh)):
            first, second = x[2 * i * sh:(2 * i + 1) * sh], x[(2 * i + 1) * sh:(2 * i + 2) * sh]
            src = (second if rev else first) if low else (first if rev else second)
            parts += [src, src]
        return jnp.concatenate(parts, axis=0)

    pos8 = pos & (SUBLANES - 1)
    b = lf
    sh = 1
    while sh < SUBLANES:
        b = b + jnp.where(pos8 >= sh, from_prev(b, sh), 0.0)
        sh *= 2
    last = 0 if rev else SUBLANES - 1
    b3 = b.reshape(c // SUBLANES, SUBLANES, w)
    tot = jnp.broadcast_to(b3[:, last:last + 1, :], b3.shape).reshape(c, w)
    while sh < c:
        tot = tot + shift_tiles(tot, sh)
        sh *= 2
    b = b + shift_tiles(tot, SUBLANES)

    q = q_ref[...]
    k = k_ref[...]
    ti = lax.broadcasted_iota(jnp.int32, (c, c), 0)
    si = lax.broadcasted_iota(jnp.int32, (c, c), 1)
    later = (ti < si) if rev else (ti > si)
    diff = ti ^ si
    att = [jnp.zeros((c, c), F32) for _ in range(HG_HEADS)]
    y = b
    for j in range(nlev):
        sh = 1 << j
        upper = ((pos >> j) & 1) == 1
        bound = spread_half(y, sh, True) if sh % 8 == 0 else jnp.where(upper, from_prev(y, sh), y)
        e = jnp.exp2(neg_abs(b - bound)).astype(BF)
        qm = q * e
        km = k * e
        level = jnp.logical_and((diff >> j) == 1, later)
        for h in range(HG_HEADS):
            sl = slice(h * LANE, (h + 1) * LANE)
            att[h] = jnp.where(level, _dot_nt(qm[:, sl], km[:, sl]), att[h])
        y = spread_half(y, sh, False) if sh % 8 == 0 else jnp.where(upper, y, from_next(y, sh))
    b_last = y
    qe = q * jnp.exp2(b).astype(BF)
    kd = k * jnp.exp2(b_last - b).astype(BF)
    carry = jnp.exp2(b_last[0:1, :])
    v = v_ref[...]
    vf = v.astype(F32)
    for h in range(HG_HEADS):
        sl = slice(h * LANE, (h + 1) * LANE)
        st = st_ref[h]
        inter = _dot_nt(qe[:, sl], st.astype(BF))
        diag = jnp.sum(q[:, sl].astype(F32) * k[:, sl].astype(F32), axis=-1, keepdims=True)
        o_ref[:, sl] = (inter + _dot(att[h].astype(BF), v[:, sl]) + diag * vf[:, sl]).astype(o_ref.dtype)
        st_ref[h] = st * carry[:, sl] + _dot_tn(v[:, sl], kd[:, sl])


def _scan_body(n_ctx, *refs):
    ins, outs, states = refs[:16], refs[16:20], refs[20:]
    s = pl.program_id(1)

    @pl.when(s == 0)
    def _():
        for st_ref in states:
            st_ref[...] = jnp.zeros_like(st_ref)

    def run(stream):
        for d in range(2):
            base = 8 * d + 4 * stream
            _scan_chunk(d == 1, *ins[base:base + 4], outs[2 * d + stream], states[d])

    @pl.when(s < n_ctx)
    def _():
        run(0)

    @pl.when(s >= n_ctx)
    def _():
        run(1)


def _scan(pc, pl_):
    qc, lfc, kc, vc = pc
    ql, lfl, kl, vl = pl_
    b, lc, w = qc.shape
    ll = ql.shape[1]
    c = SCAN_CHUNK
    n_ctx, n_lat = lc // c, ll // c

    def spec(idx, col):
        return pl.BlockSpec((None, c, w), lambda i, s: (i, idx(s), col))

    in_specs, out_specs, args = [], [], []
    for d in range(2):
        if d:
            ci = lambda s: jnp.maximum(n_ctx - 1 - s, 0)
            li = lambda s: n_lat - 1 - jnp.maximum(s - n_ctx, 0)
        else:
            ci = lambda s: jnp.minimum(s, n_ctx - 1)
            li = lambda s: jnp.maximum(s - n_ctx, 0)
        in_specs += [spec(ci, 0), spec(ci, d), spec(ci, d), spec(ci, 0),
                     spec(li, 0), spec(li, d), spec(li, d), spec(li, 0)]
        args += [qc, kc, lfc, vc, ql, kl, lfl, vl]
        out_specs += [spec(ci, 0), spec(li, 0)]
    shp = lambda l: jax.ShapeDtypeStruct((b, l, w), BF)
    of_c, of_l, ob_c, ob_l = pl.pallas_call(
        functools.partial(_scan_body, n_ctx),
        grid=(b, n_ctx + n_lat),
        in_specs=in_specs,
        out_specs=out_specs,
        out_shape=[shp(lc), shp(ll), shp(lc), shp(ll)],
        scratch_shapes=[pltpu.VMEM((HG_HEADS, HEAD_DIM, HEAD_DIM), F32) for _ in range(2)],
        compiler_params=_cparams("parallel", "arbitrary"),
        name="hgrn_scan",
    )(*args)
    return (of_c, of_l), (ob_c, ob_l)


def _out_even_body(n_lo, *refs):
    yf_refs, (of_ref, ob_ref, g_ref, x_ref, mod_ref, gn_ref, w_ref, o_ref) = refs[:-8], refs[-8:]
    if n_lo:
        yf = jnp.where(pl.program_id(1) < n_lo, yf_refs[0][...], yf_refs[1][...])
    else:
        yf = yf_refs[0][...]
    o = of_ref[...].astype(F32) + ob_ref[...].astype(F32)
    g = g_ref[...].astype(F32)
    gn = gn_ref[...]
    parts = []
    for h in range(HG_HEADS):
        sl = slice(h * LANE, (h + 1) * LANE)
        parts.append((_rms(o[:, sl]) * gn[:, sl] * _silu(g[:, sl])).astype(BF))
    gated = jnp.concatenate(parts, axis=1)
    w = HG_W
    y = _dot(yf, w_ref[0:w, :]) + _dot(gated, w_ref[w:2 * w, :])
    o_ref[...] = x_ref[...] + mod_ref[2:3, :] * y


def _out_even(yf, o_f, o_b, g, x, mod, gn, w_out):
    b, l, d = x.shape
    tm = _row_tile(l, ROW_TILE)
    row = lambda width: pl.BlockSpec((None, tm, width), lambda i, t: (i, t, 0))
    if isinstance(yf, tuple):
        assert yf[0].shape[1] % tm == 0
        n_lo = yf[0].shape[1] // tm
        yf_specs = [pl.BlockSpec((None, tm, HG_W), lambda i, t: (i, jnp.minimum(t, n_lo - 1), 0)),
                    pl.BlockSpec((None, tm, HG_W), lambda i, t: (i, jnp.maximum(t - n_lo, 0), 0))]
    else:
        n_lo, yf, yf_specs = 0, (yf,), [row(HG_W)]
    return pl.pallas_call(
        functools.partial(_out_even_body, n_lo),
        grid=(b, l // tm),
        in_specs=yf_specs + [row(HG_W), row(HG_W), row(HG_W), row(d),
                  pl.BlockSpec((None, 6, d), lambda i, t: (i, 0, 0)),
                  pl.BlockSpec(gn.shape, lambda i, t: (0, 0)),
                  pl.BlockSpec(w_out.shape, lambda i, t: (0, 0))],
        out_specs=row(d),
        out_shape=jax.ShapeDtypeStruct((b, l, d), F32),
        compiler_params=_cparams("parallel", "parallel"),
        name="out_even",
    )(*yf, o_f, o_b, g, x, mod, gn, w_out)


def _proj_att_body(x_ref, mod_ref, w_ref, wvt_ref, qg_ref, kg_ref, cos_ref, sin_ref, q_ref, k_ref, vt_ref):
    mod = mod_ref[...]
    h = _modulate(x_ref[...], mod[0:1], mod[1:2])
    cos = cos_ref[...]
    sin = sin_ref[...]
    scale = HEAD_DIM ** -0.5 * math.log2(math.e)
    for i in range(ATT_HEADS + ATT_KV_HEADS):
        sl = slice(i * LANE, (i + 1) * LANE)
        p = _dot(h, w_ref[:, sl])
        gain = qg_ref[...] if i < ATT_HEADS else kg_ref[...]
        p = _rms(p) * gain
        p = p * cos + pltpu.roll(p, HEAD_DIM // 2, axis=1) * sin
        if i < ATT_HEADS:
            q_ref[:, sl] = (p * scale).astype(BF)
        else:
            k_ref[:, (i - ATT_HEADS) * LANE:(i - ATT_HEADS + 1) * LANE] = p.astype(BF)
    vt = _dot_nt(wvt_ref[...], h)
    for hk in range(ATT_KV_HEADS):
        vt_ref[hk, 0:HEAD_DIM, :] = vt[hk * LANE:(hk + 1) * LANE].astype(BF)
        vt_ref[hk, HEAD_DIM:, :] = jnp.ones((ATT_ONES_ROWS, vt.shape[1]), BF)


def _proj_att(x, mod, w_qk, w_vt, qg, kg, cos, sin):
    b, l, d = x.shape
    tm = _row_tile(l, ROW_TILE // 2)
    qw, kw = ATT_HEADS * LANE, ATT_KV_HEADS * LANE
    vt_rows = HEAD_DIM + ATT_ONES_ROWS
    row = lambda width: pl.BlockSpec((None, tm, width), lambda i, t: (i, t, 0))
    const = lambda a: pl.BlockSpec(a.shape, lambda i, t: (0, 0))
    tab = pl.BlockSpec((tm, LANE), lambda i, t: (t, 0))
    return pl.pallas_call(
        _proj_att_body,
        grid=(b, l // tm),
        in_specs=[row(d), pl.BlockSpec((None, 6, d), lambda i, t: (i, 0, 0)),
                  const(w_qk), const(w_vt), const(qg), const(kg), tab, tab],
        out_specs=[row(qw), row(kw),
                   pl.BlockSpec((None, ATT_KV_HEADS, vt_rows, tm), lambda i, t: (i, 0, 0, t))],
        out_shape=[jax.ShapeDtypeStruct((b, l, qw), BF), jax.ShapeDtypeStruct((b, l, kw), BF),
                   jax.ShapeDtypeStruct((b, ATT_KV_HEADS, vt_rows, l), BF)],
        compiler_params=_cparams("parallel", "parallel"),
        name="proj_att",
    )(x, mod, w_qk, w_vt, qg, kg, cos, sin)


@functools.lru_cache(maxsize=None)
def _rope_tables(l):
    t = np.arange(l)
    row = (t // GRID_W).astype(np.float64)
    col = (t % GRID_W).astype(np.float64)
    n_freq = HEAD_DIM // 4
    freqs = ROPE_THETA ** (-np.arange(n_freq, dtype=np.float64) / n_freq)
    ang = np.concatenate([row[:, None] * freqs, col[:, None] * freqs], axis=-1)
    cos, sin = np.cos(ang), np.sin(ang)
    return (np.concatenate([cos, cos], axis=-1).astype(np.float32),
            np.concatenate([-sin, sin], axis=-1).astype(np.float32))


ATT_KV_BLOCK = 256
ATT_ONES_ROWS = 16
ATT_SCORE_GROUP = 1
ATT_Q_TILE = 256


def _attn_body(n_src, q_ref, *refs):
    o_ref = refs[2 * n_src]
    tq = q_ref.shape[0]
    n = ATT_GROUP * tq
    vt_rows = refs[1].shape[0]
    q = q_ref[...]
    qs = jnp.concatenate([q[:, g * LANE:(g + 1) * LANE] for g in range(ATT_GROUP)], axis=0)
    blocks = []
    groups = []
    for i in range(n_src):
        k_ref, vt_ref = refs[2 * i], refs[2 * i + 1]
        lk = k_ref.shape[0]
        bk = min(ATT_KV_BLOCK, lk)
        n_blk = lk // bk
        blocks += [(vt_ref, j * bk, bk) for j in range(n_blk)]
        for j in range(0, n_blk, ATT_SCORE_GROUP):
            cnt = min(ATT_SCORE_GROUP, n_blk - j)
            groups.append((k_ref, j * bk, bk, cnt))
    group_iter = iter(groups)
    queue = []

    def refill():
        while len(queue) < 2:
            grp = next(group_iter, None)
            if grp is None:
                return
            k_ref, off, bk, cnt = grp
            sg = _dot_nt(k_ref[off:off + cnt * bk, :], qs)
            queue.extend(sg[i * bk:(i + 1) * bk] for i in range(cnt))

    m = jnp.full((1, n), -1e30, F32)
    acc = jnp.zeros((vt_rows, n), F32)
    def accumulate(acc, blk, alpha, p):
        vt_ref, off, bk = blk
        return alpha * acc + _dot(vt_ref[:, off:off + bk], p)

    refill()
    pending = None
    for j in range(len(blocks)):
        s = queue.pop(0)
        refill()
        m_new = jnp.maximum(m, jnp.max(s, axis=0, keepdims=True))
        alpha = jnp.exp2(m - m_new)
        p = jnp.exp2((s - m_new).astype(BF))
        m = m_new
        if pending is not None:
            acc = accumulate(acc, *pending)
        pending = (blocks[j], alpha, p)
    acc = accumulate(acc, *pending)

    o = acc[:HEAD_DIM] * (1.0 / acc[HEAD_DIM:HEAD_DIM + 1])
    for g in range(ATT_GROUP):
        o_ref[:, g * LANE:(g + 1) * LANE] = o[:, g * tq:(g + 1) * tq].T.astype(BF)


def _attention(q, kvs, tq):
    b, lq, qw = q.shape
    gw = ATT_GROUP * LANE
    in_specs = [pl.BlockSpec((None, tq, gw), lambda i, h, t: (i, t, h))]
    args = [q]
    for k, vt in kvs:
        lk = k.shape[1]
        in_specs.append(pl.BlockSpec((None, lk, LANE), lambda i, h, t: (i, 0, h)))
        in_specs.append(pl.BlockSpec((None, None, vt.shape[2], lk), lambda i, h, t: (i, h, 0, 0)))
        args += [k, vt]
    return pl.pallas_call(
        functools.partial(_attn_body, len(kvs)),
        grid=(b, ATT_KV_HEADS, lq // tq),
        in_specs=in_specs,
        out_specs=pl.BlockSpec((None, tq, gw), lambda i, h, t: (i, t, h)),
        out_shape=jax.ShapeDtypeStruct((b, lq, qw), BF),
        compiler_params=_cparams("parallel", "parallel", "parallel"),
        name="attention",
    )(*args)


def _out_att_body(o_ref, x_ref, mod_ref, w_ref, y_ref):
    y_ref[...] = x_ref[...] + mod_ref[2:3, :] * _dot(o_ref[...], w_ref[...])


def _out_att(o, x, mod, w_out):
    b, l, d = x.shape
    tm = _row_tile(l, ROW_TILE)
    row = lambda width: pl.BlockSpec((None, tm, width), lambda i, t: (i, t, 0))
    return pl.pallas_call(
        _out_att_body,
        grid=(b, l // tm),
        in_specs=[row(o.shape[-1]), row(d), pl.BlockSpec((None, 6, d), lambda i, t: (i, 0, 0)),
                  pl.BlockSpec(w_out.shape, lambda i, t: (0, 0))],
        out_specs=row(d),
        out_shape=jax.ShapeDtypeStruct((b, l, d), F32),
        compiler_params=_cparams("parallel", "parallel"),
        name="out_att",
    )(o, x, mod, w_out)


def _ffn_body(on_grid, ck, has_final, *refs):
    if on_grid:
        xp_ref, xm_ref, xn_ref = refs[:3]
        refs = refs[3:]
    else:
        xm_ref = refs[0]
        refs = refs[1:]
    mod_ref, wu_ref, cw_ref, cb_ref, wd_ref = refs[:5]
    fg_ref = refs[5] if has_final else None
    o_ref, h_buf = refs[-2:]
    tm = xm_ref.shape[0]
    d_ff = wd_ref.shape[0]
    n_chunk = d_ff // ck
    cols = lambda c: slice(c * ck, (c + 1) * ck)
    halo = GRID_W if on_grid else 0
    rows = tm + 2 * halo
    mod = mod_ref[...]
    md = lambda x: _modulate(x, mod[3:4], mod[4:5])
    if on_grid:
        h_buf[0:halo, :] = md(xp_ref[...])
        h_buf[halo:halo + tm, :] = md(xm_ref[...])
        h_buf[halo + tm:rows, :] = md(xn_ref[...])
    else:
        h_buf[...] = md(xm_ref[...])
    r = lax.broadcasted_iota(jnp.int32, (rows, ck), 0)
    if on_grid:
        t = pl.program_id(1)
        nt = pl.num_programs(1)
        col = r & (GRID_W - 1)
        has_left = col != 0
        has_right = col != GRID_W - 1
        inside = jnp.logical_and(jnp.logical_or(r >= halo, t > 0),
                                 jnp.logical_or(r < halo + tm, t < nt - 1))
    else:
        has_left = r != 0
        has_right = r != rows - 1

    hm_sl = slice(halo, halo + tm)

    def up(c):
        gate = _dot(h_buf[...], wu_ref[:, cols(c)])
        val = _dot(h_buf[hm_sl, :], wu_ref[:, d_ff + c * ck:d_ff + (c + 1) * ck])
        return gate, val

    def down(c, gate, val):
        if on_grid:
            gate = jnp.where(inside, gate, 0.0)
        gl = jnp.where(has_left, pltpu.roll(gate, 1, axis=0), 0.0)
        gr = jnp.where(has_right, pltpu.roll(gate, rows - 1, axis=0), 0.0)
        cw = cw_ref[:, cols(c)]
        if on_grid:
            conv = None
            for dy in range(3):
                sl = slice(dy * halo, dy * halo + tm)
                term = (cw[3 * dy:3 * dy + 1] * gl[sl] + cw[3 * dy + 1:3 * dy + 2] * gate[sl]
                        + cw[3 * dy + 2:3 * dy + 3] * gr[sl])
                conv = term if conv is None else conv + term
        else:
            conv = cw[3:4] * gl + cw[4:5] * gate + cw[5:6] * gr
        return (_silu(conv + cb_ref[:, cols(c)]) * val).astype(BF)

    nxt = up(0)
    acc = None
    acts = []
    for c in range(n_chunk):
        cur = nxt
        if c + 1 < n_chunk:
            nxt = up(c + 1)
        acts.append(down(c, *cur))
        if len(acts) == FFN_DOWN_GROUP or c + 1 == n_chunk:
            c0 = c + 1 - len(acts)
            part = _dot(jnp.concatenate(acts, axis=1), wd_ref[c0 * ck:(c + 1) * ck, :])
            acc = part if acc is None else acc + part
            acts = []
    y = xm_ref[...] + mod[5:6] * acc
    if has_final:
        y = _rms(y) * fg_ref[...]
    o_ref[...] = y


def _ffn(x, mod, wu, cw, cb, wd, on_grid, final_g=None):
    b, l, d = x.shape
    ck = _ffn_chunk(wd.shape[0])
    extra = [] if final_g is None else [final_g.reshape(1, d)]
    const = lambda a: pl.BlockSpec(a.shape, lambda i, t: (0,) * a.ndim, pipeline_mode=pl.Buffered(1))
    modspec = pl.BlockSpec((None, 6, d), lambda i, t: (i, 0, 0))
    if on_grid:
        tm = FFN_TILE if l % FFN_TILE == 0 else 512
        per = tm // GRID_W
        n_rows = l // GRID_W
        x_specs = [pl.BlockSpec((None, GRID_W, d), lambda i, t: (i, jnp.maximum(t * per - 1, 0), 0)),
                   pl.BlockSpec((None, tm, d), lambda i, t: (i, t, 0)),
                   pl.BlockSpec((None, GRID_W, d), lambda i, t: (i, jnp.minimum((t + 1) * per, n_rows - 1), 0))]
        xs = [x, x, x]
        rows = tm + 2 * GRID_W
    else:
        tm = l
        x_specs = [pl.BlockSpec((None, tm, d), lambda i, t: (i, t, 0))]
        xs = [x]
        rows = tm
    return pl.pallas_call(
        functools.partial(_ffn_body, on_grid, ck, final_g is not None),
        grid=(b, l // tm),
        in_specs=(x_specs + [modspec, const(wu), const(cw), const(cb), const(wd)]
                  + [const(a) for a in extra]),
        out_specs=pl.BlockSpec((None, tm, d), lambda i, t: (i, t, 0)),
        out_shape=jax.ShapeDtypeStruct((b, l, d), F32),
        scratch_shapes=[pltpu.VMEM((rows, d), BF)],
        compiler_params=_cparams("parallel", "parallel"),
        name="ffn_grid" if on_grid else "ffn_seq",
    )(*xs, mod, wu, cw, cb, wd, *extra)


def _ffn_chunk(d_ff):
    for ck in (256, 128):
        if d_ff % ck == 0:
            return ck
    return d_ff


def _even_mixer(x, ctx, ml, mc, w_in, w_out, lb_logits, gn, e, need_ctx):
    w_in = w_in.astype(BF)
    w_out = w_out.astype(BF)
    gn = gn.reshape(1, -1)
    lbl = lb_logits.reshape(lb_logits.shape[0], -1).astype(F32)
    a_l, q_l, lf_l, k_l, v_l, g_l = _proj_even(x, ml, w_in, lbl, e)
    a_c, q_c, lf_c, k_c, v_c, g_c = _proj_even(ctx, mc, w_in, lbl, e)
    pc, pl_ = (q_c, lf_c, k_c, v_c), (q_l, lf_l, k_l, v_l)
    (of_c, of_l), (ob_c, ob_l) = _scan(pc, pl_)
    mix_l = _dft_mix_half if x.shape[1] % (2 * max(DFT_TILE, ROW_TILE)) == 0 else _dft_mix
    x = _out_even(mix_l(a_l), of_l, ob_l, g_l, x, ml, gn, w_out)
    if need_ctx:
        ctx = _out_even(_dft_mix(a_c), of_c, ob_c, g_c, ctx, mc, gn, w_out)
    return x, ctx


def _att_mixer(x, ctx, ml, mc, w_qkv, qn_g, kn_g, w_out, need_ctx):
    l, lc = x.shape[1], ctx.shape[1]
    perm = np.concatenate([np.arange(0, HEAD_DIM, 2), np.arange(1, HEAD_DIM, 2)])
    n_qk = ATT_HEADS + ATT_KV_HEADS
    cols = np.concatenate([h * HEAD_DIM + perm for h in range(n_qk)])
    w_qk = w_qkv[:, cols].astype(BF)
    w_vt = w_qkv[:, n_qk * HEAD_DIM:].T.astype(BF)
    w_out = w_out.astype(BF)
    qg = qn_g[perm].reshape(1, -1)
    kg = kn_g[perm].reshape(1, -1)
    cos_l, sin_l = (jnp.asarray(t) for t in _rope_tables(l))
    cos_c, sin_c = jnp.ones((lc, HEAD_DIM), F32), jnp.zeros((lc, HEAD_DIM), F32)
    q_l, k_l, vt_l = _proj_att(x, ml, w_qk, w_vt, qg, kg, cos_l, sin_l)
    q_c, k_c, vt_c = _proj_att(ctx, mc, w_qk, w_vt, qg, kg, cos_c, sin_c)
    x = _out_att(_attention(q_l, [(k_l, vt_l), (k_c, vt_c)], min(l, ATT_Q_TILE)), x, ml, w_out)
    if need_ctx:
        ctx = _out_att(_attention(q_c, [(k_c, vt_c)], lc), ctx, mc, w_out)
    return x, ctx


def _ffn_layer(x, ctx, ml, mc, w_up, conv_w, conv_b, w_down, need_ctx, final_g=None):
    d = x.shape[-1]
    d_ff = w_down.shape[0]
    wu = w_up.astype(BF)
    cw = conv_w.reshape(9, d_ff)
    cb = conv_b.reshape(1, d_ff)
    wd = w_down.astype(BF)
    x = _ffn(x, ml, wu, cw, cb, wd, True, final_g)
    if need_ctx:
        ctx = _ffn(ctx, mc, wu, cw, cb, wd, False)
    return x, ctx


def _mod_vectors(c, c_ctx, w_mod, b_mod):
    b, d = c.shape
    depth = w_mod.shape[0]
    pad = (-(b + 1)) % 8
    cc = jnp.concatenate([c, c_ctx[None, :], jnp.zeros((pad, d), F32)], axis=0)
    mods = _modulation(cc, w_mod, b_mod)
    mod_lat = mods[:, :b].reshape(depth, b, 6, d)
    mod_ctx = jnp.broadcast_to(mods[:, b].reshape(depth, 1, 6, d), (depth, b, 6, d))
    return mod_lat, mod_ctx


def kernel(x, c, ctx, c_ctx, w_mod, b_mod, w_in_ab, w_out_ab, hg_lb_logits, hg_norm_g, w_qkv, q_norm_g,
           k_norm_g, w_out_att, w_up, conv_w, conv_b, w_down, final_norm_g):
    depth = w_mod.shape[0]
    mod_lat, mod_ctx = _mod_vectors(c, c_ctx, w_mod, b_mod)
    for layer in range(depth):
        need_ctx = layer != depth - 1
        ml, mc = mod_lat[layer], mod_ctx[layer]
        if layer % 2 == 0:
            e = layer // 2
            x, ctx = _even_mixer(x, ctx, ml, mc, w_in_ab[e], w_out_ab[e], hg_lb_logits, hg_norm_g[e], e, need_ctx)
        else:
            o = layer // 2
            x, ctx = _att_mixer(x, ctx, ml, mc, w_qkv[o], q_norm_g[o], k_norm_g[o], w_out_att[o], need_ctx)
        x, ctx = _ffn_layer(x, ctx, ml, mc, w_up[layer], conv_w[layer], conv_b[layer], w_down[layer], need_ctx,
                            None if need_ctx else final_norm_g)
    return x
```

```python
import functools
import math

import numpy as np
import jax
import jax.numpy as jnp
from jax import lax
from jax.experimental import pallas as pl
from jax.experimental.pallas import tpu as pltpu

F32 = jnp.float32
BF = jnp.bfloat16

EPS = 1e-6
GRID_W = 64
LANE = 128
SUBLANES = 8
HEAD_DIM = 128
FNET_GROUPS = 4
HG_HEADS = 4
HG_W = HG_HEADS * HEAD_DIM
ATT_HEADS = 8
ATT_KV_HEADS = 2
ATT_GROUP = ATT_HEADS // ATT_KV_HEADS
ROPE_THETA = 10000.0
SCAN_CHUNK = 128
ROW_TILE = 1024
FFN_TILE = 512
PROJ_ATT_TILE = 512
FFN_CHUNK = 256
FFN_DOWN_GROUP = 6
VMEM_LIMIT = 56 * 1024 * 1024


def _cparams(*sem):
    return pltpu.CompilerParams(dimension_semantics=sem, vmem_limit_bytes=VMEM_LIMIT)


def _dot(a, b):
    return jnp.dot(a, b, preferred_element_type=F32)


def _dot_nt(a, b):
    return lax.dot_general(a, b, (((1,), (1,)), ((), ())), preferred_element_type=F32)


def _dot_tn(a, b):
    return lax.dot_general(a, b, (((0,), (0,)), ((), ())), preferred_element_type=F32)


def _rms(x):
    return x * lax.rsqrt(jnp.mean(x * x, axis=-1, keepdims=True) + EPS)


def _silu(x):
    return x * jax.nn.sigmoid(x)


def _modulate(x, shift, scale):
    return (_rms(x) * (1.0 + scale) + shift).astype(BF)


def _row_tile(n, pref):
    return pref if n % pref == 0 else n


def _mod_body(c_ref, w_ref, b_ref, o_ref):
    s = _silu(c_ref[...]).astype(BF)
    o_ref[...] = _dot(s, w_ref[...].astype(BF)) + b_ref[...]


def _modulation(cc, w_mod, b_mod):
    depth, d, six_d = w_mod.shape
    r = cc.shape[0]
    return pl.pallas_call(
        _mod_body,
        grid=(depth, six_d // d),
        in_specs=[pl.BlockSpec((r, d), lambda l, j: (0, 0)),
                  pl.BlockSpec((None, d, d), lambda l, j: (l, 0, j)),
                  pl.BlockSpec((None, 1, d), lambda l, j: (l, 0, j))],
        out_specs=pl.BlockSpec((None, r, d), lambda l, j: (l, 0, j)),
        out_shape=jax.ShapeDtypeStruct((depth, r, six_d), F32),
        compiler_params=_cparams("parallel", "parallel"),
        name="modulation",
    )(cc, w_mod, b_mod.reshape(depth, 1, six_d))


def _proj_even_body(e, n_even, x_ref, mod_ref, w_ref, lbl_ref, a_ref, q_ref, lf_ref, k_ref, v_ref, g_ref):
    mod = mod_ref[...]
    h = _modulate(x_ref[...], mod[0:1], mod[1:2])
    rows = [lbl_ref[i:i + 1, :] for i in range(n_even)]
    mx = functools.reduce(jnp.maximum, rows)
    ex = [jnp.exp(r - mx) for r in rows]
    den = functools.reduce(lambda a, b: a + b, ex)
    lb = jnp.zeros_like(mx)
    for i in range(1, e + 1):
        lb = lb + ex[i] / den
    w = HG_W
    a_ref[...] = _dot(h, w_ref[:, 0:w]).astype(BF)
    q_ref[...] = _silu(_dot(h, w_ref[:, w:2 * w])).astype(BF)
    for d in range(2):
        z = _dot(h, w_ref[:, (2 + d) * w:(3 + d) * w])
        lbd = lb[:, d * w:(d + 1) * w]
        f = lbd + (1.0 - lbd) * jax.nn.sigmoid(z)
        lf_ref[:, d * w:(d + 1) * w] = jnp.log2(f)
        k_ref[:, d * w:(d + 1) * w] = (1.0 - f).astype(BF)
    v_ref[...] = _dot(h, w_ref[:, 4 * w:5 * w]).astype(BF)
    g_ref[...] = _dot(h, w_ref[:, 5 * w:6 * w]).astype(BF)


def _proj_even(x, mod, w_in, lb_logits, e):
    b, l, d = x.shape
    n_even = lb_logits.shape[0]
    tm = _row_tile(l, ROW_TILE)
    w = HG_W
    row = lambda width: pl.BlockSpec((None, tm, width), lambda i, t: (i, t, 0))
    shp = lambda width, dt: jax.ShapeDtypeStruct((b, l, width), dt)
    return pl.pallas_call(
        functools.partial(_proj_even_body, e, n_even),
        grid=(b, l // tm),
        in_specs=[row(d),
                  pl.BlockSpec((None, 6, d), lambda i, t: (i, 0, 0)),
                  pl.BlockSpec(w_in.shape, lambda i, t: (0, 0)),
                  pl.BlockSpec(lb_logits.shape, lambda i, t: (0, 0))],
        out_specs=[row(w), row(w), row(2 * w), row(2 * w), row(w), row(w)],
        out_shape=[shp(w, BF), shp(w, BF), shp(2 * w, F32), shp(2 * w, BF), shp(w, BF), shp(w, BF)],
        compiler_params=_cparams("parallel", "parallel"),
        name="proj_even",
    )(x, mod, w_in, lb_logits)


@functools.lru_cache(maxsize=None)
def _dft_tables(n):
    idx = np.arange(n, dtype=np.int64)
    ang = 2.0 * np.pi * ((idx[:, None] * idx[None, :]) % n).astype(np.float64) / n
    s = 1.0 / math.sqrt(n)
    return (np.cos(ang) * s).astype(np.float32), (np.sin(ang) * s).astype(np.float32)


def _dft_body(c_ref, s_ref, a_ref, cs_ref, o_ref):
    a = a_ref[...]
    p = _dot(c_ref[...], a).astype(BF)
    q = _dot(s_ref[...], a).astype(BF)
    cs = cs_ref[...]
    for g in range(FNET_GROUPS):
        sl = slice(g * LANE, (g + 1) * LANE)
        pq = jnp.concatenate([p[:, sl], q[:, sl]], axis=1)
        o_ref[:, sl] = _dot(pq, cs).astype(BF)


def _dft_mix(a):
    b, l, w = a.shape
    cl, sl_ = _dft_tables(l)
    cc, sc = _dft_tables(LANE)
    cos_l = jnp.asarray(cl).astype(BF)
    sin_l = jnp.asarray(sl_).astype(BF)
    cs = jnp.concatenate([jnp.asarray(cc), -jnp.asarray(sc)], axis=0).astype(BF)
    tm = _row_tile(l, ROW_TILE)
    return pl.pallas_call(
        _dft_body,
        grid=(l // tm, b),
        in_specs=[pl.BlockSpec((tm, l), lambda m, i: (m, 0)),
                  pl.BlockSpec((tm, l), lambda m, i: (m, 0)),
                  pl.BlockSpec((None, l, w), lambda m, i: (i, 0, 0)),
                  pl.BlockSpec(cs.shape, lambda m, i: (0, 0))],
        out_specs=pl.BlockSpec((None, tm, w), lambda m, i: (i, m, 0)),
        out_shape=jax.ShapeDtypeStruct((b, l, w), BF),
        compiler_params=_cparams("parallel", "parallel"),
        name="dft_mix",
    )(cos_l, sin_l, a, cs)


DFT_TILE = 512
DFT_EXTRA = 16


@functools.lru_cache(maxsize=None)
def _dft_half_tables(l):
    cl, sl_ = _dft_tables(l)
    t, e = DFT_TILE, DFT_EXTRA
    n = l // (2 * t)
    cos = np.stack([cl[j * t:j * t + t + e] for j in range(n)])
    sin = np.stack([sl_[j * t:j * t + t + e] for j in range(n)])
    rev = np.zeros((t, t + e), np.float32)
    rev[np.arange(t), t - np.arange(t)] = 1.0
    return cos, sin, rev


def _dft_half_body(c_ref, s_ref, a_ref, cs_ref, rev_ref, lo_ref, hi_ref):
    t = lo_ref.shape[0]
    a = a_ref[...]
    p = _dot(c_ref[...], a).astype(BF)
    q = _dot(s_ref[...], a).astype(BF)
    cs = cs_ref[...]
    rev = rev_ref[...]
    for g in range(FNET_GROUPS):
        sl = slice(g * LANE, (g + 1) * LANE)
        y = _dot(jnp.concatenate([p[:, sl], q[:, sl]], axis=1), cs)
        lo_ref[:, sl] = y[:t, :LANE].astype(BF)
        hi_ref[:, sl] = _dot(rev, y[:, LANE:].astype(BF)).astype(BF)


def _dft_mix_half(a):
    b, l, w = a.shape
    t, e = DFT_TILE, DFT_EXTRA
    n = l // (2 * t)
    cos, sin, rev = (jnp.asarray(x).astype(BF) for x in _dft_half_tables(l))
    cc, sc = (jnp.asarray(x) for x in _dft_tables(LANE))
    cs = jnp.concatenate([jnp.concatenate([cc, -sc], axis=0), jnp.concatenate([cc, sc], axis=0)], axis=1).astype(BF)
    half = jax.ShapeDtypeStruct((b, l // 2, w), BF)
    lo, hi = pl.pallas_call(
        _dft_half_body,
        grid=(n, b),
        in_specs=[pl.BlockSpec((None, t + e, l), lambda j, i: (j, 0, 0)),
                  pl.BlockSpec((None, t + e, l), lambda j, i: (j, 0, 0)),
                  pl.BlockSpec((None, l, w), lambda j, i: (i, 0, 0)),
                  pl.BlockSpec(cs.shape, lambda j, i: (0, 0)),
                  pl.BlockSpec(rev.shape, lambda j, i: (0, 0))],
        out_specs=[pl.BlockSpec((None, t, w), lambda j, i: (i, j, 0)),
                   pl.BlockSpec((None, t, w), lambda j, i: (i, n - 1 - j, 0))],
        out_shape=[half, half],
        compiler_params=_cparams("parallel", "parallel"),
        name="dft_mix_half",
    )(cos, sin, a, cs, rev)
    return lo, hi


def _scan_chunk(rev, q_ref, k_ref, lf_ref, v_ref, o_ref, st_ref):
    c, w = lf_ref.shape
    nlev = c.bit_length() - 1
    lf = lf_ref[...]
    r = lax.broadcasted_iota(jnp.int32, (c, w), 0)
    pos = (c - 1 - r) if rev else r

    def tile_roll(x, sh):
        return pltpu.roll(x.reshape(c // SUBLANES, SUBLANES, w), sh, axis=1).reshape(c, w)

    def from_prev(x, sh):
        return tile_roll(x, (SUBLANES - sh) if rev else sh)

    def from_next(x, sh):
        return tile_roll(x, sh if rev else (SUBLANES - sh))

    def shift_tiles(x, sh):
        zero = jnp.zeros((sh, w), F32)
        return jnp.concatenate([x[sh:], zero], axis=0) if rev else jnp.concatenate([zero, x[:c - sh]], axis=0)

    def neg_abs(x):
        return pltpu.bitcast(pltpu.bitcast(x, jnp.uint32) | jnp.uint32(0x80000000), F32)

    def spread_half(x, sh, low):
        parts = []
        for i in range(c // (2 * sh)):
            first, second = x[2 * i * sh:(2 * i + 1) * sh], x[(2 * i + 1) * sh:(2 * i + 2) * sh]
            src = (second if rev else first) if low else (first if rev else second)
            parts += [src, src]
        return jnp.concatenate(parts, axis=0)

    pos8 = pos & (SUBLANES - 1)
    b = lf
    sh = 1
    while sh < SUBLANES:
        b = b + jnp.where(pos8 >= sh, from_prev(b, sh), 0.0)
        sh *= 2
    last = 0 if rev else SUBLANES - 1
    b3 = b.reshape(c // SUBLANES, SUBLANES, w)
    tot = jnp.broadcast_to(b3[:, last:last + 1, :], b3.shape).reshape(c, w)
    while sh < c:
        tot = tot + shift_tiles(tot, sh)
        sh *= 2
    b = b + shift_tiles(tot, SUBLANES)

    q = q_ref[...]
    k = k_ref[...]
    ti = lax.broadcasted_iota(jnp.int32, (c, c), 0)
    si = lax.broadcasted_iota(jnp.int32, (c, c), 1)
    later = (ti < si) if rev else (ti > si)
    diff = ti ^ si
    att = [jnp.zeros((c, c), F32) for _ in range(HG_HEADS)]
    y = b
    for j in range(nlev):
        sh = 1 << j
        upper = ((pos >> j) & 1) == 1
        bound = spread_half(y, sh, True) if sh % 8 == 0 else jnp.where(upper, from_prev(y, sh), y)
        e = jnp.exp2(neg_abs(b - bound)).astype(BF)
        qm = q * e
        km = k * e
        level = jnp.logical_and((diff >> j) == 1, later)
        for h in range(HG_HEADS):
            sl = slice(h * LANE, (h + 1) * LANE)
            att[h] = jnp.where(level, _dot_nt(qm[:, sl], km[:, sl]), att[h])
        y = spread_half(y, sh, False) if sh % 8 == 0 else jnp.where(upper, y, from_next(y, sh))
    b_last = y
    qe = q * jnp.exp2(b).astype(BF)
    kd = k * jnp.exp2(b_last - b).astype(BF)
    carry = jnp.exp2(b_last[0:1, :])
    v = v_ref[...]
    vf = v.astype(F32)
    for h in range(HG_HEADS):
        sl = slice(h * LANE, (h + 1) * LANE)
        st = st_ref[h]
        inter = _dot_nt(qe[:, sl], st.astype(BF))
        diag = jnp.sum(q[:, sl].astype(F32) * k[:, sl].astype(F32), axis=-1, keepdims=True)
        o_ref[:, sl] = (inter + _dot(att[h].astype(BF), v[:, sl]) + diag * vf[:, sl]).astype(o_ref.dtype)
        st_ref[h] = st * carry[:, sl] + _dot_tn(v[:, sl], kd[:, sl])


def _scan_body(n_ctx, *refs):
    ins, outs, states = refs[:16], refs[16:20], refs[20:]
    s = pl.program_id(1)

    @pl.when(s == 0)
    def _():
        for st_ref in states:
            st_ref[...] = jnp.zeros_like(st_ref)

    def run(stream):
        for d in range(2):
            base = 8 * d + 4 * stream
            _scan_chunk(d == 1, *ins[base:base + 4], outs[2 * d + stream], states[d])

    @pl.when(s < n_ctx)
    def _():
        run(0)

    @pl.when(s >= n_ctx)
    def _():
        run(1)


def _scan(pc, pl_):
    qc, lfc, kc, vc = pc
    ql, lfl, kl, vl = pl_
    b, lc, w = qc.shape
    ll = ql.shape[1]
    c = SCAN_CHUNK
    n_ctx, n_lat = lc // c, ll // c

    def spec(idx, col):
        return pl.BlockSpec((None, c, w), lambda i, s: (i, idx(s), col))

    in_specs, out_specs, args = [], [], []
    for d in range(2):
        if d:
            ci = lambda s: jnp.maximum(n_ctx - 1 - s, 0)
            li = lambda s: n_lat - 1 - jnp.maximum(s - n_ctx, 0)
        else:
            ci = lambda s: jnp.minimum(s, n_ctx - 1)
            li = lambda s: jnp.maximum(s - n_ctx, 0)
        in_specs += [spec(ci, 0), spec(ci, d), spec(ci, d), spec(ci, 0),
                     spec(li, 0), spec(li, d), spec(li, d), spec(li, 0)]
        args += [qc, kc, lfc, vc, ql, kl, lfl, vl]
        out_specs += [spec(ci, 0), spec(li, 0)]
    shp = lambda l: jax.ShapeDtypeStruct((b, l, w), BF)
    of_c, of_l, ob_c, ob_l = pl.pallas_call(
        functools.partial(_scan_body, n_ctx),
        grid=(b, n_ctx + n_lat),
        in_specs=in_specs,
        out_specs=out_specs,
        out_shape=[shp(lc), shp(ll), shp(lc), shp(ll)],
        scratch_shapes=[pltpu.VMEM((HG_HEADS, HEAD_DIM, HEAD_DIM), F32) for _ in range(2)],
        compiler_params=_cparams("parallel", "arbitrary"),
        name="hgrn_scan",
    )(*args)
    return (of_c, of_l), (ob_c, ob_l)


def _out_even_body(n_lo, *refs):
    yf_refs, (of_ref, ob_ref, g_ref, x_ref, mod_ref, gn_ref, w_ref, o_ref) = refs[:-8], refs[-8:]
    if n_lo:
        yf = jnp.where(pl.program_id(1) < n_lo, yf_refs[0][...], yf_refs[1][...])
    else:
        yf = yf_refs[0][...]
    o = of_ref[...].astype(F32) + ob_ref[...].astype(F32)
    g = g_ref[...].astype(F32)
    gn = gn_ref[...]
    parts = []
    for h in range(HG_HEADS):
        sl = slice(h * LANE, (h + 1) * LANE)
        parts.append((_rms(o[:, sl]) * gn[:, sl] * _silu(g[:, sl])).astype(BF))
    gated = jnp.concatenate(parts, axis=1)
    w = HG_W
    y = _dot(yf, w_ref[0:w, :]) + _dot(gated, w_ref[w:2 * w, :])
    o_ref[...] = x_ref[...] + mod_ref[2:3, :] * y


def _out_even(yf, o_f, o_b, g, x, mod, gn, w_out):
    b, l, d = x.shape
    tm = _row_tile(l, ROW_TILE)
    row = lambda width: pl.BlockSpec((None, tm, width), lambda i, t: (i, t, 0))
    if isinstance(yf, tuple):
        assert yf[0].shape[1] % tm == 0
        n_lo = yf[0].shape[1] // tm
        yf_specs = [pl.BlockSpec((None, tm, HG_W), lambda i, t: (i, jnp.minimum(t, n_lo - 1), 0)),
                    pl.BlockSpec((None, tm, HG_W), lambda i, t: (i, jnp.maximum(t - n_lo, 0), 0))]
    else:
        n_lo, yf, yf_specs = 0, (yf,), [row(HG_W)]
    return pl.pallas_call(
        functools.partial(_out_even_body, n_lo),
        grid=(b, l // tm),
        in_specs=yf_specs + [row(HG_W), row(HG_W), row(HG_W), row(d),
                  pl.BlockSpec((None, 6, d), lambda i, t: (i, 0, 0)),
                  pl.BlockSpec(gn.shape, lambda i, t: (0, 0)),
                  pl.BlockSpec(w_out.shape, lambda i, t: (0, 0))],
        out_specs=row(d),
        out_shape=jax.ShapeDtypeStruct((b, l, d), F32),
        compiler_params=_cparams("parallel", "parallel"),
        name="out_even",
    )(*yf, o_f, o_b, g, x, mod, gn, w_out)


def _proj_att_body(x_ref, mod_ref, w_ref, wvt_ref, qg_ref, kg_ref, cos_ref, sin_ref, q_ref, k_ref, vt_ref):
    mod = mod_ref[...]
    h = _modulate(x_ref[...], mod[0:1], mod[1:2])
    cos = cos_ref[...]
    sin = sin_ref[...]
    scale = HEAD_DIM ** -0.5 * math.log2(math.e)
    for i in range(ATT_HEADS + ATT_KV_HEADS):
        sl = slice(i * LANE, (i + 1) * LANE)
        p = _dot(h, w_ref[:, sl])
        gain = qg_ref[...] if i < ATT_HEADS else kg_ref[...]
        p = _rms(p) * gain
        p = p * cos + pltpu.roll(p, HEAD_DIM // 2, axis=1) * sin
        if i < ATT_HEADS:
            q_ref[:, sl] = (p * scale).astype(BF)
        else:
            k_ref[:, (i - ATT_HEADS) * LANE:(i - ATT_HEADS + 1) * LANE] = p.astype(BF)
    vt = _dot_nt(wvt_ref[...], h)
    for hk in range(ATT_KV_HEADS):
        vt_ref[hk, 0:HEAD_DIM, :] = vt[hk * LANE:(hk + 1) * LANE].astype(BF)
        vt_ref[hk, HEAD_DIM:, :] = jnp.ones((ATT_ONES_ROWS, vt.shape[1]), BF)


def _proj_att(x, mod, w_qk, w_vt, qg, kg, cos, sin):
    b, l, d = x.shape
    tm = _row_tile(l, PROJ_ATT_TILE)
    qw, kw = ATT_HEADS * LANE, ATT_KV_HEADS * LANE
    vt_rows = HEAD_DIM + ATT_ONES_ROWS
    row = lambda width: pl.BlockSpec((None, tm, width), lambda i, t: (i, t, 0))
    const = lambda a: pl.BlockSpec(a.shape, lambda i, t: (0, 0))
    tab = pl.BlockSpec((tm, LANE), lambda i, t: (t, 0))
    return pl.pallas_call(
        _proj_att_body,
        grid=(b, l // tm),
        in_specs=[row(d), pl.BlockSpec((None, 6, d), lambda i, t: (i, 0, 0)),
                  const(w_qk), const(w_vt), const(qg), const(kg), tab, tab],
        out_specs=[row(qw), row(kw),
                   pl.BlockSpec((None, ATT_KV_HEADS, vt_rows, tm), lambda i, t: (i, 0, 0, t))],
        out_shape=[jax.ShapeDtypeStruct((b, l, qw), BF), jax.ShapeDtypeStruct((b, l, kw), BF),
                   jax.ShapeDtypeStruct((b, ATT_KV_HEADS, vt_rows, l), BF)],
        compiler_params=_cparams("parallel", "parallel"),
        name="proj_att",
    )(x, mod, w_qk, w_vt, qg, kg, cos, sin)


@functools.lru_cache(maxsize=None)
def _rope_tables(l):
    t = np.arange(l)
    row = (t // GRID_W).astype(np.float64)
    col = (t % GRID_W).astype(np.float64)
    n_freq = HEAD_DIM // 4
    freqs = ROPE_THETA ** (-np.arange(n_freq, dtype=np.float64) / n_freq)
    ang = np.concatenate([row[:, None] * freqs, col[:, None] * freqs], axis=-1)
    cos, sin = np.cos(ang), np.sin(ang)
    return (np.concatenate([cos, cos], axis=-1).astype(np.float32),
            np.concatenate([-sin, sin], axis=-1).astype(np.float32))


ATT_KV_BLOCK = 256
ATT_ONES_ROWS = 16
ATT_Q_TILE = 512


def _attn_body(n_src, q_ref, *refs):
    o_ref = refs[2 * n_src]
    tq = q_ref.shape[0]
    n = ATT_GROUP * tq
    vt_rows = refs[1].shape[0]
    q = q_ref[...]
    qs = jnp.concatenate([q[:, g * LANE:(g + 1) * LANE] for g in range(ATT_GROUP)], axis=0)
    blocks = []
    for i in range(n_src):
        k_ref, vt_ref = refs[2 * i], refs[2 * i + 1]
        lk = k_ref.shape[0]
        bk = min(ATT_KV_BLOCK, lk)
        blocks += [(k_ref, vt_ref, j * bk, bk) for j in range(lk // bk)]
    to_score = iter(blocks)
    queue = []

    def refill():
        while len(queue) < 2:
            blk = next(to_score, None)
            if blk is None:
                return
            k_ref, _, off, bk = blk
            queue.append(_dot_nt(k_ref[off:off + bk, :], qs))

    m = jnp.full((1, n), -1e30, F32)
    acc = jnp.zeros((vt_rows, n), F32)
    def accumulate(acc, blk, alpha, p):
        _, vt_ref, off, bk = blk
        return alpha * acc + _dot(vt_ref[:, off:off + bk], p)

    refill()
    pending = None
    for j in range(len(blocks)):
        s = queue.pop(0)
        refill()
        m_new = jnp.maximum(m, jnp.max(s, axis=0, keepdims=True))
        alpha = jnp.exp2(m - m_new)
        p = jnp.exp2((s - m_new).astype(BF))
        m = m_new
        if pending is not None:
            acc = accumulate(acc, *pending)
        pending = (blocks[j], alpha, p)
    acc = accumulate(acc, *pending)

    o = acc[:HEAD_DIM] * (1.0 / acc[HEAD_DIM:HEAD_DIM + 1])
    for g in range(ATT_GROUP):
        o_ref[:, g * LANE:(g + 1) * LANE] = o[:, g * tq:(g + 1) * tq].T.astype(BF)


def _attention(q, kvs, tq):
    b, lq, qw = q.shape
    gw = ATT_GROUP * LANE
    in_specs = [pl.BlockSpec((None, tq, gw), lambda i, h, t: (i, t, h))]
    args = [q]
    for k, vt in kvs:
        lk = k.shape[1]
        in_specs.append(pl.BlockSpec((None, lk, LANE), lambda i, h, t: (i, 0, h)))
        in_specs.append(pl.BlockSpec((None, None, vt.shape[2], lk), lambda i, h, t: (i, h, 0, 0)))
        args += [k, vt]
    return pl.pallas_call(
        functools.partial(_attn_body, len(kvs)),
        grid=(b, ATT_KV_HEADS, lq // tq),
        in_specs=in_specs,
        out_specs=pl.BlockSpec((None, tq, gw), lambda i, h, t: (i, t, h)),
        out_shape=jax.ShapeDtypeStruct((b, lq, qw), BF),
        compiler_params=_cparams("parallel", "parallel", "parallel"),
        name="attention",
    )(*args)


def _out_att_body(o_ref, x_ref, mod_ref, w_ref, y_ref):
    y_ref[...] = x_ref[...] + mod_ref[2:3, :] * _dot(o_ref[...], w_ref[...])


def _out_att(o, x, mod, w_out):
    b, l, d = x.shape
    tm = _row_tile(l, ROW_TILE)
    row = lambda width: pl.BlockSpec((None, tm, width), lambda i, t: (i, t, 0))
    return pl.pallas_call(
        _out_att_body,
        grid=(b, l // tm),
        in_specs=[row(o.shape[-1]), row(d), pl.BlockSpec((None, 6, d), lambda i, t: (i, 0, 0)),
                  pl.BlockSpec(w_out.shape, lambda i, t: (0, 0))],
        out_specs=row(d),
        out_shape=jax.ShapeDtypeStruct((b, l, d), F32),
        compiler_params=_cparams("parallel", "parallel"),
        name="out_att",
    )(o, x, mod, w_out)


def _ffn_body(on_grid, ck, has_final, *refs):
    if on_grid:
        xp_ref, xm_ref, xn_ref = refs[:3]
        refs = refs[3:]
    else:
        xm_ref = refs[0]
        refs = refs[1:]
    mod_ref, wu_ref, cw_ref, cb_ref, wd_ref = refs[:5]
    fg_ref = refs[5] if has_final else None
    o_ref, h_buf = refs[-2:]
    tm = xm_ref.shape[0]
    d_ff = wd_ref.shape[0]
    bounds = list(range(0, d_ff, ck)) + [d_ff]
    n_chunk = len(bounds) - 1
    cols = lambda c, base=0: slice(base + bounds[c], base + bounds[c + 1])
    halo = GRID_W if on_grid else 0
    rows = tm + 2 * halo
    mod = mod_ref[...]
    md = lambda x: _modulate(x, mod[3:4], mod[4:5])
    if on_grid:
        h_buf[0:halo, :] = md(xp_ref[...])
        h_buf[halo:halo + tm, :] = md(xm_ref[...])
        h_buf[halo + tm:rows, :] = md(xn_ref[...])
    else:
        h_buf[...] = md(xm_ref[...])
    if on_grid:
        t = pl.program_id(1)
        nt = pl.num_programs(1)

    @functools.lru_cache(maxsize=None)
    def neighbour_masks(width):
        r = lax.broadcasted_iota(jnp.int32, (rows, width), 0)
        if on_grid:
            col = r & (GRID_W - 1)
            return col != 0, col != GRID_W - 1
        return r != 0, r != rows - 1

    hm_sl = slice(halo, halo + tm)

    def up(c):
        gate = _dot(h_buf[...], wu_ref[:, cols(c)])
        val = _dot(h_buf[hm_sl, :], wu_ref[:, cols(c, d_ff)])
        return gate, val

    def down(c, gate, val):
        if on_grid:
            gate = jnp.concatenate([jnp.where(t > 0, gate[:halo], 0.0), gate[halo:halo + tm],
                                    jnp.where(t < nt - 1, gate[halo + tm:], 0.0)], axis=0)
        has_left, has_right = neighbour_masks(gate.shape[1])
        gl = jnp.where(has_left, pltpu.roll(gate, 1, axis=0), 0.0)
        gr = jnp.where(has_right, pltpu.roll(gate, rows - 1, axis=0), 0.0)
        cw = cw_ref[:, cols(c)]
        if on_grid:
            conv = None
            for dy in range(3):
                sl = slice(dy * halo, dy * halo + tm)
                term = (cw[3 * dy:3 * dy + 1] * gl[sl] + cw[3 * dy + 1:3 * dy + 2] * gate[sl]
                        + cw[3 * dy + 2:3 * dy + 3] * gr[sl])
                conv = term if conv is None else conv + term
        else:
            conv = cw[3:4] * gl + cw[4:5] * gate + cw[5:6] * gr
        return (_silu(conv + cb_ref[:, cols(c)]) * val).astype(BF)

    nxt = up(0)
    acc = None
    acts = []
    for c in range(n_chunk):
        cur = nxt
        if c + 1 < n_chunk:
            nxt = up(c + 1)
        acts.append(down(c, *cur))
        if len(acts) == FFN_DOWN_GROUP or c + 1 == n_chunk:
            c0 = c + 1 - len(acts)
            part = _dot(jnp.concatenate(acts, axis=1), wd_ref[bounds[c0]:bounds[c + 1], :])
            acc = part if acc is None else acc + part
            acts = []
    y = xm_ref[...] + mod[5:6] * acc
    if has_final:
        y = _rms(y) * fg_ref[...]
    o_ref[...] = y


def _ffn(x, mod, wu, cw, cb, wd, on_grid, final_g=None):
    b, l, d = x.shape
    ck = min(FFN_CHUNK, wd.shape[0])
    extra = [] if final_g is None else [final_g.reshape(1, d)]
    const = lambda a: pl.BlockSpec(a.shape, lambda i, t: (0,) * a.ndim, pipeline_mode=pl.Buffered(1))
    modspec = pl.BlockSpec((None, 6, d), lambda i, t: (i, 0, 0))
    if on_grid:
        tm = FFN_TILE
        assert l % tm == 0 and tm % GRID_W == 0
        per = tm // GRID_W
        n_rows = l // GRID_W
        x_specs = [pl.BlockSpec((None, GRID_W, d), lambda i, t: (i, jnp.maximum(t * per - 1, 0), 0)),
                   pl.BlockSpec((None, tm, d), lambda i, t: (i, t, 0)),
                   pl.BlockSpec((None, GRID_W, d), lambda i, t: (i, jnp.minimum((t + 1) * per, n_rows - 1), 0))]
        xs = [x, x, x]
        rows = tm + 2 * GRID_W
    else:
        tm = l
        x_specs = [pl.BlockSpec((None, tm, d), lambda i, t: (i, t, 0))]
        xs = [x]
        rows = tm
    return pl.pallas_call(
        functools.partial(_ffn_body, on_grid, ck, final_g is not None),
        grid=(b, l // tm),
        in_specs=(x_specs + [modspec, const(wu), const(cw), const(cb), const(wd)]
                  + [const(a) for a in extra]),
        out_specs=pl.BlockSpec((None, tm, d), lambda i, t: (i, t, 0)),
        out_shape=jax.ShapeDtypeStruct((b, l, d), F32),
        scratch_shapes=[pltpu.VMEM((rows, d), BF)],
        compiler_params=_cparams("parallel", "parallel"),
        name="ffn_grid" if on_grid else "ffn_seq",
    )(*xs, mod, wu, cw, cb, wd, *extra)


def _even_mixer(x, ctx, ml, mc, w_in, w_out, lb_logits, gn, e, need_ctx):
    w_in = w_in.astype(BF)
    w_out = w_out.astype(BF)
    gn = gn.reshape(1, -1)
    lbl = lb_logits.reshape(lb_logits.shape[0], -1).astype(F32)
    a_l, q_l, lf_l, k_l, v_l, g_l = _proj_even(x, ml, w_in, lbl, e)
    a_c, q_c, lf_c, k_c, v_c, g_c = _proj_even(ctx, mc, w_in, lbl, e)
    pc, pl_ = (q_c, lf_c, k_c, v_c), (q_l, lf_l, k_l, v_l)
    (of_c, of_l), (ob_c, ob_l) = _scan(pc, pl_)
    mix_l = _dft_mix_half if x.shape[1] % (2 * max(DFT_TILE, ROW_TILE)) == 0 else _dft_mix
    x = _out_even(mix_l(a_l), of_l, ob_l, g_l, x, ml, gn, w_out)
    if need_ctx:
        ctx = _out_even(_dft_mix(a_c), of_c, ob_c, g_c, ctx, mc, gn, w_out)
    return x, ctx


def _att_mixer(x, ctx, ml, mc, w_qkv, qn_g, kn_g, w_out, need_ctx):
    l, lc = x.shape[1], ctx.shape[1]
    perm = np.concatenate([np.arange(0, HEAD_DIM, 2), np.arange(1, HEAD_DIM, 2)])
    n_qk = ATT_HEADS + ATT_KV_HEADS
    cols = np.concatenate([h * HEAD_DIM + perm for h in range(n_qk)])
    w_qk = w_qkv[:, cols].astype(BF)
    w_vt = w_qkv[:, n_qk * HEAD_DIM:].T.astype(BF)
    w_out = w_out.astype(BF)
    qg = qn_g[perm].reshape(1, -1)
    kg = kn_g[perm].reshape(1, -1)
    cos_l, sin_l = (jnp.asarray(t) for t in _rope_tables(l))
    cos_c, sin_c = jnp.ones((lc, HEAD_DIM), F32), jnp.zeros((lc, HEAD_DIM), F32)
    q_l, k_l, vt_l = _proj_att(x, ml, w_qk, w_vt, qg, kg, cos_l, sin_l)
    q_c, k_c, vt_c = _proj_att(ctx, mc, w_qk, w_vt, qg, kg, cos_c, sin_c)
    x = _out_att(_attention(q_l, [(k_l, vt_l), (k_c, vt_c)], min(l, ATT_Q_TILE)), x, ml, w_out)
    if need_ctx:
        ctx = _out_att(_attention(q_c, [(k_c, vt_c)], lc), ctx, mc, w_out)
    return x, ctx


def _ffn_layer(x, ctx, ml, mc, w_up, conv_w, conv_b, w_down, need_ctx, final_g=None):
    d = x.shape[-1]
    d_ff = w_down.shape[0]
    wu = w_up.astype(BF)
    cw = conv_w.reshape(9, d_ff)
    cb = conv_b.reshape(1, d_ff)
    wd = w_down.astype(BF)
    x = _ffn(x, ml, wu, cw, cb, wd, True, final_g)
    if need_ctx:
        ctx = _ffn(ctx, mc, wu, cw, cb, wd, False)
    return x, ctx


def _mod_vectors(c, c_ctx, w_mod, b_mod):
    b, d = c.shape
    depth = w_mod.shape[0]
    pad = (-(b + 1)) % 8
    cc = jnp.concatenate([c, c_ctx[None, :], jnp.zeros((pad, d), F32)], axis=0)
    mods = _modulation(cc, w_mod, b_mod)
    mod_lat = mods[:, :b].reshape(depth, b, 6, d)
    mod_ctx = jnp.broadcast_to(mods[:, b].reshape(depth, 1, 6, d), (depth, b, 6, d))
    return mod_lat, mod_ctx


def kernel(x, c, ctx, c_ctx, w_mod, b_mod, w_in_ab, w_out_ab, hg_lb_logits, hg_norm_g, w_qkv, q_norm_g,
           k_norm_g, w_out_att, w_up, conv_w, conv_b, w_down, final_norm_g):
    depth = w_mod.shape[0]
    mod_lat, mod_ctx = _mod_vectors(c, c_ctx, w_mod, b_mod)
    for layer in range(depth):
        need_ctx = layer != depth - 1
        ml, mc = mod_lat[layer], mod_ctx[layer]
        if layer % 2 == 0:
            e = layer // 2
            x, ctx = _even_mixer(x, ctx, ml, mc, w_in_ab[e], w_out_ab[e], hg_lb_logits, hg_norm_g[e], e, need_ctx)
        else:
            o = layer // 2
            x, ctx = _att_mixer(x, ctx, ml, mc, w_qkv[o], q_norm_g[o], k_norm_g[o], w_out_att[o], need_ctx)
        x, ctx = _ffn_layer(x, ctx, ml, mc, w_up[layer], conv_w[layer], conv_b[layer], w_down[layer], need_ctx,
                            None if need_ctx else final_norm_g)
    return x
```

```python
import functools
import math

import numpy as np
import jax
import jax.numpy as jnp
from jax import lax
from jax.experimental import pallas as pl
from jax.experimental.pallas import tpu as pltpu

F32 = jnp.float32
BF = jnp.bfloat16

EPS = 1e-6
GRID_W = 64
LANE = 128
SUBLANES = 8
HEAD_DIM = 128
FNET_GROUPS = 4
HG_HEADS = 4
HG_W = HG_HEADS * HEAD_DIM
ATT_HEADS = 8
ATT_KV_HEADS = 2
ATT_GROUP = ATT_HEADS // ATT_KV_HEADS
ROPE_THETA = 10000.0
SCAN_CHUNK = 128
ROW_TILE = 1024
FFN_TILE = 512
PROJ_ATT_TILE = 512
FFN_CHUNK = 256
FFN_DOWN_GROUP = 6
VMEM_LIMIT = 56 * 1024 * 1024


def _cparams(*sem):
    return pltpu.CompilerParams(dimension_semantics=sem, vmem_limit_bytes=VMEM_LIMIT)


def _dot(a, b):
    return jnp.dot(a, b, preferred_element_type=F32)


def _dot_nt(a, b):
    return lax.dot_general(a, b, (((1,), (1,)), ((), ())), preferred_element_type=F32)


def _dot_tn(a, b):
    return lax.dot_general(a, b, (((0,), (0,)), ((), ())), preferred_element_type=F32)


def _rms(x):
    return x * lax.rsqrt(jnp.mean(x * x, axis=-1, keepdims=True) + EPS)


def _silu(x):
    return x * jax.nn.sigmoid(x)


def _modulate(x, shift, scale):
    return (_rms(x) * (1.0 + scale) + shift).astype(BF)


def _row_tile(n, pref):
    return pref if n % pref == 0 else n


def _mod_body(c_ref, w_ref, b_ref, o_ref):
    s = _silu(c_ref[...]).astype(BF)
    o_ref[...] = _dot(s, w_ref[...].astype(BF)) + b_ref[...]


def _modulation(cc, w_mod, b_mod):
    depth, d, six_d = w_mod.shape
    r = cc.shape[0]
    return pl.pallas_call(
        _mod_body,
        grid=(depth, six_d // d),
        in_specs=[pl.BlockSpec((r, d), lambda l, j: (0, 0)),
                  pl.BlockSpec((None, d, d), lambda l, j: (l, 0, j)),
                  pl.BlockSpec((None, 1, d), lambda l, j: (l, 0, j))],
        out_specs=pl.BlockSpec((None, r, d), lambda l, j: (l, 0, j)),
        out_shape=jax.ShapeDtypeStruct((depth, r, six_d), F32),
        compiler_params=_cparams("parallel", "parallel"),
        name="modulation",
    )(cc, w_mod, b_mod.reshape(depth, 1, six_d))


def _proj_even_body(e, n_even, x_ref, mod_ref, w_ref, lbl_ref, a_ref, q_ref, lf_ref, k_ref, v_ref, g_ref):
    mod = mod_ref[...]
    h = _modulate(x_ref[...], mod[0:1], mod[1:2])
    rows = [lbl_ref[i:i + 1, :] for i in range(n_even)]
    mx = functools.reduce(jnp.maximum, rows)
    ex = [jnp.exp(r - mx) for r in rows]
    den = functools.reduce(lambda a, b: a + b, ex)
    lb = jnp.zeros_like(mx)
    for i in range(1, e + 1):
        lb = lb + ex[i] / den
    w = HG_W
    a_ref[...] = _dot(h, w_ref[:, 0:w]).astype(BF)
    q_ref[...] = _silu(_dot(h, w_ref[:, w:2 * w])).astype(BF)
    for d in range(2):
        z = _dot(h, w_ref[:, (2 + d) * w:(3 + d) * w])
        lbd = lb[:, d * w:(d + 1) * w]
        f = lbd + (1.0 - lbd) * jax.nn.sigmoid(z)
        lf_ref[:, d * w:(d + 1) * w] = jnp.log2(f)
        k_ref[:, d * w:(d + 1) * w] = (1.0 - f).astype(BF)
    v_ref[...] = _dot(h, w_ref[:, 4 * w:5 * w]).astype(BF)
    g_ref[...] = _dot(h, w_ref[:, 5 * w:6 * w]).astype(BF)


def _proj_even(x, mod, w_in, lb_logits, e):
    b, l, d = x.shape
    n_even = lb_logits.shape[0]
    tm = _row_tile(l, ROW_TILE)
    w = HG_W
    row = lambda width: pl.BlockSpec((None, tm, width), lambda i, t: (i, t, 0))
    shp = lambda width, dt: jax.ShapeDtypeStruct((b, l, width), dt)
    return pl.pallas_call(
        functools.partial(_proj_even_body, e, n_even),
        grid=(b, l // tm),
        in_specs=[row(d),
                  pl.BlockSpec((None, 6, d), lambda i, t: (i, 0, 0)),
                  pl.BlockSpec(w_in.shape, lambda i, t: (0, 0)),
                  pl.BlockSpec(lb_logits.shape, lambda i, t: (0, 0))],
        out_specs=[row(w), row(w), row(2 * w), row(2 * w), row(w), row(w)],
        out_shape=[shp(w, BF), shp(w, BF), shp(2 * w, F32), shp(2 * w, BF), shp(w, BF), shp(w, BF)],
        compiler_params=_cparams("parallel", "parallel"),
        name="proj_even",
    )(x, mod, w_in, lb_logits)


@functools.lru_cache(maxsize=None)
def _dft_tables(n):
    idx = np.arange(n, dtype=np.int64)
    ang = 2.0 * np.pi * ((idx[:, None] * idx[None, :]) % n).astype(np.float64) / n
    s = 1.0 / math.sqrt(n)
    return (np.cos(ang) * s).astype(np.float32), (np.sin(ang) * s).astype(np.float32)


def _dft_body(c_ref, s_ref, a_ref, cs_ref, o_ref):
    a = a_ref[...]
    p = _dot(c_ref[...], a).astype(BF)
    q = _dot(s_ref[...], a).astype(BF)
    cs = cs_ref[...]
    for g in range(FNET_GROUPS):
        sl = slice(g * LANE, (g + 1) * LANE)
        pq = jnp.concatenate([p[:, sl], q[:, sl]], axis=1)
        o_ref[:, sl] = _dot(pq, cs).astype(BF)


def _dft_mix(a):
    b, l, w = a.shape
    cl, sl_ = _dft_tables(l)
    cc, sc = _dft_tables(LANE)
    cos_l = jnp.asarray(cl).astype(BF)
    sin_l = jnp.asarray(sl_).astype(BF)
    cs = jnp.concatenate([jnp.asarray(cc), -jnp.asarray(sc)], axis=0).astype(BF)
    tm = _row_tile(l, ROW_TILE)
    return pl.pallas_call(
        _dft_body,
        grid=(l // tm, b),
        in_specs=[pl.BlockSpec((tm, l), lambda m, i: (m, 0)),
                  pl.BlockSpec((tm, l), lambda m, i: (m, 0)),
                  pl.BlockSpec((None, l, w), lambda m, i: (i, 0, 0)),
                  pl.BlockSpec(cs.shape, lambda m, i: (0, 0))],
        out_specs=pl.BlockSpec((None, tm, w), lambda m, i: (i, m, 0)),
        out_shape=jax.ShapeDtypeStruct((b, l, w), BF),
        compiler_params=_cparams("parallel", "parallel"),
        name="dft_mix",
    )(cos_l, sin_l, a, cs)


DFT_TILE = 512
DFT_EXTRA = 16


@functools.lru_cache(maxsize=None)
def _dft_half_tables(l):
    cl, sl_ = _dft_tables(l)
    t, e = DFT_TILE, DFT_EXTRA
    n = l // (2 * t)
    cos = np.stack([cl[j * t:j * t + t + e] for j in range(n)])
    sin = np.stack([sl_[j * t:j * t + t + e] for j in range(n)])
    rev = np.zeros((t, t + e), np.float32)
    rev[np.arange(t), t - np.arange(t)] = 1.0
    return cos, sin, rev


def _dft_half_body(c_ref, s_ref, a_ref, cs_ref, rev_ref, lo_ref, hi_ref):
    t = lo_ref.shape[0]
    a = a_ref[...]
    p = _dot(c_ref[...], a).astype(BF)
    q = _dot(s_ref[...], a).astype(BF)
    cs = cs_ref[...]
    mirrored = []
    for g in range(FNET_GROUPS):
        sl = slice(g * LANE, (g + 1) * LANE)
        y = _dot(jnp.concatenate([p[:, sl], q[:, sl]], axis=1), cs)
        lo_ref[:, sl] = y[:t, :LANE].astype(BF)
        mirrored.append(y[:, LANE:].astype(BF))
    hi_ref[...] = _dot(rev_ref[...], jnp.concatenate(mirrored, axis=1)).astype(BF)


def _dft_mix_half(a):
    b, l, w = a.shape
    t, e = DFT_TILE, DFT_EXTRA
    n = l // (2 * t)
    cos, sin, rev = (jnp.asarray(x).astype(BF) for x in _dft_half_tables(l))
    cc, sc = (jnp.asarray(x) for x in _dft_tables(LANE))
    cs = jnp.concatenate([jnp.concatenate([cc, -sc], axis=0), jnp.concatenate([cc, sc], axis=0)], axis=1).astype(BF)
    half = jax.ShapeDtypeStruct((b, l // 2, w), BF)
    lo, hi = pl.pallas_call(
        _dft_half_body,
        grid=(n, b),
        in_specs=[pl.BlockSpec((None, t + e, l), lambda j, i: (j, 0, 0)),
                  pl.BlockSpec((None, t + e, l), lambda j, i: (j, 0, 0)),
                  pl.BlockSpec((None, l, w), lambda j, i: (i, 0, 0)),
                  pl.BlockSpec(cs.shape, lambda j, i: (0, 0)),
                  pl.BlockSpec(rev.shape, lambda j, i: (0, 0))],
        out_specs=[pl.BlockSpec((None, t, w), lambda j, i: (i, j, 0)),
                   pl.BlockSpec((None, t, w), lambda j, i: (i, n - 1 - j, 0))],
        out_shape=[half, half],
        compiler_params=_cparams("parallel", "parallel"),
        name="dft_mix_half",
    )(cos, sin, a, cs, rev)
    return lo, hi


def _scan_chunk(rev, q_ref, k_ref, lf_ref, v_ref, o_ref, st_ref):
    c, w = lf_ref.shape
    nlev = c.bit_length() - 1
    lf = lf_ref[...]
    r = lax.broadcasted_iota(jnp.int32, (c, w), 0)
    pos = (c - 1 - r) if rev else r

    def tile_roll(x, sh):
        return pltpu.roll(x.reshape(c // SUBLANES, SUBLANES, w), sh, axis=1).reshape(c, w)

    def from_prev(x, sh):
        return tile_roll(x, (SUBLANES - sh) if rev else sh)

    def from_next(x, sh):
        return tile_roll(x, sh if rev else (SUBLANES - sh))

    def shift_tiles(x, sh):
        zero = jnp.zeros((sh, w), F32)
        return jnp.concatenate([x[sh:], zero], axis=0) if rev else jnp.concatenate([zero, x[:c - sh]], axis=0)

    def neg_abs(x):
        return pltpu.bitcast(pltpu.bitcast(x, jnp.uint32) | jnp.uint32(0x80000000), F32)

    def spread_half(x, sh, low):
        parts = []
        for i in range(c // (2 * sh)):
            first, second = x[2 * i * sh:(2 * i + 1) * sh], x[(2 * i + 1) * sh:(2 * i + 2) * sh]
            src = (second if rev else first) if low else (first if rev else second)
            parts += [src, src]
        return jnp.concatenate(parts, axis=0)

    pos8 = pos & (SUBLANES - 1)
    b = lf
    sh = 1
    while sh < SUBLANES:
        b = b + jnp.where(pos8 >= sh, from_prev(b, sh), 0.0)
        sh *= 2
    last = 0 if rev else SUBLANES - 1
    b3 = b.reshape(c // SUBLANES, SUBLANES, w)
    tot = jnp.broadcast_to(b3[:, last:last + 1, :], b3.shape).reshape(c, w)
    while sh < c:
        tot = tot + shift_tiles(tot, sh)
        sh *= 2
    b = b + shift_tiles(tot, SUBLANES)

    q = q_ref[...]
    k = k_ref[...]
    ti = lax.broadcasted_iota(jnp.int32, (c, c), 0)
    si = lax.broadcasted_iota(jnp.int32, (c, c), 1)
    later = (ti < si) if rev else (ti > si)
    diff = ti ^ si
    att = [jnp.zeros((c, c), F32) for _ in range(HG_HEADS)]
    y = b
    for j in range(nlev):
        sh = 1 << j
        upper = ((pos >> j) & 1) == 1
        bound = spread_half(y, sh, True) if sh % 8 == 0 else jnp.where(upper, from_prev(y, sh), y)
        e = jnp.exp2(neg_abs(b - bound)).astype(BF)
        qm = q * e
        km = k * e
        level = jnp.logical_and((diff >> j) == 1, later)
        for h in range(HG_HEADS):
            sl = slice(h * LANE, (h + 1) * LANE)
            att[h] = jnp.where(level, _dot_nt(qm[:, sl], km[:, sl]), att[h])
        y = spread_half(y, sh, False) if sh % 8 == 0 else jnp.where(upper, y, from_next(y, sh))
    b_last = y
    qe = q * jnp.exp2(b).astype(BF)
    kd = k * jnp.exp2(b_last - b).astype(BF)
    carry = jnp.exp2(b_last[0:1, :])
    v = v_ref[...]
    vf = v.astype(F32)
    for h in range(HG_HEADS):
        sl = slice(h * LANE, (h + 1) * LANE)
        st = st_ref[h]
        inter = _dot_nt(qe[:, sl], st.astype(BF))
        diag = jnp.sum(q[:, sl].astype(F32) * k[:, sl].astype(F32), axis=-1, keepdims=True)
        o_ref[:, sl] = (inter + _dot(att[h].astype(BF), v[:, sl]) + diag * vf[:, sl]).astype(o_ref.dtype)
        st_ref[h] = st * carry[:, sl] + _dot_tn(v[:, sl], kd[:, sl])


def _scan_body(n_ctx, *refs):
    ins, outs, states = refs[:16], refs[16:20], refs[20:]
    s = pl.program_id(1)

    @pl.when(s == 0)
    def _():
        for st_ref in states:
            st_ref[...] = jnp.zeros_like(st_ref)

    def run(stream):
        for d in range(2):
            base = 8 * d + 4 * stream
            _scan_chunk(d == 1, *ins[base:base + 4], outs[2 * d + stream], states[d])

    @pl.when(s < n_ctx)
    def _():
        run(0)

    @pl.when(s >= n_ctx)
    def _():
        run(1)


def _scan(pc, pl_):
    qc, lfc, kc, vc = pc
    ql, lfl, kl, vl = pl_
    b, lc, w = qc.shape
    ll = ql.shape[1]
    c = SCAN_CHUNK
    n_ctx, n_lat = lc // c, ll // c

    def spec(idx, col):
        return pl.BlockSpec((None, c, w), lambda i, s: (i, idx(s), col))

    in_specs, out_specs, args = [], [], []
    for d in range(2):
        if d:
            ci = lambda s: jnp.maximum(n_ctx - 1 - s, 0)
            li = lambda s: n_lat - 1 - jnp.maximum(s - n_ctx, 0)
        else:
            ci = lambda s: jnp.minimum(s, n_ctx - 1)
            li = lambda s: jnp.maximum(s - n_ctx, 0)
        in_specs += [spec(ci, 0), spec(ci, d), spec(ci, d), spec(ci, 0),
                     spec(li, 0), spec(li, d), spec(li, d), spec(li, 0)]
        args += [qc, kc, lfc, vc, ql, kl, lfl, vl]
        out_specs += [spec(ci, 0), spec(li, 0)]
    shp = lambda l: jax.ShapeDtypeStruct((b, l, w), BF)
    of_c, of_l, ob_c, ob_l = pl.pallas_call(
        functools.partial(_scan_body, n_ctx),
        grid=(b, n_ctx + n_lat),
        in_specs=in_specs,
        out_specs=out_specs,
        out_shape=[shp(lc), shp(ll), shp(lc), shp(ll)],
        scratch_shapes=[pltpu.VMEM((HG_HEADS, HEAD_DIM, HEAD_DIM), F32) for _ in range(2)],
        compiler_params=_cparams("parallel", "arbitrary"),
        name="hgrn_scan",
    )(*args)
    return (of_c, of_l), (ob_c, ob_l)


def _out_even_body(n_lo, *refs):
    yf_refs, (of_ref, ob_ref, g_ref, x_ref, mod_ref, gn_ref, w_ref, o_ref) = refs[:-8], refs[-8:]
    if n_lo:
        yf = jnp.where(pl.program_id(1) < n_lo, yf_refs[0][...], yf_refs[1][...])
    else:
        yf = yf_refs[0][...]
    o = of_ref[...].astype(F32) + ob_ref[...].astype(F32)
    g = g_ref[...].astype(F32)
    gn = gn_ref[...]
    parts = []
    for h in range(HG_HEADS):
        sl = slice(h * LANE, (h + 1) * LANE)
        parts.append((_rms(o[:, sl]) * gn[:, sl] * _silu(g[:, sl])).astype(BF))
    gated = jnp.concatenate(parts, axis=1)
    w = HG_W
    y = _dot(yf, w_ref[0:w, :]) + _dot(gated, w_ref[w:2 * w, :])
    o_ref[...] = x_ref[...] + mod_ref[2:3, :] * y


def _out_even(yf, o_f, o_b, g, x, mod, gn, w_out):
    b, l, d = x.shape
    tm = _row_tile(l, ROW_TILE)
    row = lambda width: pl.BlockSpec((None, tm, width), lambda i, t: (i, t, 0))
    if isinstance(yf, tuple):
        assert yf[0].shape[1] % tm == 0
        n_lo = yf[0].shape[1] // tm
        yf_specs = [pl.BlockSpec((None, tm, HG_W), lambda i, t: (i, jnp.minimum(t, n_lo - 1), 0)),
                    pl.BlockSpec((None, tm, HG_W), lambda i, t: (i, jnp.maximum(t - n_lo, 0), 0))]
    else:
        n_lo, yf, yf_specs = 0, (yf,), [row(HG_W)]
    return pl.pallas_call(
        functools.partial(_out_even_body, n_lo),
        grid=(b, l // tm),
        in_specs=yf_specs + [row(HG_W), row(HG_W), row(HG_W), row(d),
                  pl.BlockSpec((None, 6, d), lambda i, t: (i, 0, 0)),
                  pl.BlockSpec(gn.shape, lambda i, t: (0, 0)),
                  pl.BlockSpec(w_out.shape, lambda i, t: (0, 0))],
        out_specs=row(d),
        out_shape=jax.ShapeDtypeStruct((b, l, d), F32),
        compiler_params=_cparams("parallel", "parallel"),
        name="out_even",
    )(*yf, o_f, o_b, g, x, mod, gn, w_out)


def _proj_att_body(x_ref, mod_ref, w_ref, wvt_ref, qg_ref, kg_ref, cos_ref, sin_ref, q_ref, k_ref, vt_ref):
    mod = mod_ref[...]
    h = _modulate(x_ref[...], mod[0:1], mod[1:2])
    cos = cos_ref[...]
    sin = sin_ref[...]
    scale = HEAD_DIM ** -0.5 * math.log2(math.e)
    for i in range(ATT_HEADS + ATT_KV_HEADS):
        sl = slice(i * LANE, (i + 1) * LANE)
        p = _dot(h, w_ref[:, sl])
        gain = qg_ref[...] if i < ATT_HEADS else kg_ref[...]
        p = _rms(p) * gain
        p = p * cos + pltpu.roll(p, HEAD_DIM // 2, axis=1) * sin
        if i < ATT_HEADS:
            q_ref[:, sl] = (p * scale).astype(BF)
        else:
            k_ref[:, (i - ATT_HEADS) * LANE:(i - ATT_HEADS + 1) * LANE] = p.astype(BF)
    vt = _dot_nt(wvt_ref[...], h)
    for hk in range(ATT_KV_HEADS):
        vt_ref[hk, 0:HEAD_DIM, :] = vt[hk * LANE:(hk + 1) * LANE].astype(BF)
        vt_ref[hk, HEAD_DIM:, :] = jnp.ones((ATT_ONES_ROWS, vt.shape[1]), BF)


def _proj_att(x, mod, w_qk, w_vt, qg, kg, cos, sin):
    b, l, d = x.shape
    tm = _row_tile(l, PROJ_ATT_TILE)
    qw, kw = ATT_HEADS * LANE, ATT_KV_HEADS * LANE
    vt_rows = HEAD_DIM + ATT_ONES_ROWS
    row = lambda width: pl.BlockSpec((None, tm, width), lambda i, t: (i, t, 0))
    const = lambda a: pl.BlockSpec(a.shape, lambda i, t: (0, 0))
    tab = pl.BlockSpec((tm, LANE), lambda i, t: (t, 0))
    return pl.pallas_call(
        _proj_att_body,
        grid=(b, l // tm),
        in_specs=[row(d), pl.BlockSpec((None, 6, d), lambda i, t: (i, 0, 0)),
                  const(w_qk), const(w_vt), const(qg), const(kg), tab, tab],
        out_specs=[row(qw), row(kw),
                   pl.BlockSpec((None, ATT_KV_HEADS, vt_rows, tm), lambda i, t: (i, 0, 0, t))],
        out_shape=[jax.ShapeDtypeStruct((b, l, qw), BF), jax.ShapeDtypeStruct((b, l, kw), BF),
                   jax.ShapeDtypeStruct((b, ATT_KV_HEADS, vt_rows, l), BF)],
        compiler_params=_cparams("parallel", "parallel"),
        name="proj_att",
    )(x, mod, w_qk, w_vt, qg, kg, cos, sin)


@functools.lru_cache(maxsize=None)
def _rope_tables(l):
    t = np.arange(l)
    row = (t // GRID_W).astype(np.float64)
    col = (t % GRID_W).astype(np.float64)
    n_freq = HEAD_DIM // 4
    freqs = ROPE_THETA ** (-np.arange(n_freq, dtype=np.float64) / n_freq)
    ang = np.concatenate([row[:, None] * freqs, col[:, None] * freqs], axis=-1)
    cos, sin = np.cos(ang), np.sin(ang)
    return (np.concatenate([cos, cos], axis=-1).astype(np.float32),
            np.concatenate([-sin, sin], axis=-1).astype(np.float32))


ATT_KV_BLOCK = 256
ATT_ONES_ROWS = 16
ATT_Q_TILE = 512


def _attn_body(n_src, q_ref, *refs):
    o_ref = refs[2 * n_src]
    tq = q_ref.shape[0]
    n = ATT_GROUP * tq
    vt_rows = refs[1].shape[0]
    q = q_ref[...]
    qs = jnp.concatenate([q[:, g * LANE:(g + 1) * LANE] for g in range(ATT_GROUP)], axis=0)
    blocks = []
    for i in range(n_src):
        k_ref, vt_ref = refs[2 * i], refs[2 * i + 1]
        lk = k_ref.shape[0]
        bk = min(ATT_KV_BLOCK, lk)
        blocks += [(k_ref, vt_ref, j * bk, bk) for j in range(lk // bk)]
    to_score = iter(blocks)
    queue = []

    def refill():
        while len(queue) < 2:
            blk = next(to_score, None)
            if blk is None:
                return
            k_ref, _, off, bk = blk
            queue.append(_dot_nt(k_ref[off:off + bk, :], qs))

    m = jnp.full((1, n), -1e30, F32)
    acc = jnp.zeros((vt_rows, n), F32)
    def accumulate(acc, blk, alpha, p):
        _, vt_ref, off, bk = blk
        return alpha * acc + _dot(vt_ref[:, off:off + bk], p)

    refill()
    pending = None
    for j in range(len(blocks)):
        s = queue.pop(0)
        refill()
        m_new = jnp.maximum(m, jnp.max(s, axis=0, keepdims=True))
        alpha = jnp.exp2(m - m_new)
        p = jnp.exp2((s - m_new).astype(BF))
        m = m_new
        if pending is not None:
            acc = accumulate(acc, *pending)
        pending = (blocks[j], alpha, p)
    acc = accumulate(acc, *pending)

    o = acc[:HEAD_DIM] * (1.0 / acc[HEAD_DIM:HEAD_DIM + 1])
    for g in range(ATT_GROUP):
        o_ref[:, g * LANE:(g + 1) * LANE] = o[:, g * tq:(g + 1) * tq].T.astype(BF)


def _attention(q, kvs, tq):
    b, lq, qw = q.shape
    gw = ATT_GROUP * LANE
    in_specs = [pl.BlockSpec((None, tq, gw), lambda i, h, t: (i, t, h))]
    args = [q]
    for k, vt in kvs:
        lk = k.shape[1]
        in_specs.append(pl.BlockSpec((None, lk, LANE), lambda i, h, t: (i, 0, h)))
        in_specs.append(pl.BlockSpec((None, None, vt.shape[2], lk), lambda i, h, t: (i, h, 0, 0)))
        args += [k, vt]
    return pl.pallas_call(
        functools.partial(_attn_body, len(kvs)),
        grid=(b, ATT_KV_HEADS, lq // tq),
        in_specs=in_specs,
        out_specs=pl.BlockSpec((None, tq, gw), lambda i, h, t: (i, t, h)),
        out_shape=jax.ShapeDtypeStruct((b, lq, qw), BF),
        compiler_params=_cparams("parallel", "parallel", "parallel"),
        name="attention",
    )(*args)


def _out_att_body(o_ref, x_ref, mod_ref, w_ref, y_ref):
    y_ref[...] = x_ref[...] + mod_ref[2:3, :] * _dot(o_ref[...], w_ref[...])


def _out_att(o, x, mod, w_out):
    b, l, d = x.shape
    tm = _row_tile(l, ROW_TILE)
    row = lambda width: pl.BlockSpec((None, tm, width), lambda i, t: (i, t, 0))
    return pl.pallas_call(
        _out_att_body,
        grid=(b, l // tm),
        in_specs=[row(o.shape[-1]), row(d), pl.BlockSpec((None, 6, d), lambda i, t: (i, 0, 0)),
                  pl.BlockSpec(w_out.shape, lambda i, t: (0, 0))],
        out_specs=row(d),
        out_shape=jax.ShapeDtypeStruct((b, l, d), F32),
        compiler_params=_cparams("parallel", "parallel"),
        name="out_att",
    )(o, x, mod, w_out)


def _ffn_body(on_grid, ck, has_final, *refs):
    if on_grid:
        xp_ref, xm_ref, xn_ref = refs[:3]
        refs = refs[3:]
    else:
        xm_ref = refs[0]
        refs = refs[1:]
    mod_ref, wu_ref, cw_ref, cb_ref, wd_ref = refs[:5]
    fg_ref = refs[5] if has_final else None
    o_ref, h_buf = refs[-2:]
    tm = xm_ref.shape[0]
    d_ff = wd_ref.shape[0]
    bounds = list(range(0, d_ff, ck)) + [d_ff]
    n_chunk = len(bounds) - 1
    cols = lambda c, base=0: slice(base + bounds[c], base + bounds[c + 1])
    halo = GRID_W if on_grid else 0
    rows = tm + 2 * halo
    mod = mod_ref[...]
    md = lambda x: _modulate(x, mod[3:4], mod[4:5])
    if on_grid:
        h_buf[0:halo, :] = md(xp_ref[...])
        h_buf[halo:halo + tm, :] = md(xm_ref[...])
        h_buf[halo + tm:rows, :] = md(xn_ref[...])
    else:
        h_buf[...] = md(xm_ref[...])
    if on_grid:
        t = pl.program_id(1)
        nt = pl.num_programs(1)

    @functools.lru_cache(maxsize=None)
    def neighbour_masks(width):
        r = lax.broadcasted_iota(jnp.int32, (rows, width), 0)
        if on_grid:
            col = r & (GRID_W - 1)
            return col != 0, col != GRID_W - 1
        return r != 0, r != rows - 1

    hm_sl = slice(halo, halo + tm)

    def up(c):
        gate = _dot(h_buf[...], wu_ref[:, cols(c)])
        val = _dot(h_buf[hm_sl, :], wu_ref[:, cols(c, d_ff)])
        return gate, val

    def down(c, gate, val):
        if on_grid:
            gate = jnp.concatenate([jnp.where(t > 0, gate[:halo], 0.0), gate[halo:halo + tm],
                                    jnp.where(t < nt - 1, gate[halo + tm:], 0.0)], axis=0)
        has_left, has_right = neighbour_masks(gate.shape[1])
        gl = jnp.where(has_left, pltpu.roll(gate, 1, axis=0), 0.0)
        gr = jnp.where(has_right, pltpu.roll(gate, rows - 1, axis=0), 0.0)
        cw = cw_ref[:, cols(c)]
        if on_grid:
            conv = None
            for dy in range(3):
                sl = slice(dy * halo, dy * halo + tm)
                term = (cw[3 * dy:3 * dy + 1] * gl[sl] + cw[3 * dy + 1:3 * dy + 2] * gate[sl]
                        + cw[3 * dy + 2:3 * dy + 3] * gr[sl])
                conv = term if conv is None else conv + term
        else:
            conv = cw[3:4] * gl + cw[4:5] * gate + cw[5:6] * gr
        return (_silu(conv + cb_ref[:, cols(c)]) * val).astype(BF)

    nxt = up(0)
    acc = None
    acts = []
    for c in range(n_chunk):
        cur = nxt
        if c + 1 < n_chunk:
            nxt = up(c + 1)
        acts.append(down(c, *cur))
        if len(acts) == FFN_DOWN_GROUP or c + 1 == n_chunk:
            c0 = c + 1 - len(acts)
            part = _dot(jnp.concatenate(acts, axis=1), wd_ref[bounds[c0]:bounds[c + 1], :])
            acc = part if acc is None else acc + part
            acts = []
    y = xm_ref[...] + mod[5:6] * acc
    if has_final:
        y = _rms(y) * fg_ref[...]
    o_ref[...] = y


def _ffn(x, mod, wu, cw, cb, wd, on_grid, final_g=None):
    b, l, d = x.shape
    ck = min(FFN_CHUNK, wd.shape[0])
    extra = [] if final_g is None else [final_g.reshape(1, d)]
    const = lambda a: pl.BlockSpec(a.shape, lambda i, t: (0,) * a.ndim, pipeline_mode=pl.Buffered(1))
    modspec = pl.BlockSpec((None, 6, d), lambda i, t: (i, 0, 0))
    if on_grid:
        tm = FFN_TILE
        assert l % tm == 0 and tm % GRID_W == 0
        per = tm // GRID_W
        n_rows = l // GRID_W
        x_specs = [pl.BlockSpec((None, GRID_W, d), lambda i, t: (i, jnp.maximum(t * per - 1, 0), 0)),
                   pl.BlockSpec((None, tm, d), lambda i, t: (i, t, 0)),
                   pl.BlockSpec((None, GRID_W, d), lambda i, t: (i, jnp.minimum((t + 1) * per, n_rows - 1), 0))]
        xs = [x, x, x]
        rows = tm + 2 * GRID_W
    else:
        tm = l
        x_specs = [pl.BlockSpec((None, tm, d), lambda i, t: (i, t, 0))]
        xs = [x]
        rows = tm
    return pl.pallas_call(
        functools.partial(_ffn_body, on_grid, ck, final_g is not None),
        grid=(b, l // tm),
        in_specs=(x_specs + [modspec, const(wu), const(cw), const(cb), const(wd)]
                  + [const(a) for a in extra]),
        out_specs=pl.BlockSpec((None, tm, d), lambda i, t: (i, t, 0)),
        out_shape=jax.ShapeDtypeStruct((b, l, d), F32),
        scratch_shapes=[pltpu.VMEM((rows, d), BF)],
        compiler_params=_cparams("parallel", "parallel"),
        name="ffn_grid" if on_grid else "ffn_seq",
    )(*xs, mod, wu, cw, cb, wd, *extra)


def _even_mixer(x, ctx, ml, mc, w_in, w_out, lb_logits, gn, e, need_ctx):
    w_in = w_in.astype(BF)
    w_out = w_out.astype(BF)
    gn = gn.reshape(1, -1)
    lbl = lb_logits.reshape(lb_logits.shape[0], -1).astype(F32)
    a_l, q_l, lf_l, k_l, v_l, g_l = _proj_even(x, ml, w_in, lbl, e)
    a_c, q_c, lf_c, k_c, v_c, g_c = _proj_even(ctx, mc, w_in, lbl, e)
    pc, pl_ = (q_c, lf_c, k_c, v_c), (q_l, lf_l, k_l, v_l)
    (of_c, of_l), (ob_c, ob_l) = _scan(pc, pl_)
    mix_l = _dft_mix_half if x.shape[1] % (2 * max(DFT_TILE, ROW_TILE)) == 0 else _dft_mix
    x = _out_even(mix_l(a_l), of_l, ob_l, g_l, x, ml, gn, w_out)
    if need_ctx:
        ctx = _out_even(_dft_mix(a_c), of_c, ob_c, g_c, ctx, mc, gn, w_out)
    return x, ctx


def _att_mixer(x, ctx, ml, mc, w_qkv, qn_g, kn_g, w_out, need_ctx):
    l, lc = x.shape[1], ctx.shape[1]
    perm = np.concatenate([np.arange(0, HEAD_DIM, 2), np.arange(1, HEAD_DIM, 2)])
    n_qk = ATT_HEADS + ATT_KV_HEADS
    cols = np.concatenate([h * HEAD_DIM + perm for h in range(n_qk)])
    w_qk = w_qkv[:, cols].astype(BF)
    w_vt = w_qkv[:, n_qk * HEAD_DIM:].T.astype(BF)
    w_out = w_out.astype(BF)
    qg = qn_g[perm].reshape(1, -1)
    kg = kn_g[perm].reshape(1, -1)
    cos_l, sin_l = (jnp.asarray(t) for t in _rope_tables(l))
    cos_c, sin_c = jnp.ones((lc, HEAD_DIM), F32), jnp.zeros((lc, HEAD_DIM), F32)
    q_l, k_l, vt_l = _proj_att(x, ml, w_qk, w_vt, qg, kg, cos_l, sin_l)
    q_c, k_c, vt_c = _proj_att(ctx, mc, w_qk, w_vt, qg, kg, cos_c, sin_c)
    x = _out_att(_attention(q_l, [(k_l, vt_l), (k_c, vt_c)], min(l, ATT_Q_TILE)), x, ml, w_out)
    if need_ctx:
        ctx = _out_att(_attention(q_c, [(k_c, vt_c)], lc), ctx, mc, w_out)
    return x, ctx


def _ffn_layer(x, ctx, ml, mc, w_up, conv_w, conv_b, w_down, need_ctx, final_g=None):
    d = x.shape[-1]
    d_ff = w_down.shape[0]
    wu = w_up.astype(BF)
    cw = conv_w.reshape(9, d_ff)
    cb = conv_b.reshape(1, d_ff)
    wd = w_down.astype(BF)
    x = _ffn(x, ml, wu, cw, cb, wd, True, final_g)
    if need_ctx:
        ctx = _ffn(ctx, mc, wu, cw, cb, wd, False)
    return x, ctx


def _mod_vectors(c, c_ctx, w_mod, b_mod):
    b, d = c.shape
    depth = w_mod.shape[0]
    pad = (-(b + 1)) % 8
    cc = jnp.concatenate([c, c_ctx[None, :], jnp.zeros((pad, d), F32)], axis=0)
    mods = _modulation(cc, w_mod, b_mod)
    mod_lat = mods[:, :b].reshape(depth, b, 6, d)
    mod_ctx = jnp.broadcast_to(mods[:, b].reshape(depth, 1, 6, d), (depth, b, 6, d))
    return mod_lat, mod_ctx


def kernel(x, c, ctx, c_ctx, w_mod, b_mod, w_in_ab, w_out_ab, hg_lb_logits, hg_norm_g, w_qkv, q_norm_g,
           k_norm_g, w_out_att, w_up, conv_w, conv_b, w_down, final_norm_g):
    depth = w_mod.shape[0]
    mod_lat, mod_ctx = _mod_vectors(c, c_ctx, w_mod, b_mod)
    for layer in range(depth):
        need_ctx = layer != depth - 1
        ml, mc = mod_lat[layer], mod_ctx[layer]
        if layer % 2 == 0:
            e = layer // 2
            x, ctx = _even_mixer(x, ctx, ml, mc, w_in_ab[e], w_out_ab[e], hg_lb_logits, hg_norm_g[e], e, need_ctx)
        else:
            o = layer // 2
            x, ctx = _att_mixer(x, ctx, ml, mc, w_qkv[o], q_norm_g[o], k_norm_g[o], w_out_att[o], need_ctx)
        x, ctx = _ffn_layer(x, ctx, ml, mc, w_up[layer], conv_w[layer], conv_b[layer], w_down[layer], need_ctx,
                            None if need_ctx else final_norm_g)
    return x
```

```python
import functools
import math

import numpy as np
import jax
import jax.numpy as jnp
from jax import lax
from jax.experimental import pallas as pl
from jax.experimental.pallas import tpu as pltpu

F32 = jnp.float32
BF = jnp.bfloat16

EPS = 1e-6
GRID_W = 64
LANE = 128
SUBLANES = 8
HEAD_DIM = 128
FNET_GROUPS = 4
HG_HEADS = 4
HG_W = HG_HEADS * HEAD_DIM
ATT_HEADS = 8
ATT_KV_HEADS = 2
ATT_GROUP = ATT_HEADS // ATT_KV_HEADS
ROPE_THETA = 10000.0
SCAN_CHUNK = 128
ROW_TILE = 1024
FFN_TILE = 512
PROJ_ATT_TILE = 512
FFN_CHUNK = 256
FFN_DOWN_GROUP = 6
VMEM_LIMIT = 56 * 1024 * 1024


def _cparams(*sem):
    return pltpu.CompilerParams(dimension_semantics=sem, vmem_limit_bytes=VMEM_LIMIT)


def _dot(a, b):
    return jnp.dot(a, b, preferred_element_type=F32)


def _dot_nt(a, b):
    return lax.dot_general(a, b, (((1,), (1,)), ((), ())), preferred_element_type=F32)


def _dot_tn(a, b):
    return lax.dot_general(a, b, (((0,), (0,)), ((), ())), preferred_element_type=F32)


def _rms(x):
    return x * lax.rsqrt(jnp.mean(x * x, axis=-1, keepdims=True) + EPS)


def _silu(x):
    return x * jax.nn.sigmoid(x)


def _modulate(x, shift, scale):
    return (_rms(x) * (1.0 + scale) + shift).astype(BF)


def _row_tile(n, pref):
    return pref if n % pref == 0 else n


def _mod_body(c_ref, w_ref, b_ref, o_ref):
    s = _silu(c_ref[...]).astype(BF)
    o_ref[...] = _dot(s, w_ref[...].astype(BF)) + b_ref[...]


def _modulation(cc, w_mod, b_mod):
    depth, d, six_d = w_mod.shape
    r = cc.shape[0]
    return pl.pallas_call(
        _mod_body,
        grid=(depth, six_d // d),
        in_specs=[pl.BlockSpec((r, d), lambda l, j: (0, 0)),
                  pl.BlockSpec((None, d, d), lambda l, j: (l, 0, j)),
                  pl.BlockSpec((None, 1, d), lambda l, j: (l, 0, j))],
        out_specs=pl.BlockSpec((None, r, d), lambda l, j: (l, 0, j)),
        out_shape=jax.ShapeDtypeStruct((depth, r, six_d), F32),
        compiler_params=_cparams("parallel", "parallel"),
        name="modulation",
    )(cc, w_mod, b_mod.reshape(depth, 1, six_d))


def _proj_even_body(e, n_even, x_ref, mod_ref, w_ref, lbl_ref, a_ref, q_ref, lf_ref, k_ref, v_ref, g_ref):
    mod = mod_ref[...]
    h = _modulate(x_ref[...], mod[0:1], mod[1:2])
    rows = [lbl_ref[i:i + 1, :] for i in range(n_even)]
    mx = functools.reduce(jnp.maximum, rows)
    ex = [jnp.exp(r - mx) for r in rows]
    den = functools.reduce(lambda a, b: a + b, ex)
    lb = jnp.zeros_like(mx)
    for i in range(1, e + 1):
        lb = lb + ex[i] / den
    w = HG_W
    a_ref[...] = _dot(h, w_ref[:, 0:w]).astype(BF)
    q_ref[...] = _silu(_dot(h, w_ref[:, w:2 * w])).astype(BF)
    for d in range(2):
        z = _dot(h, w_ref[:, (2 + d) * w:(3 + d) * w])
        lbd = lb[:, d * w:(d + 1) * w]
        f = lbd + (1.0 - lbd) * jax.nn.sigmoid(z)
        lf_ref[:, d * w:(d + 1) * w] = jnp.log2(f)
        k_ref[:, d * w:(d + 1) * w] = (1.0 - f).astype(BF)
    v_ref[...] = _dot(h, w_ref[:, 4 * w:5 * w]).astype(BF)
    g_ref[...] = _dot(h, w_ref[:, 5 * w:6 * w]).astype(BF)


def _proj_even(x, mod, w_in, lb_logits, e):
    b, l, d = x.shape
    n_even = lb_logits.shape[0]
    tm = _row_tile(l, ROW_TILE)
    w = HG_W
    row = lambda width: pl.BlockSpec((None, tm, width), lambda i, t: (i, t, 0))
    shp = lambda width, dt: jax.ShapeDtypeStruct((b, l, width), dt)
    return pl.pallas_call(
        functools.partial(_proj_even_body, e, n_even),
        grid=(b, l // tm),
        in_specs=[row(d),
                  pl.BlockSpec((None, 6, d), lambda i, t: (i, 0, 0)),
                  pl.BlockSpec(w_in.shape, lambda i, t: (0, 0)),
                  pl.BlockSpec(lb_logits.shape, lambda i, t: (0, 0))],
        out_specs=[row(w), row(w), row(2 * w), row(2 * w), row(w), row(w)],
        out_shape=[shp(w, BF), shp(w, BF), shp(2 * w, F32), shp(2 * w, BF), shp(w, BF), shp(w, BF)],
        compiler_params=_cparams("parallel", "parallel"),
        name="proj_even",
    )(x, mod, w_in, lb_logits)


@functools.lru_cache(maxsize=None)
def _dft_tables(n):
    idx = np.arange(n, dtype=np.int64)
    ang = 2.0 * np.pi * ((idx[:, None] * idx[None, :]) % n).astype(np.float64) / n
    s = 1.0 / math.sqrt(n)
    return (np.cos(ang) * s).astype(np.float32), (np.sin(ang) * s).astype(np.float32)


def _dft_body(c_ref, s_ref, a_ref, cs_ref, o_ref):
    a = a_ref[...]
    p = _dot(c_ref[...], a).astype(BF)
    q = _dot(s_ref[...], a).astype(BF)
    cs = cs_ref[...]
    for g in range(FNET_GROUPS):
        sl = slice(g * LANE, (g + 1) * LANE)
        pq = jnp.concatenate([p[:, sl], q[:, sl]], axis=1)
        o_ref[:, sl] = _dot(pq, cs).astype(BF)


def _dft_mix(a):
    b, l, w = a.shape
    cl, sl_ = _dft_tables(l)
    cc, sc = _dft_tables(LANE)
    cos_l = jnp.asarray(cl).astype(BF)
    sin_l = jnp.asarray(sl_).astype(BF)
    cs = jnp.concatenate([jnp.asarray(cc), -jnp.asarray(sc)], axis=0).astype(BF)
    tm = _row_tile(l, ROW_TILE)
    return pl.pallas_call(
        _dft_body,
        grid=(l // tm, b),
        in_specs=[pl.BlockSpec((tm, l), lambda m, i: (m, 0)),
                  pl.BlockSpec((tm, l), lambda m, i: (m, 0)),
                  pl.BlockSpec((None, l, w), lambda m, i: (i, 0, 0)),
                  pl.BlockSpec(cs.shape, lambda m, i: (0, 0))],
        out_specs=pl.BlockSpec((None, tm, w), lambda m, i: (i, m, 0)),
        out_shape=jax.ShapeDtypeStruct((b, l, w), BF),
        compiler_params=_cparams("parallel", "parallel"),
        name="dft_mix",
    )(cos_l, sin_l, a, cs)


DFT_TILE = 512
DFT_EXTRA = 16


@functools.lru_cache(maxsize=None)
def _dft_half_tables(l):
    cl, sl_ = _dft_tables(l)
    t, e = DFT_TILE, DFT_EXTRA
    n = l // (2 * t)
    cos = np.stack([cl[j * t:j * t + t + e] for j in range(n)])
    sin = np.stack([sl_[j * t:j * t + t + e] for j in range(n)])
    rev = np.zeros((t, t + e), np.float32)
    rev[np.arange(t), t - np.arange(t)] = 1.0
    return cos, sin, rev


def _dft_half_body(c_ref, s_ref, a_ref, cs_ref, rev_ref, lo_ref, hi_ref):
    t = lo_ref.shape[0]
    a = a_ref[...]
    p = _dot(c_ref[...], a).astype(BF)
    q = _dot(s_ref[...], a).astype(BF)
    cs = cs_ref[...]
    mirrored = []
    for g in range(FNET_GROUPS):
        sl = slice(g * LANE, (g + 1) * LANE)
        y = _dot(jnp.concatenate([p[:, sl], q[:, sl]], axis=1), cs)
        lo_ref[:, sl] = y[:t, :LANE].astype(BF)
        mirrored.append(y[:, LANE:].astype(BF))
    hi_ref[...] = _dot(rev_ref[...], jnp.concatenate(mirrored, axis=1)).astype(BF)


def _dft_mix_half(a):
    b, l, w = a.shape
    t, e = DFT_TILE, DFT_EXTRA
    n = l // (2 * t)
    cos, sin, rev = (jnp.asarray(x).astype(BF) for x in _dft_half_tables(l))
    cc, sc = (jnp.asarray(x) for x in _dft_tables(LANE))
    cs = jnp.concatenate([jnp.concatenate([cc, -sc], axis=0), jnp.concatenate([cc, sc], axis=0)], axis=1).astype(BF)
    half = jax.ShapeDtypeStruct((b, l // 2, w), BF)
    lo, hi = pl.pallas_call(
        _dft_half_body,
        grid=(n, b),
        in_specs=[pl.BlockSpec((None, t + e, l), lambda j, i: (j, 0, 0)),
                  pl.BlockSpec((None, t + e, l), lambda j, i: (j, 0, 0)),
                  pl.BlockSpec((None, l, w), lambda j, i: (i, 0, 0)),
                  pl.BlockSpec(cs.shape, lambda j, i: (0, 0)),
                  pl.BlockSpec(rev.shape, lambda j, i: (0, 0))],
        out_specs=[pl.BlockSpec((None, t, w), lambda j, i: (i, j, 0)),
                   pl.BlockSpec((None, t, w), lambda j, i: (i, n - 1 - j, 0))],
        out_shape=[half, half],
        compiler_params=_cparams("parallel", "parallel"),
        name="dft_mix_half",
    )(cos, sin, a, cs, rev)
    return lo, hi


def _scan_chunk(rev, q_ref, k_ref, lf_ref, v_ref, o_ref, st_ref):
    c, w = lf_ref.shape
    nlev = c.bit_length() - 1
    lf = lf_ref[...]
    r = lax.broadcasted_iota(jnp.int32, (c, w), 0)
    pos = (c - 1 - r) if rev else r

    def tile_roll(x, sh):
        return pltpu.roll(x.reshape(c // SUBLANES, SUBLANES, w), sh, axis=1).reshape(c, w)

    def from_prev(x, sh):
        return tile_roll(x, (SUBLANES - sh) if rev else sh)

    def from_next(x, sh):
        return tile_roll(x, sh if rev else (SUBLANES - sh))

    def shift_tiles(x, sh):
        zero = jnp.zeros((sh, w), F32)
        return jnp.concatenate([x[sh:], zero], axis=0) if rev else jnp.concatenate([zero, x[:c - sh]], axis=0)

    def neg_abs(x):
        return pltpu.bitcast(pltpu.bitcast(x, jnp.uint32) | jnp.uint32(0x80000000), F32)

    def spread_half(x, sh, low):
        parts = []
        for i in range(c // (2 * sh)):
            first, second = x[2 * i * sh:(2 * i + 1) * sh], x[(2 * i + 1) * sh:(2 * i + 2) * sh]
            src = (second if rev else first) if low else (first if rev else second)
            parts += [src, src]
        return jnp.concatenate(parts, axis=0)

    pos8 = pos & (SUBLANES - 1)
    b = lf
    sh = 1
    while sh < SUBLANES:
        b = b + jnp.where(pos8 >= sh, from_prev(b, sh), 0.0)
        sh *= 2
    last = 0 if rev else SUBLANES - 1
    b3 = b.reshape(c // SUBLANES, SUBLANES, w)
    tot = jnp.broadcast_to(b3[:, last:last + 1, :], b3.shape).reshape(c, w)
    while sh < c:
        tot = tot + shift_tiles(tot, sh)
        sh *= 2
    b = b + shift_tiles(tot, SUBLANES)

    q = q_ref[...]
    k = k_ref[...]
    ti = lax.broadcasted_iota(jnp.int32, (c, c), 0)
    si = lax.broadcasted_iota(jnp.int32, (c, c), 1)
    later = (ti < si) if rev else (ti > si)
    diff = ti ^ si
    att = [jnp.zeros((c, c), F32) for _ in range(HG_HEADS)]
    y = b
    for j in range(nlev):
        sh = 1 << j
        upper = ((pos >> j) & 1) == 1
        bound = spread_half(y, sh, True) if sh % 8 == 0 else jnp.where(upper, from_prev(y, sh), y)
        e = jnp.exp2(neg_abs(b - bound)).astype(BF)
        qm = q * e
        km = k * e
        level = jnp.logical_and((diff >> j) == 1, later)
        for h in range(HG_HEADS):
            sl = slice(h * LANE, (h + 1) * LANE)
            att[h] = jnp.where(level, _dot_nt(qm[:, sl], km[:, sl]), att[h])
        y = spread_half(y, sh, False) if sh % 8 == 0 else jnp.where(upper, y, from_next(y, sh))
    b_last = y
    qe = q * jnp.exp2(b).astype(BF)
    kd = k * jnp.exp2(b_last - b).astype(BF)
    carry = jnp.exp2(b_last[0:1, :])
    v = v_ref[...]
    vf = v.astype(F32)
    for h in range(HG_HEADS):
        sl = slice(h * LANE, (h + 1) * LANE)
        st = st_ref[h]
        inter = _dot_nt(qe[:, sl], st.astype(BF))
        diag = jnp.sum(q[:, sl].astype(F32) * k[:, sl].astype(F32), axis=-1, keepdims=True)
        o_ref[:, sl] = (inter + _dot(att[h].astype(BF), v[:, sl]) + diag * vf[:, sl]).astype(o_ref.dtype)
        st_ref[h] = st * carry[:, sl] + _dot_tn(v[:, sl], kd[:, sl])


def _scan_body(n_ctx, *refs):
    ins, outs, states = refs[:16], refs[16:20], refs[20:]
    s = pl.program_id(1)

    @pl.when(s == 0)
    def _():
        for st_ref in states:
            st_ref[...] = jnp.zeros_like(st_ref)

    def run(stream):
        for d in range(2):
            base = 8 * d + 4 * stream
            _scan_chunk(d == 1, *ins[base:base + 4], outs[2 * d + stream], states[d])

    @pl.when(s < n_ctx)
    def _():
        run(0)

    @pl.when(s >= n_ctx)
    def _():
        run(1)


def _scan(pc, pl_):
    qc, lfc, kc, vc = pc
    ql, lfl, kl, vl = pl_
    b, lc, w = qc.shape
    ll = ql.shape[1]
    c = SCAN_CHUNK
    n_ctx, n_lat = lc // c, ll // c

    def spec(idx, col):
        return pl.BlockSpec((None, c, w), lambda i, s: (i, idx(s), col))

    in_specs, out_specs, args = [], [], []
    for d in range(2):
        if d:
            ci = lambda s: jnp.maximum(n_ctx - 1 - s, 0)
            li = lambda s: n_lat - 1 - jnp.maximum(s - n_ctx, 0)
        else:
            ci = lambda s: jnp.minimum(s, n_ctx - 1)
            li = lambda s: jnp.maximum(s - n_ctx, 0)
        in_specs += [spec(ci, 0), spec(ci, d), spec(ci, d), spec(ci, 0),
                     spec(li, 0), spec(li, d), spec(li, d), spec(li, 0)]
        args += [qc, kc, lfc, vc, ql, kl, lfl, vl]
        out_specs += [spec(ci, 0), spec(li, 0)]
    shp = lambda l: jax.ShapeDtypeStruct((b, l, w), BF)
    of_c, of_l, ob_c, ob_l = pl.pallas_call(
        functools.partial(_scan_body, n_ctx),
        grid=(b, n_ctx + n_lat),
        in_specs=in_specs,
        out_specs=out_specs,
        out_shape=[shp(lc), shp(ll), shp(lc), shp(ll)],
        scratch_shapes=[pltpu.VMEM((HG_HEADS, HEAD_DIM, HEAD_DIM), F32) for _ in range(2)],
        compiler_params=_cparams("parallel", "arbitrary"),
        name="hgrn_scan",
    )(*args)
    return (of_c, of_l), (ob_c, ob_l)


def _out_even_body(n_lo, *refs):
    yf_refs, (of_ref, ob_ref, g_ref, x_ref, mod_ref, gn_ref, w_ref, o_ref) = refs[:-8], refs[-8:]
    if n_lo:
        yf = jnp.where(pl.program_id(1) < n_lo, yf_refs[0][...], yf_refs[1][...])
    else:
        yf = yf_refs[0][...]
    o = of_ref[...].astype(F32) + ob_ref[...].astype(F32)
    g = g_ref[...].astype(F32)
    gn = gn_ref[...]
    parts = []
    for h in range(HG_HEADS):
        sl = slice(h * LANE, (h + 1) * LANE)
        parts.append((_rms(o[:, sl]) * gn[:, sl] * _silu(g[:, sl])).astype(BF))
    gated = jnp.concatenate(parts, axis=1)
    w = HG_W
    y = _dot(yf, w_ref[0:w, :]) + _dot(gated, w_ref[w:2 * w, :])
    o_ref[...] = x_ref[...] + mod_ref[2:3, :] * y


def _out_even(yf, o_f, o_b, g, x, mod, gn, w_out):
    b, l, d = x.shape
    tm = _row_tile(l, ROW_TILE)
    row = lambda width: pl.BlockSpec((None, tm, width), lambda i, t: (i, t, 0))
    if isinstance(yf, tuple):
        assert yf[0].shape[1] % tm == 0
        n_lo = yf[0].shape[1] // tm
        yf_specs = [pl.BlockSpec((None, tm, HG_W), lambda i, t: (i, jnp.minimum(t, n_lo - 1), 0)),
                    pl.BlockSpec((None, tm, HG_W), lambda i, t: (i, jnp.maximum(t - n_lo, 0), 0))]
    else:
        n_lo, yf, yf_specs = 0, (yf,), [row(HG_W)]
    return pl.pallas_call(
        functools.partial(_out_even_body, n_lo),
        grid=(b, l // tm),
        in_specs=yf_specs + [row(HG_W), row(HG_W), row(HG_W), row(d),
                  pl.BlockSpec((None, 6, d), lambda i, t: (i, 0, 0)),
                  pl.BlockSpec(gn.shape, lambda i, t: (0, 0)),
                  pl.BlockSpec(w_out.shape, lambda i, t: (0, 0))],
        out_specs=row(d),
        out_shape=jax.ShapeDtypeStruct((b, l, d), F32),
        compiler_params=_cparams("parallel", "parallel"),
        name="out_even",
    )(*yf, o_f, o_b, g, x, mod, gn, w_out)


def _proj_att_body(x_ref, mod_ref, w_ref, wvt_ref, qg_ref, kg_ref, cos_ref, sin_ref, q_ref, k_ref, vt_ref):
    mod = mod_ref[...]
    h = _modulate(x_ref[...], mod[0:1], mod[1:2])
    cos = cos_ref[...]
    sin = sin_ref[...]
    scale = HEAD_DIM ** -0.5 * math.log2(math.e)
    n_heads = ATT_HEADS + ATT_KV_HEADS
    ps = []
    for i in range(0, n_heads, 2):
        pp = _dot(h, w_ref[:, i * LANE:(i + 2) * LANE])
        ps += [pp[:, :LANE], pp[:, LANE:]]
    ps = [_rms(p) * (qg_ref[...] if i < ATT_HEADS else kg_ref[...]) for i, p in enumerate(ps)]
    ps = [p * cos + pltpu.roll(p, HEAD_DIM // 2, axis=1) * sin for p in ps]
    for i, p in enumerate(ps):
        if i < ATT_HEADS:
            q_ref[:, i * LANE:(i + 1) * LANE] = (p * scale).astype(BF)
        else:
            k_ref[:, (i - ATT_HEADS) * LANE:(i - ATT_HEADS + 1) * LANE] = p.astype(BF)
    vt = _dot_nt(wvt_ref[...], h)
    for hk in range(ATT_KV_HEADS):
        vt_ref[hk, 0:HEAD_DIM, :] = vt[hk * LANE:(hk + 1) * LANE].astype(BF)
        vt_ref[hk, HEAD_DIM:, :] = jnp.ones((ATT_ONES_ROWS, vt.shape[1]), BF)


def _proj_att(x, mod, w_qk, w_vt, qg, kg, cos, sin):
    b, l, d = x.shape
    tm = _row_tile(l, PROJ_ATT_TILE)
    qw, kw = ATT_HEADS * LANE, ATT_KV_HEADS * LANE
    vt_rows = HEAD_DIM + ATT_ONES_ROWS
    row = lambda width: pl.BlockSpec((None, tm, width), lambda i, t: (i, t, 0))
    const = lambda a: pl.BlockSpec(a.shape, lambda i, t: (0, 0))
    tab = pl.BlockSpec((tm, LANE), lambda i, t: (t, 0))
    return pl.pallas_call(
        _proj_att_body,
        grid=(b, l // tm),
        in_specs=[row(d), pl.BlockSpec((None, 6, d), lambda i, t: (i, 0, 0)),
                  const(w_qk), const(w_vt), const(qg), const(kg), tab, tab],
        out_specs=[row(qw), row(kw),
                   pl.BlockSpec((None, ATT_KV_HEADS, vt_rows, tm), lambda i, t: (i, 0, 0, t))],
        out_shape=[jax.ShapeDtypeStruct((b, l, qw), BF), jax.ShapeDtypeStruct((b, l, kw), BF),
                   jax.ShapeDtypeStruct((b, ATT_KV_HEADS, vt_rows, l), BF)],
        compiler_params=_cparams("parallel", "parallel"),
        name="proj_att",
    )(x, mod, w_qk, w_vt, qg, kg, cos, sin)


@functools.lru_cache(maxsize=None)
def _rope_tables(l):
    t = np.arange(l)
    row = (t // GRID_W).astype(np.float64)
    col = (t % GRID_W).astype(np.float64)
    n_freq = HEAD_DIM // 4
    freqs = ROPE_THETA ** (-np.arange(n_freq, dtype=np.float64) / n_freq)
    ang = np.concatenate([row[:, None] * freqs, col[:, None] * freqs], axis=-1)
    cos, sin = np.cos(ang), np.sin(ang)
    return (np.concatenate([cos, cos], axis=-1).astype(np.float32),
            np.concatenate([-sin, sin], axis=-1).astype(np.float32))


ATT_KV_BLOCK = 256
ATT_ONES_ROWS = 16
ATT_Q_TILE = 512


def _attn_body(n_src, q_ref, *refs):
    o_ref = refs[2 * n_src]
    tq = q_ref.shape[0]
    n = ATT_GROUP * tq
    vt_rows = refs[1].shape[0]
    q = q_ref[...]
    qs = jnp.concatenate([q[:, g * LANE:(g + 1) * LANE] for g in range(ATT_GROUP)], axis=0)
    blocks = []
    for i in range(n_src):
        k_ref, vt_ref = refs[2 * i], refs[2 * i + 1]
        lk = k_ref.shape[0]
        bk = min(ATT_KV_BLOCK, lk)
        blocks += [(k_ref, vt_ref, j * bk, bk) for j in range(lk // bk)]
    to_score = iter(blocks)
    queue = []

    def refill():
        while len(queue) < 2:
            blk = next(to_score, None)
            if blk is None:
                return
            k_ref, _, off, bk = blk
            queue.append(_dot_nt(k_ref[off:off + bk, :], qs))

    m = jnp.full((1, n), -1e30, F32)
    acc = jnp.zeros((vt_rows, n), F32)
    def accumulate(acc, blk, alpha, p):
        _, vt_ref, off, bk = blk
        return alpha * acc + _dot(vt_ref[:, off:off + bk], p)

    refill()
    pending = None
    for j in range(len(blocks)):
        s = queue.pop(0)
        refill()
        m_new = jnp.maximum(m, jnp.max(s, axis=0, keepdims=True))
        alpha = jnp.exp2(m - m_new)
        p = jnp.exp2((s - m_new).astype(BF))
        m = m_new
        if pending is not None:
            acc = accumulate(acc, *pending)
        pending = (blocks[j], alpha, p)
    acc = accumulate(acc, *pending)

    o = acc[:HEAD_DIM] * (1.0 / acc[HEAD_DIM:HEAD_DIM + 1])
    for g in range(ATT_GROUP):
        o_ref[:, g * LANE:(g + 1) * LANE] = o[:, g * tq:(g + 1) * tq].T.astype(BF)


def _attention(q, kvs, tq):
    b, lq, qw = q.shape
    gw = ATT_GROUP * LANE
    in_specs = [pl.BlockSpec((None, tq, gw), lambda i, h, t: (i, t, h))]
    args = [q]
    for k, vt in kvs:
        lk = k.shape[1]
        in_specs.append(pl.BlockSpec((None, lk, LANE), lambda i, h, t: (i, 0, h)))
        in_specs.append(pl.BlockSpec((None, None, vt.shape[2], lk), lambda i, h, t: (i, h, 0, 0)))
        args += [k, vt]
    return pl.pallas_call(
        functools.partial(_attn_body, len(kvs)),
        grid=(b, ATT_KV_HEADS, lq // tq),
        in_specs=in_specs,
        out_specs=pl.BlockSpec((None, tq, gw), lambda i, h, t: (i, t, h)),
        out_shape=jax.ShapeDtypeStruct((b, lq, qw), BF),
        compiler_params=_cparams("parallel", "parallel", "parallel"),
        name="attention",
    )(*args)


def _out_att_body(o_ref, x_ref, mod_ref, w_ref, y_ref):
    y_ref[...] = x_ref[...] + mod_ref[2:3, :] * _dot(o_ref[...], w_ref[...])


def _out_att(o, x, mod, w_out):
    b, l, d = x.shape
    tm = _row_tile(l, ROW_TILE)
    row = lambda width: pl.BlockSpec((None, tm, width), lambda i, t: (i, t, 0))
    return pl.pallas_call(
        _out_att_body,
        grid=(b, l // tm),
        in_specs=[row(o.shape[-1]), row(d), pl.BlockSpec((None, 6, d), lambda i, t: (i, 0, 0)),
                  pl.BlockSpec(w_out.shape, lambda i, t: (0, 0))],
        out_specs=row(d),
        out_shape=jax.ShapeDtypeStruct((b, l, d), F32),
        compiler_params=_cparams("parallel", "parallel"),
        name="out_att",
    )(o, x, mod, w_out)


def _ffn_body(on_grid, ck, has_final, *refs):
    if on_grid:
        xp_ref, xm_ref, xn_ref = refs[:3]
        refs = refs[3:]
    else:
        xm_ref = refs[0]
        refs = refs[1:]
    mod_ref, wu_ref, cw_ref, cb_ref, wd_ref = refs[:5]
    fg_ref = refs[5] if has_final else None
    o_ref, h_buf = refs[-2:]
    tm = xm_ref.shape[0]
    d_ff = wd_ref.shape[0]
    bounds = list(range(0, d_ff, ck)) + [d_ff]
    n_chunk = len(bounds) - 1
    cols = lambda c, base=0: slice(base + bounds[c], base + bounds[c + 1])
    halo = GRID_W if on_grid else 0
    rows = tm + 2 * halo
    mod = mod_ref[...]
    md = lambda x: _modulate(x, mod[3:4], mod[4:5])
    if on_grid:
        h_buf[0:halo, :] = md(xp_ref[...])
        h_buf[halo:halo + tm, :] = md(xm_ref[...])
        h_buf[halo + tm:rows, :] = md(xn_ref[...])
    else:
        h_buf[...] = md(xm_ref[...])
    if on_grid:
        t = pl.program_id(1)
        nt = pl.num_programs(1)

    @functools.lru_cache(maxsize=None)
    def neighbour_masks(width):
        r = lax.broadcasted_iota(jnp.int32, (rows, width), 0)
        if on_grid:
            col = r & (GRID_W - 1)
            return col != 0, col != GRID_W - 1
        return r != 0, r != rows - 1

    hm_sl = slice(halo, halo + tm)

    def up(c):
        gate = _dot(h_buf[...], wu_ref[:, cols(c)])
        val = _dot(h_buf[hm_sl, :], wu_ref[:, cols(c, d_ff)])
        return gate, val

    def down(c, gate, val):
        if on_grid:
            gate = jnp.concatenate([jnp.where(t > 0, gate[:halo], 0.0), gate[halo:halo + tm],
                                    jnp.where(t < nt - 1, gate[halo + tm:], 0.0)], axis=0)
        has_left, has_right = neighbour_masks(gate.shape[1])
        gl = jnp.where(has_left, pltpu.roll(gate, 1, axis=0), 0.0)
        gr = jnp.where(has_right, pltpu.roll(gate, rows - 1, axis=0), 0.0)
        cw = cw_ref[:, cols(c)]
        if on_grid:
            conv = None
            for dy in range(3):
                sl = slice(dy * halo, dy * halo + tm)
                term = (cw[3 * dy:3 * dy + 1] * gl[sl] + cw[3 * dy + 1:3 * dy + 2] * gate[sl]
                        + cw[3 * dy + 2:3 * dy + 3] * gr[sl])
                conv = term if conv is None else conv + term
        else:
            conv = cw[3:4] * gl + cw[4:5] * gate + cw[5:6] * gr
        return (_silu(conv + cb_ref[:, cols(c)]) * val).astype(BF)

    nxt = up(0)
    acc = None
    acts = []
    for c in range(n_chunk):
        cur = nxt
        if c + 1 < n_chunk:
            nxt = up(c + 1)
        acts.append(down(c, *cur))
        if len(acts) == FFN_DOWN_GROUP or c + 1 == n_chunk:
            c0 = c + 1 - len(acts)
            part = _dot(jnp.concatenate(acts, axis=1), wd_ref[bounds[c0]:bounds[c + 1], :])
            acc = part if acc is None else acc + part
            acts = []
    y = xm_ref[...] + mod[5:6] * acc
    if has_final:
        y = _rms(y) * fg_ref[...]
    o_ref[...] = y


def _ffn(x, mod, wu, cw, cb, wd, on_grid, final_g=None):
    b, l, d = x.shape
    ck = min(FFN_CHUNK, wd.shape[0])
    extra = [] if final_g is None else [final_g.reshape(1, d)]
    const = lambda a: pl.BlockSpec(a.shape, lambda i, t: (0,) * a.ndim, pipeline_mode=pl.Buffered(1))
    modspec = pl.BlockSpec((None, 6, d), lambda i, t: (i, 0, 0))
    if on_grid:
        tm = FFN_TILE
        assert l % tm == 0 and tm % GRID_W == 0
        per = tm // GRID_W
        n_rows = l // GRID_W
        x_specs = [pl.BlockSpec((None, GRID_W, d), lambda i, t: (i, jnp.maximum(t * per - 1, 0), 0)),
                   pl.BlockSpec((None, tm, d), lambda i, t: (i, t, 0)),
                   pl.BlockSpec((None, GRID_W, d), lambda i, t: (i, jnp.minimum((t + 1) * per, n_rows - 1), 0))]
        xs = [x, x, x]
        rows = tm + 2 * GRID_W
    else:
        tm = l
        x_specs = [pl.BlockSpec((None, tm, d), lambda i, t: (i, t, 0))]
        xs = [x]
        rows = tm
    return pl.pallas_call(
        functools.partial(_ffn_body, on_grid, ck, final_g is not None),
        grid=(b, l // tm),
        in_specs=(x_specs + [modspec, const(wu), const(cw), const(cb), const(wd)]
                  + [const(a) for a in extra]),
        out_specs=pl.BlockSpec((None, tm, d), lambda i, t: (i, t, 0)),
        out_shape=jax.ShapeDtypeStruct((b, l, d), F32),
        scratch_shapes=[pltpu.VMEM((rows, d), BF)],
        compiler_params=_cparams("parallel", "parallel"),
        name="ffn_grid" if on_grid else "ffn_seq",
    )(*xs, mod, wu, cw, cb, wd, *extra)


def _even_mixer(x, ctx, ml, mc, w_in, w_out, lb_logits, gn, e, need_ctx):
    w_in = w_in.astype(BF)
    w_out = w_out.astype(BF)
    gn = gn.reshape(1, -1)
    lbl = lb_logits.reshape(lb_logits.shape[0], -1).astype(F32)
    a_l, q_l, lf_l, k_l, v_l, g_l = _proj_even(x, ml, w_in, lbl, e)
    a_c, q_c, lf_c, k_c, v_c, g_c = _proj_even(ctx, mc, w_in, lbl, e)
    pc, pl_ = (q_c, lf_c, k_c, v_c), (q_l, lf_l, k_l, v_l)
    (of_c, of_l), (ob_c, ob_l) = _scan(pc, pl_)
    mix_l = _dft_mix_half if x.shape[1] % (2 * max(DFT_TILE, ROW_TILE)) == 0 else _dft_mix
    x = _out_even(mix_l(a_l), of_l, ob_l, g_l, x, ml, gn, w_out)
    if need_ctx:
        ctx = _out_even(_dft_mix(a_c), of_c, ob_c, g_c, ctx, mc, gn, w_out)
    return x, ctx


def _att_mixer(x, ctx, ml, mc, w_qkv, qn_g, kn_g, w_out, need_ctx):
    l, lc = x.shape[1], ctx.shape[1]
    perm = np.concatenate([np.arange(0, HEAD_DIM, 2), np.arange(1, HEAD_DIM, 2)])
    n_qk = ATT_HEADS + ATT_KV_HEADS
    cols = np.concatenate([h * HEAD_DIM + perm for h in range(n_qk)])
    w_qk = w_qkv[:, cols].astype(BF)
    w_vt = w_qkv[:, n_qk * HEAD_DIM:].T.astype(BF)
    w_out = w_out.astype(BF)
    qg = qn_g[perm].reshape(1, -1)
    kg = kn_g[perm].reshape(1, -1)
    cos_l, sin_l = (jnp.asarray(t) for t in _rope_tables(l))
    cos_c, sin_c = jnp.ones((lc, HEAD_DIM), F32), jnp.zeros((lc, HEAD_DIM), F32)
    q_l, k_l, vt_l = _proj_att(x, ml, w_qk, w_vt, qg, kg, cos_l, sin_l)
    q_c, k_c, vt_c = _proj_att(ctx, mc, w_qk, w_vt, qg, kg, cos_c, sin_c)
    x = _out_att(_attention(q_l, [(k_l, vt_l), (k_c, vt_c)], min(l, ATT_Q_TILE)), x, ml, w_out)
    if need_ctx:
        ctx = _out_att(_attention(q_c, [(k_c, vt_c)], lc), ctx, mc, w_out)
    return x, ctx


def _ffn_layer(x, ctx, ml, mc, w_up, conv_w, conv_b, w_down, need_ctx, final_g=None):
    d = x.shape[-1]
    d_ff = w_down.shape[0]
    wu = w_up.astype(BF)
    cw = conv_w.reshape(9, d_ff)
    cb = conv_b.reshape(1, d_ff)
    wd = w_down.astype(BF)
    x = _ffn(x, ml, wu, cw, cb, wd, True, final_g)
    if need_ctx:
        ctx = _ffn(ctx, mc, wu, cw, cb, wd, False)
    return x, ctx


def _mod_vectors(c, c_ctx, w_mod, b_mod):
    b, d = c.shape
    depth = w_mod.shape[0]
    pad = (-(b + 1)) % 8
    cc = jnp.concatenate([c, c_ctx[None, :], jnp.zeros((pad, d), F32)], axis=0)
    mods = _modulation(cc, w_mod, b_mod)
    mod_lat = mods[:, :b].reshape(depth, b, 6, d)
    mod_ctx = jnp.broadcast_to(mods[:, b].reshape(depth, 1, 6, d), (depth, b, 6, d))
    return mod_lat, mod_ctx


def kernel(x, c, ctx, c_ctx, w_mod, b_mod, w_in_ab, w_out_ab, hg_lb_logits, hg_norm_g, w_qkv, q_norm_g,
           k_norm_g, w_out_att, w_up, conv_w, conv_b, w_down, final_norm_g):
    depth = w_mod.shape[0]
    mod_lat, mod_ctx = _mod_vectors(c, c_ctx, w_mod, b_mod)
    for layer in range(depth):
        need_ctx = layer != depth - 1
        ml, mc = mod_lat[layer], mod_ctx[layer]
        if layer % 2 == 0:
            e = layer // 2
            x, ctx = _even_mixer(x, ctx, ml, mc, w_in_ab[e], w_out_ab[e], hg_lb_logits, hg_norm_g[e], e, need_ctx)
        else:
            o = layer // 2
            x, ctx = _att_mixer(x, ctx, ml, mc, w_qkv[o], q_norm_g[o], k_norm_g[o], w_out_att[o], need_ctx)
        x, ctx = _ffn_layer(x, ctx, ml, mc, w_up[layer], conv_w[layer], conv_b[layer], w_down[layer], need_ctx,
                            None if need_ctx else final_norm_g)
    return x
```

```python
import functools
import math

import numpy as np
import jax
import jax.numpy as jnp
from jax import lax
from jax.experimental import pallas as pl
from jax.experimental.pallas import tpu as pltpu

F32 = jnp.float32
BF = jnp.bfloat16

EPS = 1e-6
GRID_W = 64
LANE = 128
SUBLANES = 8
HEAD_DIM = 128
FNET_GROUPS = 4
HG_HEADS = 4
HG_W = HG_HEADS * HEAD_DIM
ATT_HEADS = 8
ATT_KV_HEADS = 2
ATT_GROUP = ATT_HEADS // ATT_KV_HEADS
ROPE_THETA = 10000.0
SCAN_CHUNK = 128
ROW_TILE = 1024
FFN_TILE = 512
PROJ_ATT_TILE = 512
FFN_CHUNK = 256
FFN_DOWN_GROUP = 6
VMEM_LIMIT = 56 * 1024 * 1024


def _cparams(*sem):
    return pltpu.CompilerParams(dimension_semantics=sem, vmem_limit_bytes=VMEM_LIMIT)


def _dot(a, b):
    return jnp.dot(a, b, preferred_element_type=F32)


def _dot_nt(a, b):
    return lax.dot_general(a, b, (((1,), (1,)), ((), ())), preferred_element_type=F32)


def _dot_tn(a, b):
    return lax.dot_general(a, b, (((0,), (0,)), ((), ())), preferred_element_type=F32)


def _rms(x):
    return x * lax.rsqrt(jnp.mean(x * x, axis=-1, keepdims=True) + EPS)


def _silu(x):
    return x * jax.nn.sigmoid(x)


def _modulate(x, shift, scale):
    return (_rms(x) * (1.0 + scale) + shift).astype(BF)


def _row_tile(n, pref):
    return pref if n % pref == 0 else n


def _mod_body(c_ref, w_ref, b_ref, o_ref):
    s = _silu(c_ref[...]).astype(BF)
    o_ref[...] = _dot(s, w_ref[...].astype(BF)) + b_ref[...]


def _modulation(cc, w_mod, b_mod):
    depth, d, six_d = w_mod.shape
    r = cc.shape[0]
    return pl.pallas_call(
        _mod_body,
        grid=(depth, six_d // d),
        in_specs=[pl.BlockSpec((r, d), lambda l, j: (0, 0)),
                  pl.BlockSpec((None, d, d), lambda l, j: (l, 0, j)),
                  pl.BlockSpec((None, 1, d), lambda l, j: (l, 0, j))],
        out_specs=pl.BlockSpec((None, r, d), lambda l, j: (l, 0, j)),
        out_shape=jax.ShapeDtypeStruct((depth, r, six_d), F32),
        compiler_params=_cparams("parallel", "parallel"),
        name="modulation",
    )(cc, w_mod, b_mod.reshape(depth, 1, six_d))


def _proj_even_body(e, n_even, x_ref, mod_ref, w_ref, lbl_ref, a_ref, q_ref, lf_ref, k_ref, v_ref, g_ref):
    mod = mod_ref[...]
    h = _modulate(x_ref[...], mod[0:1], mod[1:2])
    rows = [lbl_ref[i:i + 1, :] for i in range(n_even)]
    mx = functools.reduce(jnp.maximum, rows)
    ex = [jnp.exp(r - mx) for r in rows]
    den = functools.reduce(lambda a, b: a + b, ex)
    lb = jnp.zeros_like(mx)
    for i in range(1, e + 1):
        lb = lb + ex[i] / den
    w = HG_W
    a_ref[...] = _dot(h, w_ref[:, 0:w]).astype(BF)
    q_ref[...] = _silu(_dot(h, w_ref[:, w:2 * w])).astype(BF)
    for d in range(2):
        z = _dot(h, w_ref[:, (2 + d) * w:(3 + d) * w])
        lbd = lb[:, d * w:(d + 1) * w]
        f = lbd + (1.0 - lbd) * jax.nn.sigmoid(z)
        lf_ref[:, d * w:(d + 1) * w] = jnp.log2(f)
        k_ref[:, d * w:(d + 1) * w] = (1.0 - f).astype(BF)
    v_ref[...] = _dot(h, w_ref[:, 4 * w:5 * w]).astype(BF)
    g_ref[...] = _dot(h, w_ref[:, 5 * w:6 * w]).astype(BF)


def _proj_even(x, mod, w_in, lb_logits, e):
    b, l, d = x.shape
    n_even = lb_logits.shape[0]
    tm = _row_tile(l, ROW_TILE)
    w = HG_W
    row = lambda width: pl.BlockSpec((None, tm, width), lambda i, t: (i, t, 0))
    shp = lambda width, dt: jax.ShapeDtypeStruct((b, l, width), dt)
    return pl.pallas_call(
        functools.partial(_proj_even_body, e, n_even),
        grid=(b, l // tm),
        in_specs=[row(d),
                  pl.BlockSpec((None, 6, d), lambda i, t: (i, 0, 0)),
                  pl.BlockSpec(w_in.shape, lambda i, t: (0, 0)),
                  pl.BlockSpec(lb_logits.shape, lambda i, t: (0, 0))],
        out_specs=[row(w), row(w), row(2 * w), row(2 * w), row(w), row(w)],
        out_shape=[shp(w, BF), shp(w, BF), shp(2 * w, F32), shp(2 * w, BF), shp(w, BF), shp(w, BF)],
        compiler_params=_cparams("parallel", "parallel"),
        name="proj_even",
    )(x, mod, w_in, lb_logits)


@functools.lru_cache(maxsize=None)
def _dft_tables(n):
    idx = np.arange(n, dtype=np.int64)
    ang = 2.0 * np.pi * ((idx[:, None] * idx[None, :]) % n).astype(np.float64) / n
    s = 1.0 / math.sqrt(n)
    return (np.cos(ang) * s).astype(np.float32), (np.sin(ang) * s).astype(np.float32)


def _dft_body(c_ref, s_ref, a_ref, cs_ref, o_ref):
    a = a_ref[...]
    p = _dot(c_ref[...], a).astype(BF)
    q = _dot(s_ref[...], a).astype(BF)
    cs = cs_ref[...]
    for g in range(FNET_GROUPS):
        sl = slice(g * LANE, (g + 1) * LANE)
        pq = jnp.concatenate([p[:, sl], q[:, sl]], axis=1)
        o_ref[:, sl] = _dot(pq, cs).astype(BF)


def _dft_mix(a):
    b, l, w = a.shape
    cl, sl_ = _dft_tables(l)
    cc, sc = _dft_tables(LANE)
    cos_l = jnp.asarray(cl).astype(BF)
    sin_l = jnp.asarray(sl_).astype(BF)
    cs = jnp.concatenate([jnp.asarray(cc), -jnp.asarray(sc)], axis=0).astype(BF)
    tm = _row_tile(l, ROW_TILE)
    return pl.pallas_call(
        _dft_body,
        grid=(l // tm, b),
        in_specs=[pl.BlockSpec((tm, l), lambda m, i: (m, 0)),
                  pl.BlockSpec((tm, l), lambda m, i: (m, 0)),
                  pl.BlockSpec((None, l, w), lambda m, i: (i, 0, 0)),
                  pl.BlockSpec(cs.shape, lambda m, i: (0, 0))],
        out_specs=pl.BlockSpec((None, tm, w), lambda m, i: (i, m, 0)),
        out_shape=jax.ShapeDtypeStruct((b, l, w), BF),
        compiler_params=_cparams("parallel", "parallel"),
        name="dft_mix",
    )(cos_l, sin_l, a, cs)


DFT_TILE = 512
DFT_EXTRA = 16


@functools.lru_cache(maxsize=None)
def _dft_half_tables(l):
    cl, sl_ = _dft_tables(l)
    t, e = DFT_TILE, DFT_EXTRA
    n = l // (2 * t)
    cos = np.stack([cl[j * t:j * t + t + e] for j in range(n)])
    sin = np.stack([sl_[j * t:j * t + t + e] for j in range(n)])
    rev = np.zeros((t, t + e), np.float32)
    rev[np.arange(t), t - np.arange(t)] = 1.0
    return cos, sin, rev


def _dft_half_body(c_ref, s_ref, a_ref, cs_ref, rev_ref, lo_ref, hi_ref):
    t = lo_ref.shape[0]
    a = a_ref[...]
    p = _dot(c_ref[...], a).astype(BF)
    q = _dot(s_ref[...], a).astype(BF)
    cs = cs_ref[...]
    mirrored = []
    for g in range(FNET_GROUPS):
        sl = slice(g * LANE, (g + 1) * LANE)
        y = _dot(jnp.concatenate([p[:, sl], q[:, sl]], axis=1), cs)
        lo_ref[:, sl] = y[:t, :LANE].astype(BF)
        mirrored.append(y[:, LANE:].astype(BF))
    hi_ref[...] = _dot(rev_ref[...], jnp.concatenate(mirrored, axis=1)).astype(BF)


def _dft_mix_half(a):
    b, l, w = a.shape
    t, e = DFT_TILE, DFT_EXTRA
    n = l // (2 * t)
    cos, sin, rev = (jnp.asarray(x).astype(BF) for x in _dft_half_tables(l))
    cc, sc = (jnp.asarray(x) for x in _dft_tables(LANE))
    cs = jnp.concatenate([jnp.concatenate([cc, -sc], axis=0), jnp.concatenate([cc, sc], axis=0)], axis=1).astype(BF)
    half = jax.ShapeDtypeStruct((b, l // 2, w), BF)
    lo, hi = pl.pallas_call(
        _dft_half_body,
        grid=(n, b),
        in_specs=[pl.BlockSpec((None, t + e, l), lambda j, i: (j, 0, 0)),
                  pl.BlockSpec((None, t + e, l), lambda j, i: (j, 0, 0)),
                  pl.BlockSpec((None, l, w), lambda j, i: (i, 0, 0)),
                  pl.BlockSpec(cs.shape, lambda j, i: (0, 0)),
                  pl.BlockSpec(rev.shape, lambda j, i: (0, 0))],
        out_specs=[pl.BlockSpec((None, t, w), lambda j, i: (i, j, 0)),
                   pl.BlockSpec((None, t, w), lambda j, i: (i, n - 1 - j, 0))],
        out_shape=[half, half],
        compiler_params=_cparams("parallel", "parallel"),
        name="dft_mix_half",
    )(cos, sin, a, cs, rev)
    return lo, hi


def _scan_chunk(rev, q_ref, k_ref, lf_ref, v_ref, o_ref, st_ref):
    c, w = lf_ref.shape
    nlev = c.bit_length() - 1
    lf = lf_ref[...]
    r = lax.broadcasted_iota(jnp.int32, (c, w), 0)
    pos = (c - 1 - r) if rev else r

    def tile_roll(x, sh):
        return pltpu.roll(x.reshape(c // SUBLANES, SUBLANES, w), sh, axis=1).reshape(c, w)

    def from_prev(x, sh):
        return tile_roll(x, (SUBLANES - sh) if rev else sh)

    def from_next(x, sh):
        return tile_roll(x, sh if rev else (SUBLANES - sh))

    def shift_tiles(x, sh):
        zero = jnp.zeros((sh, w), F32)
        return jnp.concatenate([x[sh:], zero], axis=0) if rev else jnp.concatenate([zero, x[:c - sh]], axis=0)

    def neg_abs(x):
        return pltpu.bitcast(pltpu.bitcast(x, jnp.uint32) | jnp.uint32(0x80000000), F32)

    def spread_half(x, sh, low):
        parts = []
        for i in range(c // (2 * sh)):
            first, second = x[2 * i * sh:(2 * i + 1) * sh], x[(2 * i + 1) * sh:(2 * i + 2) * sh]
            src = (second if rev else first) if low else (first if rev else second)
            parts += [src, src]
        return jnp.concatenate(parts, axis=0)

    pos8 = pos & (SUBLANES - 1)
    b = lf
    sh = 1
    while sh < SUBLANES:
        b = b + jnp.where(pos8 >= sh, from_prev(b, sh), 0.0)
        sh *= 2
    last = 0 if rev else SUBLANES - 1
    b3 = b.reshape(c // SUBLANES, SUBLANES, w)
    tot = jnp.broadcast_to(b3[:, last:last + 1, :], b3.shape).reshape(c, w)
    while sh < c:
        tot = tot + shift_tiles(tot, sh)
        sh *= 2
    b = b + shift_tiles(tot, SUBLANES)

    q = q_ref[...]
    k = k_ref[...]
    ti = lax.broadcasted_iota(jnp.int32, (c, c), 0)
    si = lax.broadcasted_iota(jnp.int32, (c, c), 1)
    later = (ti < si) if rev else (ti > si)
    diff = ti ^ si
    att = [jnp.zeros((c, c), F32) for _ in range(HG_HEADS)]
    y = b
    for j in range(nlev):
        sh = 1 << j
        upper = ((pos >> j) & 1) == 1
        bound = spread_half(y, sh, True) if sh % 8 == 0 else jnp.where(upper, from_prev(y, sh), y)
        e = jnp.exp2(neg_abs(b - bound)).astype(BF)
        qm = q * e
        km = k * e
        level = jnp.logical_and((diff >> j) == 1, later)
        for h in range(HG_HEADS):
            sl = slice(h * LANE, (h + 1) * LANE)
            att[h] = jnp.where(level, _dot_nt(qm[:, sl], km[:, sl]), att[h])
        y = spread_half(y, sh, False) if sh % 8 == 0 else jnp.where(upper, y, from_next(y, sh))
    b_last = y
    qe = q * jnp.exp2(b).astype(BF)
    kd = k * jnp.exp2(b_last - b).astype(BF)
    carry = jnp.exp2(b_last[0:1, :])
    v = v_ref[...]
    vf = v.astype(F32)
    heads = [slice(h * LANE, (h + 1) * LANE) for h in range(HG_HEADS)]
    states = [st_ref[h] for h in range(HG_HEADS)]
    inter = [_dot_nt(qe[:, sl], st.astype(BF)) for sl, st in zip(heads, states)]
    intra = [_dot(att[h].astype(BF), v[:, sl]) for h, sl in enumerate(heads)]
    outer = [_dot_tn(v[:, sl], kd[:, sl]) for sl in heads]
    diag = [jnp.sum(q[:, sl].astype(F32) * k[:, sl].astype(F32), axis=-1, keepdims=True) for sl in heads]
    for h, sl in enumerate(heads):
        o_ref[:, sl] = (inter[h] + intra[h] + diag[h] * vf[:, sl]).astype(o_ref.dtype)
    for h, sl in enumerate(heads):
        st_ref[h] = states[h] * carry[:, sl] + outer[h]


def _scan_body(n_ctx, *refs):
    ins, outs, states = refs[:16], refs[16:20], refs[20:]
    s = pl.program_id(1)

    @pl.when(s == 0)
    def _():
        for st_ref in states:
            st_ref[...] = jnp.zeros_like(st_ref)

    def run(stream):
        for d in range(2):
            base = 8 * d + 4 * stream
            _scan_chunk(d == 1, *ins[base:base + 4], outs[2 * d + stream], states[d])

    @pl.when(s < n_ctx)
    def _():
        run(0)

    @pl.when(s >= n_ctx)
    def _():
        run(1)


def _scan(pc, pl_):
    qc, lfc, kc, vc = pc
    ql, lfl, kl, vl = pl_
    b, lc, w = qc.shape
    ll = ql.shape[1]
    c = SCAN_CHUNK
    n_ctx, n_lat = lc // c, ll // c

    def spec(idx, col):
        return pl.BlockSpec((None, c, w), lambda i, s: (i, idx(s), col))

    in_specs, out_specs, args = [], [], []
    for d in range(2):
        if d:
            ci = lambda s: jnp.maximum(n_ctx - 1 - s, 0)
            li = lambda s: n_lat - 1 - jnp.maximum(s - n_ctx, 0)
        else:
            ci = lambda s: jnp.minimum(s, n_ctx - 1)
            li = lambda s: jnp.maximum(s - n_ctx, 0)
        in_specs += [spec(ci, 0), spec(ci, d), spec(ci, d), spec(ci, 0),
                     spec(li, 0), spec(li, d), spec(li, d), spec(li, 0)]
        args += [qc, kc, lfc, vc, ql, kl, lfl, vl]
        out_specs += [spec(ci, 0), spec(li, 0)]
    shp = lambda l: jax.ShapeDtypeStruct((b, l, w), BF)
    of_c, of_l, ob_c, ob_l = pl.pallas_call(
        functools.partial(_scan_body, n_ctx),
        grid=(b, n_ctx + n_lat),
        in_specs=in_specs,
        out_specs=out_specs,
        out_shape=[shp(lc), shp(ll), shp(lc), shp(ll)],
        scratch_shapes=[pltpu.VMEM((HG_HEADS, HEAD_DIM, HEAD_DIM), F32) for _ in range(2)],
        compiler_params=_cparams("parallel", "arbitrary"),
        name="hgrn_scan",
    )(*args)
    return (of_c, of_l), (ob_c, ob_l)


def _out_even_body(n_lo, *refs):
    yf_refs, (of_ref, ob_ref, g_ref, x_ref, mod_ref, gn_ref, w_ref, o_ref) = refs[:-8], refs[-8:]
    if n_lo:
        yf = jnp.where(pl.program_id(1) < n_lo, yf_refs[0][...], yf_refs[1][...])
    else:
        yf = yf_refs[0][...]
    o = of_ref[...].astype(F32) + ob_ref[...].astype(F32)
    g = g_ref[...].astype(F32)
    gn = gn_ref[...]
    parts = []
    for h in range(HG_HEADS):
        sl = slice(h * LANE, (h + 1) * LANE)
        parts.append((_rms(o[:, sl]) * gn[:, sl] * _silu(g[:, sl])).astype(BF))
    gated = jnp.concatenate(parts, axis=1)
    w = HG_W
    y = _dot(yf, w_ref[0:w, :]) + _dot(gated, w_ref[w:2 * w, :])
    o_ref[...] = x_ref[...] + mod_ref[2:3, :] * y


def _out_even(yf, o_f, o_b, g, x, mod, gn, w_out):
    b, l, d = x.shape
    tm = _row_tile(l, ROW_TILE)
    row = lambda width: pl.BlockSpec((None, tm, width), lambda i, t: (i, t, 0))
    if isinstance(yf, tuple):
        assert yf[0].shape[1] % tm == 0
        n_lo = yf[0].shape[1] // tm
        yf_specs = [pl.BlockSpec((None, tm, HG_W), lambda i, t: (i, jnp.minimum(t, n_lo - 1), 0)),
                    pl.BlockSpec((None, tm, HG_W), lambda i, t: (i, jnp.maximum(t - n_lo, 0), 0))]
    else:
        n_lo, yf, yf_specs = 0, (yf,), [row(HG_W)]
    return pl.pallas_call(
        functools.partial(_out_even_body, n_lo),
        grid=(b, l // tm),
        in_specs=yf_specs + [row(HG_W), row(HG_W), row(HG_W), row(d),
                  pl.BlockSpec((None, 6, d), lambda i, t: (i, 0, 0)),
                  pl.BlockSpec(gn.shape, lambda i, t: (0, 0)),
                  pl.BlockSpec(w_out.shape, lambda i, t: (0, 0))],
        out_specs=row(d),
        out_shape=jax.ShapeDtypeStruct((b, l, d), F32),
        compiler_params=_cparams("parallel", "parallel"),
        name="out_even",
    )(*yf, o_f, o_b, g, x, mod, gn, w_out)


def _proj_att_body(x_ref, mod_ref, w_ref, wvt_ref, qg_ref, kg_ref, cos_ref, sin_ref, q_ref, k_ref, vt_ref):
    mod = mod_ref[...]
    h = _modulate(x_ref[...], mod[0:1], mod[1:2])
    cos = cos_ref[...]
    sin = sin_ref[...]
    scale = HEAD_DIM ** -0.5 * math.log2(math.e)
    n_heads = ATT_HEADS + ATT_KV_HEADS
    ps = []
    for i in range(0, n_heads, 2):
        pp = _dot(h, w_ref[:, i * LANE:(i + 2) * LANE])
        ps += [pp[:, :LANE], pp[:, LANE:]]
    ps = [_rms(p) * (qg_ref[...] if i < ATT_HEADS else kg_ref[...]) for i, p in enumerate(ps)]
    ps = [p * cos + pltpu.roll(p, HEAD_DIM // 2, axis=1) * sin for p in ps]
    for i, p in enumerate(ps):
        if i < ATT_HEADS:
            q_ref[:, i * LANE:(i + 1) * LANE] = (p * scale).astype(BF)
        else:
            k_ref[:, (i - ATT_HEADS) * LANE:(i - ATT_HEADS + 1) * LANE] = p.astype(BF)
    vt = _dot_nt(wvt_ref[...], h)
    for hk in range(ATT_KV_HEADS):
        vt_ref[hk, 0:HEAD_DIM, :] = vt[hk * LANE:(hk + 1) * LANE].astype(BF)
        vt_ref[hk, HEAD_DIM:, :] = jnp.ones((ATT_ONES_ROWS, vt.shape[1]), BF)


def _proj_att(x, mod, w_qk, w_vt, qg, kg, cos, sin):
    b, l, d = x.shape
    tm = _row_tile(l, PROJ_ATT_TILE)
    qw, kw = ATT_HEADS * LANE, ATT_KV_HEADS * LANE
    vt_rows = HEAD_DIM + ATT_ONES_ROWS
    row = lambda width: pl.BlockSpec((None, tm, width), lambda i, t: (i, t, 0))
    const = lambda a: pl.BlockSpec(a.shape, lambda i, t: (0, 0))
    tab = pl.BlockSpec((tm, LANE), lambda i, t: (t, 0))
    return pl.pallas_call(
        _proj_att_body,
        grid=(b, l // tm),
        in_specs=[row(d), pl.BlockSpec((None, 6, d), lambda i, t: (i, 0, 0)),
                  const(w_qk), const(w_vt), const(qg), const(kg), tab, tab],
        out_specs=[row(qw), row(kw),
                   pl.BlockSpec((None, ATT_KV_HEADS, vt_rows, tm), lambda i, t: (i, 0, 0, t))],
        out_shape=[jax.ShapeDtypeStruct((b, l, qw), BF), jax.ShapeDtypeStruct((b, l, kw), BF),
                   jax.ShapeDtypeStruct((b, ATT_KV_HEADS, vt_rows, l), BF)],
        compiler_params=_cparams("parallel", "parallel"),
        name="proj_att",
    )(x, mod, w_qk, w_vt, qg, kg, cos, sin)


@functools.lru_cache(maxsize=None)
def _rope_tables(l):
    t = np.arange(l)
    row = (t // GRID_W).astype(np.float64)
    col = (t % GRID_W).astype(np.float64)
    n_freq = HEAD_DIM // 4
    freqs = ROPE_THETA ** (-np.arange(n_freq, dtype=np.float64) / n_freq)
    ang = np.concatenate([row[:, None] * freqs, col[:, None] * freqs], axis=-1)
    cos, sin = np.cos(ang), np.sin(ang)
    return (np.concatenate([cos, cos], axis=-1).astype(np.float32),
            np.concatenate([-sin, sin], axis=-1).astype(np.float32))


ATT_KV_BLOCK = 256
ATT_ONES_ROWS = 16
ATT_Q_TILE = 512


def _attn_body(n_src, q_ref, *refs):
    o_ref = refs[2 * n_src]
    tq = q_ref.shape[0]
    n = ATT_GROUP * tq
    vt_rows = refs[1].shape[0]
    q = q_ref[...]
    qs = jnp.concatenate([q[:, g * LANE:(g + 1) * LANE] for g in range(ATT_GROUP)], axis=0)
    blocks = []
    for i in range(n_src):
        k_ref, vt_ref = refs[2 * i], refs[2 * i + 1]
        lk = k_ref.shape[0]
        bk = min(ATT_KV_BLOCK, lk)
        blocks += [(k_ref, vt_ref, j * bk, bk) for j in range(lk // bk)]
    to_score = iter(blocks)
    queue = []

    def refill():
        while len(queue) < 2:
            blk = next(to_score, None)
            if blk is None:
                return
            k_ref, _, off, bk = blk
            queue.append(_dot_nt(k_ref[off:off + bk, :], qs))

    m = jnp.full((1, n), -1e30, F32)
    acc = jnp.zeros((vt_rows, n), F32)
    def accumulate(acc, blk, alpha, p):
        _, vt_ref, off, bk = blk
        return alpha * acc + _dot(vt_ref[:, off:off + bk], p)

    refill()
    pending = None
    for j in range(len(blocks)):
        s = queue.pop(0)
        refill()
        m_new = jnp.maximum(m, jnp.max(s, axis=0, keepdims=True))
        alpha = jnp.exp2(m - m_new)
        p = jnp.exp2((s - m_new).astype(BF))
        m = m_new
        if pending is not None:
            acc = accumulate(acc, *pending)
        pending = (blocks[j], alpha, p)
    acc = accumulate(acc, *pending)

    o = acc[:HEAD_DIM] * (1.0 / acc[HEAD_DIM:HEAD_DIM + 1])
    for g in range(ATT_GROUP):
        o_ref[:, g * LANE:(g + 1) * LANE] = o[:, g * tq:(g + 1) * tq].T.astype(BF)


def _attention(q, kvs, tq):
    b, lq, qw = q.shape
    gw = ATT_GROUP * LANE
    in_specs = [pl.BlockSpec((None, tq, gw), lambda i, h, t: (i, t, h))]
    args = [q]
    for k, vt in kvs:
        lk = k.shape[1]
        in_specs.append(pl.BlockSpec((None, lk, LANE), lambda i, h, t: (i, 0, h)))
        in_specs.append(pl.BlockSpec((None, None, vt.shape[2], lk), lambda i, h, t: (i, h, 0, 0)))
        args += [k, vt]
    return pl.pallas_call(
        functools.partial(_attn_body, len(kvs)),
        grid=(b, ATT_KV_HEADS, lq // tq),
        in_specs=in_specs,
        out_specs=pl.BlockSpec((None, tq, gw), lambda i, h, t: (i, t, h)),
        out_shape=jax.ShapeDtypeStruct((b, lq, qw), BF),
        compiler_params=_cparams("parallel", "parallel", "parallel"),
        name="attention",
    )(*args)


def _out_att_body(o_ref, x_ref, mod_ref, w_ref, y_ref):
    y_ref[...] = x_ref[...] + mod_ref[2:3, :] * _dot(o_ref[...], w_ref[...])


def _out_att(o, x, mod, w_out):
    b, l, d = x.shape
    tm = _row_tile(l, ROW_TILE)
    row = lambda width: pl.BlockSpec((None, tm, width), lambda i, t: (i, t, 0))
    return pl.pallas_call(
        _out_att_body,
        grid=(b, l // tm),
        in_specs=[row(o.shape[-1]), row(d), pl.BlockSpec((None, 6, d), lambda i, t: (i, 0, 0)),
                  pl.BlockSpec(w_out.shape, lambda i, t: (0, 0))],
        out_specs=row(d),
        out_shape=jax.ShapeDtypeStruct((b, l, d), F32),
        compiler_params=_cparams("parallel", "parallel"),
        name="out_att",
    )(o, x, mod, w_out)


def _ffn_body(on_grid, ck, has_final, *refs):
    if on_grid:
        xp_ref, xm_ref, xn_ref = refs[:3]
        refs = refs[3:]
    else:
        xm_ref = refs[0]
        refs = refs[1:]
    mod_ref, wu_ref, cw_ref, cb_ref, wd_ref = refs[:5]
    fg_ref = refs[5] if has_final else None
    o_ref, h_buf = refs[-2:]
    tm = xm_ref.shape[0]
    d_ff = wd_ref.shape[0]
    bounds = list(range(0, d_ff, ck)) + [d_ff]
    n_chunk = len(bounds) - 1
    cols = lambda c, base=0: slice(base + bounds[c], base + bounds[c + 1])
    halo = GRID_W if on_grid else 0
    rows = tm + 2 * halo
    mod = mod_ref[...]
    md = lambda x: _modulate(x, mod[3:4], mod[4:5])
    if on_grid:
        h_buf[0:halo, :] = md(xp_ref[...])
        h_buf[halo:halo + tm, :] = md(xm_ref[...])
        h_buf[halo + tm:rows, :] = md(xn_ref[...])
    else:
        h_buf[...] = md(xm_ref[...])
    if on_grid:
        t = pl.program_id(1)
        nt = pl.num_programs(1)

    @functools.lru_cache(maxsize=None)
    def neighbour_masks(width):
        r = lax.broadcasted_iota(jnp.int32, (rows, width), 0)
        if on_grid:
            col = r & (GRID_W - 1)
            return col != 0, col != GRID_W - 1
        return r != 0, r != rows - 1

    hm_sl = slice(halo, halo + tm)

    def up(c):
        gate = _dot(h_buf[...], wu_ref[:, cols(c)])
        val = _dot(h_buf[hm_sl, :], wu_ref[:, cols(c, d_ff)])
        return gate, val

    def down(c, gate, val):
        if on_grid:
            gate = jnp.concatenate([jnp.where(t > 0, gate[:halo], 0.0), gate[halo:halo + tm],
                                    jnp.where(t < nt - 1, gate[halo + tm:], 0.0)], axis=0)
        has_left, has_right = neighbour_masks(gate.shape[1])
        gl = jnp.where(has_left, pltpu.roll(gate, 1, axis=0), 0.0)
        gr = jnp.where(has_right, pltpu.roll(gate, rows - 1, axis=0), 0.0)
        cw = cw_ref[:, cols(c)]
        if on_grid:
            conv = None
            for dy in range(3):
                sl = slice(dy * halo, dy * halo + tm)
                term = (cw[3 * dy:3 * dy + 1] * gl[sl] + cw[3 * dy + 1:3 * dy + 2] * gate[sl]
                        + cw[3 * dy + 2:3 * dy + 3] * gr[sl])
                conv = term if conv is None else conv + term
        else:
            conv = cw[3:4] * gl + cw[4:5] * gate + cw[5:6] * gr
        return (_silu(conv + cb_ref[:, cols(c)]) * val).astype(BF)

    nxt = up(0)
    acc = None
    acts = []
    for c in range(n_chunk):
        cur = nxt
        if c + 1 < n_chunk:
            nxt = up(c + 1)
        acts.append(down(c, *cur))
        if len(acts) == FFN_DOWN_GROUP or c + 1 == n_chunk:
            c0 = c + 1 - len(acts)
            part = _dot(jnp.concatenate(acts, axis=1), wd_ref[bounds[c0]:bounds[c + 1], :])
            acc = part if acc is None else acc + part
            acts = []
    y = xm_ref[...] + mod[5:6] * acc
    if has_final:
        y = _rms(y) * fg_ref[...]
    o_ref[...] = y


def _ffn(x, mod, wu, cw, cb, wd, on_grid, final_g=None):
    b, l, d = x.shape
    ck = min(FFN_CHUNK, wd.shape[0])
    extra = [] if final_g is None else [final_g.reshape(1, d)]
    const = lambda a: pl.BlockSpec(a.shape, lambda i, t: (0,) * a.ndim, pipeline_mode=pl.Buffered(1))
    modspec = pl.BlockSpec((None, 6, d), lambda i, t: (i, 0, 0))
    if on_grid:
        tm = FFN_TILE
        assert l % tm == 0 and tm % GRID_W == 0
        per = tm // GRID_W
        n_rows = l // GRID_W
        x_specs = [pl.BlockSpec((None, GRID_W, d), lambda i, t: (i, jnp.maximum(t * per - 1, 0), 0)),
                   pl.BlockSpec((None, tm, d), lambda i, t: (i, t, 0)),
                   pl.BlockSpec((None, GRID_W, d), lambda i, t: (i, jnp.minimum((t + 1) * per, n_rows - 1), 0))]
        xs = [x, x, x]
        rows = tm + 2 * GRID_W
    else:
        tm = l
        x_specs = [pl.BlockSpec((None, tm, d), lambda i, t: (i, t, 0))]
        xs = [x]
        rows = tm
    return pl.pallas_call(
        functools.partial(_ffn_body, on_grid, ck, final_g is not None),
        grid=(b, l // tm),
        in_specs=(x_specs + [modspec, const(wu), const(cw), const(cb), const(wd)]
                  + [const(a) for a in extra]),
        out_specs=pl.BlockSpec((None, tm, d), lambda i, t: (i, t, 0)),
        out_shape=jax.ShapeDtypeStruct((b, l, d), F32),
        scratch_shapes=[pltpu.VMEM((rows, d), BF)],
        compiler_params=_cparams("parallel", "parallel"),
        name="ffn_grid" if on_grid else "ffn_seq",
    )(*xs, mod, wu, cw, cb, wd, *extra)


def _even_mixer(x, ctx, ml, mc, w_in, w_out, lb_logits, gn, e, need_ctx):
    w_in = w_in.astype(BF)
    w_out = w_out.astype(BF)
    gn = gn.reshape(1, -1)
    lbl = lb_logits.reshape(lb_logits.shape[0], -1).astype(F32)
    a_l, q_l, lf_l, k_l, v_l, g_l = _proj_even(x, ml, w_in, lbl, e)
    a_c, q_c, lf_c, k_c, v_c, g_c = _proj_even(ctx, mc, w_in, lbl, e)
    pc, pl_ = (q_c, lf_c, k_c, v_c), (q_l, lf_l, k_l, v_l)
    (of_c, of_l), (ob_c, ob_l) = _scan(pc, pl_)
    mix_l = _dft_mix_half if x.shape[1] % (2 * max(DFT_TILE, ROW_TILE)) == 0 else _dft_mix
    x = _out_even(mix_l(a_l), of_l, ob_l, g_l, x, ml, gn, w_out)
    if need_ctx:
        ctx = _out_even(_dft_mix(a_c), of_c, ob_c, g_c, ctx, mc, gn, w_out)
    return x, ctx


def _att_mixer(x, ctx, ml, mc, w_qkv, qn_g, kn_g, w_out, need_ctx):
    l, lc = x.shape[1], ctx.shape[1]
    perm = np.concatenate([np.arange(0, HEAD_DIM, 2), np.arange(1, HEAD_DIM, 2)])
    n_qk = ATT_HEADS + ATT_KV_HEADS
    cols = np.concatenate([h * HEAD_DIM + perm for h in range(n_qk)])
    w_qk = w_qkv[:, cols].astype(BF)
    w_vt = w_qkv[:, n_qk * HEAD_DIM:].T.astype(BF)
    w_out = w_out.astype(BF)
    qg = qn_g[perm].reshape(1, -1)
    kg = kn_g[perm].reshape(1, -1)
    cos_l, sin_l = (jnp.asarray(t) for t in _rope_tables(l))
    cos_c, sin_c = jnp.ones((lc, HEAD_DIM), F32), jnp.zeros((lc, HEAD_DIM), F32)
    q_l, k_l, vt_l = _proj_att(x, ml, w_qk, w_vt, qg, kg, cos_l, sin_l)
    q_c, k_c, vt_c = _proj_att(ctx, mc, w_qk, w_vt, qg, kg, cos_c, sin_c)
    x = _out_att(_attention(q_l, [(k_l, vt_l), (k_c, vt_c)], min(l, ATT_Q_TILE)), x, ml, w_out)
    if need_ctx:
        ctx = _out_att(_attention(q_c, [(k_c, vt_c)], lc), ctx, mc, w_out)
    return x, ctx


def _ffn_layer(x, ctx, ml, mc, w_up, conv_w, conv_b, w_down, need_ctx, final_g=None):
    d = x.shape[-1]
    d_ff = w_down.shape[0]
    wu = w_up.astype(BF)
    cw = conv_w.reshape(9, d_ff)
    cb = conv_b.reshape(1, d_ff)
    wd = w_down.astype(BF)
    x = _ffn(x, ml, wu, cw, cb, wd, True, final_g)
    if need_ctx:
        ctx = _ffn(ctx, mc, wu, cw, cb, wd, False)
    return x, ctx


def _mod_vectors(c, c_ctx, w_mod, b_mod):
    b, d = c.shape
    depth = w_mod.shape[0]
    pad = (-(b + 1)) % 8
    cc = jnp.concatenate([c, c_ctx[None, :], jnp.zeros((pad, d), F32)], axis=0)
    mods = _modulation(cc, w_mod, b_mod)
    mod_lat = mods[:, :b].reshape(depth, b, 6, d)
    mod_ctx = jnp.broadcast_to(mods[:, b].reshape(depth, 1, 6, d), (depth, b, 6, d))
    return mod_lat, mod_ctx


def kernel(x, c, ctx, c_ctx, w_mod, b_mod, w_in_ab, w_out_ab, hg_lb_logits, hg_norm_g, w_qkv, q_norm_g,
           k_norm_g, w_out_att, w_up, conv_w, conv_b, w_down, final_norm_g):
    depth = w_mod.shape[0]
    mod_lat, mod_ctx = _mod_vectors(c, c_ctx, w_mod, b_mod)
    for layer in range(depth):
        need_ctx = layer != depth - 1
        ml, mc = mod_lat[layer], mod_ctx[layer]
        if layer % 2 == 0:
            e = layer // 2
            x, ctx = _even_mixer(x, ctx, ml, mc, w_in_ab[e], w_out_ab[e], hg_lb_logits, hg_norm_g[e], e, need_ctx)
        else:
            o = layer // 2
            x, ctx = _att_mixer(x, ctx, ml, mc, w_qkv[o], q_norm_g[o], k_norm_g[o], w_out_att[o], need_ctx)
        x, ctx = _ffn_layer(x, ctx, ml, mc, w_up[layer], conv_w[layer], conv_b[layer], w_down[layer], need_ctx,
                            None if need_ctx else final_norm_g)
    return x
```
